```python
import jax
import jax.numpy as jnp
from jax import lax
import numpy as np

D_MODEL = 1024
BATCH = 8
SEQ = 2048
DEPTH = 4

GRID_W = 64
CTX_LEN = 256

FNET_GROUPS = 4
FNET_CH = 64
D_FNET = FNET_GROUPS * FNET_CH

MLA_HEADS = 6
MLA_Q_RANK = 256
MLA_KV_RANK = 128
MLA_NOPE = 64
MLA_ROPE = 32
MLA_QK = MLA_NOPE + MLA_ROPE
MLA_V = 64

SWA_HEADS = 6
SWA_KV_HEADS = 2
SWA_HEAD_DIM = 64
WINDOW = 128
BLOCK = 128
Q_BLOCK = 128

D_MIX = D_FNET + MLA_HEADS * MLA_V + SWA_HEADS * SWA_HEAD_DIM
IN_SIZES = (D_FNET, MLA_Q_RANK, MLA_KV_RANK, MLA_ROPE,
            SWA_HEADS * SWA_HEAD_DIM, SWA_KV_HEADS * SWA_HEAD_DIM, SWA_KV_HEADS * SWA_HEAD_DIM)
D_IN = sum(IN_SIZES)
IN_SPLITS = tuple(int(v) for v in np.cumsum(IN_SIZES)[:-1])

N_EXPERTS = 16
N_EXPERT_GROUPS = 4
EXPERTS_PER_GROUP = N_EXPERTS // N_EXPERT_GROUPS
TOP_K = 2
D_EXPERT = 512

ROPE_THETA = 10000.0
EPS = 1e-6

kernel_name = "hybrid_dit_fnet_mla_swa_moe"


def rmsnorm(x, g):
    xf = x.astype(jnp.float32)
    y = xf * lax.rsqrt(jnp.mean(xf * xf, axis=-1, keepdims=True) + EPS)
    return (y * g.astype(jnp.float32)).astype(x.dtype)


def axial_rope_table(rows, cols, d_rot):
    d_axis = d_rot // 2
    inv = ROPE_THETA ** (-jnp.arange(0, d_axis, 2, dtype=jnp.float32) / d_axis)
    ar = rows.astype(jnp.float32)[:, None] * inv
    ac = cols.astype(jnp.float32)[:, None] * inv
    ang = jnp.concatenate([ar, ar, ac, ac], axis=-1)
    return jnp.cos(ang)[:, None, :], jnp.sin(ang)[:, None, :]


def apply_axial_rope(x, cos, sin):
    a1, a2, b1, b2 = jnp.split(x, 4, axis=-1)
    rot = jnp.concatenate([-a2, a1, -b2, b1], axis=-1)
    return (x * cos + rot * sin).astype(x.dtype)


def rope_tail(x, cos, sin):
    return jnp.concatenate([x[..., :MLA_NOPE], apply_axial_rope(x[..., MLA_NOPE:], cos, sin)], axis=-1)


def mixer_inputs(h, w_in, cq_g, ckv_g, w_uq, w_uk, w_uv, mq_g, mk_g, sq_g, sk_g, rope, with_queries=True):
    B, N, _ = h.shape
    p = h @ w_in
    u_f, cq, ckv, k_r, q_s, k_s, v_s = jnp.split(p, IN_SPLITS, axis=-1)
    ckv = rmsnorm(ckv, ckv_g)
    k_nope = (ckv @ w_uk).reshape(B, N, MLA_HEADS, MLA_NOPE)
    v_m = (ckv @ w_uv).reshape(B, N, MLA_HEADS, MLA_V)
    k_rope = jnp.broadcast_to(k_r[:, :, None, :], (B, N, MLA_HEADS, MLA_ROPE))
    k_m = rmsnorm(jnp.concatenate([k_nope, k_rope], axis=-1), mk_g)
    k_s = rmsnorm(k_s.reshape(B, N, SWA_KV_HEADS, SWA_HEAD_DIM), sk_g)
    v_s = v_s.reshape(B, N, SWA_KV_HEADS, SWA_HEAD_DIM)
    if rope is not None:
        cos_m, sin_m, cos_s, sin_s = rope
        k_m = rope_tail(k_m, cos_m, sin_m)
        k_s = apply_axial_rope(k_s, cos_s, sin_s)
    if not with_queries:
        return None, None, k_m, v_m, None, k_s, v_s
    q_m = rmsnorm((rmsnorm(cq, cq_g) @ w_uq).reshape(B, N, MLA_HEADS, MLA_QK), mq_g)
    q_s = rmsnorm(q_s.reshape(B, N, SWA_HEADS, SWA_HEAD_DIM), sq_g)
    if rope is not None:
        q_m = rope_tail(q_m, cos_m, sin_m)
        q_s = apply_axial_rope(q_s, cos_s, sin_s)
    return u_f, q_m, k_m, v_m, q_s, k_s, v_s


def fourier_mix(u, w):
    B, N, _ = u.shape
    ug = u.reshape(B, N, FNET_GROUPS, FNET_CH).astype(jnp.float32)
    f = jnp.fft.fft2(ug, axes=(1, 3), norm='ortho').real.astype(u.dtype)
    return jnp.einsum('bngc,gce->bnge', f, w).reshape(B, N, D_FNET)


def dense_attend_blocks(q, k, v):
    B, S, H, d = q.shape
    nb = S // Q_BLOCK
    qb = jnp.moveaxis(q.reshape(B, nb, Q_BLOCK, H, d), 1, 0)
    scale = d ** -0.5

    def one_block(qi):
        s = jnp.einsum('bqhd,bkhd->bhqk', qi, k, preferred_element_type=jnp.float32) * scale
        p = jax.nn.softmax(s, axis=-1).astype(v.dtype)
        return jnp.einsum('bhqk,bkhd->bqhd', p, v)

    out = lax.map(one_block, qb)
    return jnp.moveaxis(out, 0, 1).reshape(B, S, H * v.shape[-1])


def window_attend(q, k, v, k_ctx, v_ctx, sink):
    B, S, H, d = q.shape
    kvh = k.shape[2]
    g = H // kvh
    nb = S // BLOCK
    qb = q.reshape(B, nb, BLOCK, kvh, g, d)

    def band(t):
        tp = jnp.pad(t, ((0, 0), (BLOCK, BLOCK), (0, 0), (0, 0)))
        tb = tp.reshape(B, nb + 2, BLOCK, kvh, d)
        return jnp.concatenate([tb[:, :-2], tb[:, 1:-1], tb[:, 2:]], axis=2)

    kb, vb = band(k), band(v)
    scale = d ** -0.5
    s_loc = jnp.einsum('bnqkgd,bnjkd->bnkgqj', qb, kb, preferred_element_type=jnp.float32) * scale
    kpos = jnp.arange(3 * BLOCK) - BLOCK
    rel = kpos[None, :] - jnp.arange(BLOCK)[:, None]
    kabs = jnp.arange(nb)[:, None, None] * BLOCK + kpos[None, None, :]
    valid = (jnp.abs(rel) <= WINDOW)[None] & (kabs >= 0) & (kabs < S)
    s_loc = jnp.where(valid[None, :, None, None], s_loc, -jnp.inf)
    s_ctx = jnp.einsum('bnqkgd,bjkd->bnkgqj', qb, k_ctx, preferred_element_type=jnp.float32) * scale
    s_sink = jnp.broadcast_to(sink.astype(jnp.float32).reshape(1, 1, kvh, g, 1, 1), s_loc.shape[:-1] + (1,))
    p = jax.nn.softmax(jnp.concatenate([s_loc, s_ctx, s_sink], axis=-1), axis=-1).astype(v.dtype)
    n_loc = 3 * BLOCK
    n_ctx = k_ctx.shape[1]
    out = (jnp.einsum('bnkgqj,bnjkd->bnqkgd', p[..., :n_loc], vb)
           + jnp.einsum('bnkgqj,bjkd->bnqkgd', p[..., n_loc:n_loc + n_ctx], v_ctx))
    return out.reshape(B, S, H * d)


def context_gqa_attend(q, k, v, sink):
    B, C, H, d = q.shape
    kvh = k.shape[2]
    g = H // kvh
    qg = q.reshape(B, C, kvh, g, d)
    s = jnp.einsum('bqkgd,bjkd->bkgqj', qg, k, preferred_element_type=jnp.float32) * d ** -0.5
    s_sink = jnp.broadcast_to(sink.astype(jnp.float32).reshape(1, kvh, g, 1, 1), s.shape[:-1] + (1,))
    p = jax.nn.softmax(jnp.concatenate([s, s_sink], axis=-1), axis=-1).astype(v.dtype)
    out = jnp.einsum('bkgqj,bjkd->bqkgd', p[..., :k.shape[1]], v)
    return out.reshape(B, C, H * d)


def moe_ffn(h, router_w, router_b, w_gate, w_up, w_down):
    T = h.shape[0]
    aff = jax.nn.sigmoid(jnp.einsum('td,de->te', h, router_w, preferred_element_type=jnp.float32))
    sel = aff + router_b.astype(jnp.float32)
    gscore = lax.top_k(sel.reshape(T, N_EXPERT_GROUPS, EXPERTS_PER_GROUP), TOP_K)[0].sum(-1)
    gbest = jnp.argmax(gscore, axis=-1)
    in_group = (jnp.arange(N_EXPERTS) // EXPERTS_PER_GROUP)[None, :] == gbest[:, None]
    _, idx = lax.top_k(jnp.where(in_group, sel, -jnp.inf), TOP_K)
    wsel = jnp.take_along_axis(aff, idx, axis=-1)
    wsel = wsel / jnp.sum(wsel, axis=-1, keepdims=True)
    gates = jnp.einsum('tk,tke->te', wsel, jax.nn.one_hot(idx, N_EXPERTS, dtype=jnp.float32)).astype(h.dtype)
    y = jnp.zeros_like(h)
    for e in range(N_EXPERTS):
        a = jax.nn.silu(h @ w_gate[e]) * (h @ w_up[e])
        y = y + gates[:, e:e + 1] * (a @ w_down[e])
    return y


def setup_inputs(seed: int = 0) -> dict:
    key = jax.random.key(seed)
    ks = jax.random.split(key, 26)
    f32 = jnp.float32

    def nrm(k, shape, fan_in, mult=1.0):
        return jax.random.normal(k, shape, f32) * (mult * fan_in ** -0.5)

    def gain(k, shape):
        return 1.0 + 0.1 * jax.random.normal(k, shape, f32)

    return {
        'x': jax.random.normal(ks[0], (BATCH, SEQ, D_MODEL), f32),
        'c': jax.random.normal(ks[1], (BATCH, D_MODEL), f32),
        'ctx': jax.random.normal(ks[2], (BATCH, CTX_LEN, D_MODEL), f32),
        'c_ctx': jax.random.normal(ks[3], (D_MODEL,), f32),
        'ada_w': nrm(ks[4], (DEPTH, D_MODEL, 6 * D_MODEL), D_MODEL, 0.5),
        'ada_b': 0.02 * jax.random.normal(ks[5], (DEPTH, 6 * D_MODEL), f32),
        'norm1_g': gain(ks[6], (DEPTH, D_MODEL)),
        'norm2_g': gain(ks[7], (DEPTH, D_MODEL)),
        'w_in': nrm(ks[8], (DEPTH, D_MODEL, D_IN), D_MODEL),
        'fnet_w': nrm(ks[9], (DEPTH, FNET_GROUPS, FNET_CH, FNET_CH), FNET_CH),
        'mla_cq_g': gain(ks[10], (DEPTH, MLA_Q_RANK)),
        'mla_ckv_g': gain(ks[11], (DEPTH, MLA_KV_RANK)),
        'mla_w_uq': nrm(ks[12], (DEPTH, MLA_Q_RANK, MLA_HEADS * MLA_QK), MLA_Q_RANK),
        'mla_w_uk': nrm(ks[13], (DEPTH, MLA_KV_RANK, MLA_HEADS * MLA_NOPE), MLA_KV_RANK),
        'mla_w_uv': nrm(ks[14], (DEPTH, MLA_KV_RANK, MLA_HEADS * MLA_V), MLA_KV_RANK),
        'mla_q_g': gain(ks[15], (DEPTH, MLA_QK)),
        'mla_k_g': gain(ks[16], (DEPTH, MLA_QK)),
        'swa_q_g': gain(ks[17], (DEPTH, SWA_HEAD_DIM)),
        'swa_k_g': gain(ks[18], (DEPTH, SWA_HEAD_DIM)),
        'swa_sink': 0.5 * jax.random.normal(ks[19], (DEPTH, SWA_HEADS), f32),
        'w_out': nrm(ks[20], (DEPTH, D_MIX, D_MODEL), D_MIX),
        'router_w': nrm(ks[21], (D_MODEL, N_EXPERTS), D_MODEL),
        'router_b': 0.01 * jax.random.normal(ks[22], (N_EXPERTS,), f32),
        'exp_w_gate': nrm(ks[23], (DEPTH, N_EXPERTS, D_MODEL, D_EXPERT), D_MODEL),
        'exp_w_up': nrm(ks[24], (DEPTH, N_EXPERTS, D_MODEL, D_EXPERT), D_MODEL),
        'exp_w_down': nrm(ks[25], (DEPTH, N_EXPERTS, D_EXPERT, D_MODEL), D_EXPERT),
    }


def reference(x, c, ctx, c_ctx, ada_w, ada_b, norm1_g, norm2_g, w_in, fnet_w,
              mla_cq_g, mla_ckv_g, mla_w_uq, mla_w_uk, mla_w_uv, mla_q_g, mla_k_g,
              swa_q_g, swa_k_g, swa_sink, w_out, router_w, router_b,
              exp_w_gate, exp_w_up, exp_w_down):
    B, S, D = x.shape
    C = ctx.shape[1]
    ROWS = S // GRID_W
    rows = jnp.repeat(jnp.arange(ROWS), GRID_W)
    cols = jnp.arange(ROWS * GRID_W) % GRID_W
    rope = axial_rope_table(rows, cols, MLA_ROPE) + axial_rope_table(rows, cols, SWA_HEAD_DIM)
    silu_c = jax.nn.silu(c)
    silu_cc = jax.nn.silu(c_ctx)

    for l in range(DEPTH):
        last = l == DEPTH - 1
        sh1, sc1, g1, sh2, sc2, g2 = jnp.split((silu_c @ ada_w[l] + ada_b[l])[:, None, :], 6, axis=-1)
        csh1, csc1, cg1, csh2, csc2, cg2 = jnp.split(silu_cc @ ada_w[l] + ada_b[l], 6, axis=-1)
        mix_w = (w_in[l], mla_cq_g[l], mla_ckv_g[l], mla_w_uq[l], mla_w_uk[l], mla_w_uv[l],
                 mla_q_g[l], mla_k_g[l], swa_q_g[l], swa_k_g[l])

        hx = rmsnorm(x, norm1_g[l]) * (1.0 + sc1) + sh1
        hc = rmsnorm(ctx, norm1_g[l]) * (1.0 + csc1) + csh1
        fx, qmx, kmx, vmx, qsx, ksx, vsx = mixer_inputs(hx, *mix_w, rope=rope)
        fc, qmc, kmc, vmc, qsc, ksc, vsc = mixer_inputs(hc, *mix_w, rope=None, with_queries=not last)
        out_x = jnp.concatenate([
            fourier_mix(fx, fnet_w[l]),
            dense_attend_blocks(qmx, jnp.concatenate([kmx, kmc], axis=1), jnp.concatenate([vmx, vmc], axis=1)),
            window_attend(qsx, ksx, vsx, ksc, vsc, swa_sink[l]),
        ], axis=-1)
        x = x + g1 * (out_x @ w_out[l])
        if not last:
            out_c = jnp.concatenate([
                fourier_mix(fc, fnet_w[l]),
                dense_attend_blocks(qmc, kmc, vmc),
                context_gqa_attend(qsc, ksc, vsc, swa_sink[l]),
            ], axis=-1)
            ctx = ctx + cg1 * (out_c @ w_out[l])

        moe_w = (router_w, router_b, exp_w_gate[l], exp_w_up[l], exp_w_down[l])
        h2x = (rmsnorm(x, norm2_g[l]) * (1.0 + sc2) + sh2).reshape(B * S, D)
        if last:
            x = x + g2 * moe_ffn(h2x, *moe_w).reshape(B, S, D)
        else:
            h2c = (rmsnorm(ctx, norm2_g[l]) * (1.0 + csc2) + csh2).reshape(B * C, D)
            y = moe_ffn(jnp.concatenate([h2x, h2c], axis=0), *moe_w)
            x = x + g2 * y[:B * S].reshape(B, S, D)
            ctx = ctx + cg2 * y[B * S:].reshape(B, C, D)
    return x
```

```python
import functools

import numpy as np
import jax
import jax.numpy as jnp
from jax import lax
from jax.experimental import pallas as pl
from jax.experimental.pallas import tpu as pltpu

F32 = jnp.float32
BF16 = jnp.bfloat16

D_MODEL = 1024
GRID_W = 64
FNET_GROUPS = 4
FNET_CH = 64
D_FNET = FNET_GROUPS * FNET_CH
MLA_HEADS = 6
MLA_Q_RANK = 256
MLA_KV_RANK = 128
MLA_NOPE = 64
MLA_ROPE = 32
MLA_QK = MLA_NOPE + MLA_ROPE
MLA_V = 64
SWA_HEADS = 6
SWA_KV_HEADS = 2
SWA_HEAD_DIM = 64
WINDOW = 128
BLOCK = 128
N_EXPERTS = 16
N_EXPERT_GROUPS = 4
EXPERTS_PER_GROUP = 4
D_EXPERT = 512
ROPE_THETA = 10000.0
EPS = 1e-6

LANES = 128
TM = 256
HEAD_SLOT = LANES
VMEM_LIMIT = 48 * 1024 * 1024

P_UF = 0
P_CQ = 256
P_CKV = 512
P_QS = 640
P_KS = 1024
P_VS = 1152
P_KR = 1280
P_TOT = 1408
SWA_HEAD_ORDER = (0, 3, 1, 4, 2, 5)


def _dot(a, b):
    return jnp.dot(a, b, preferred_element_type=F32)


def _dot_nt(a, b):
    return lax.dot_general(a, b, (((1,), (1,)), ((), ())), preferred_element_type=F32)


def _rms(x, n):
    return x * lax.rsqrt(jnp.sum(x * x, axis=-1, keepdims=True) * (1.0 / n) + EPS)


def _rope(x, c, sa, sb, half):
    n = x.shape[-1]
    return x * c + pltpu.roll(x, n - half, 1) * sa + pltpu.roll(x, half, 1) * sb


def _cparams(sem):
    return pltpu.CompilerParams(dimension_semantics=sem, vmem_limit_bytes=VMEM_LIMIT)


def _adaln_kernel(c_ref, w_ref, b_ref, o_ref):
    c = c_ref[...]
    sc = c * jax.nn.sigmoid(c)
    o_ref[...] = jnp.dot(sc, w_ref[...], preferred_element_type=F32,
                         precision=lax.Precision.HIGHEST) + b_ref[...]


def _adaln(cvec, ada_w, ada_b):
    L, D, N6 = ada_w.shape
    R = cvec.shape[0]
    bn = 512
    return pl.pallas_call(
        _adaln_kernel,
        grid=(L, N6 // bn),
        in_specs=[pl.BlockSpec((R, D), lambda l, j: (0, 0)),
                  pl.BlockSpec((None, D, bn), lambda l, j: (l, 0, j)),
                  pl.BlockSpec((None, 1, bn), lambda l, j: (l, 0, j))],
        out_specs=pl.BlockSpec((None, R, bn), lambda l, j: (l, 0, j)),
        out_shape=jax.ShapeDtypeStruct((L, R, N6), F32),
        compiler_params=_cparams(("arbitrary", "arbitrary")),
        name="adaln",
    )(cvec, ada_w, ada_b.reshape(L, 1, N6))


def _prep_kernel(x_ref, mod_ref, n1g_ref, win_ref, cqg_ref, ckvg_ref, wuq_ref, wuk_ref, wuv_ref,
                 mqg_ref, mkg_ref, sqg_ref, skg_ref, cm_ref, sam_ref, sbm_ref, cs_ref, sas_ref, sbs_ref,
                 dft_ref, pp_ref, qm_ref, km_ref, vm_ref, qs_ref, ks_ref, vs_ref):
    x = x_ref[...]
    m = mod_ref[...]
    h = _rms(x, D_MODEL) * n1g_ref[...] * (1.0 + m[1:2]) + m[0:1]
    p = _dot(h.astype(BF16), win_ref[...])

    u = p[:, P_UF:P_UF + D_FNET].astype(BF16)
    pp_ref[...] = _dot(u, dft_ref[...]).astype(BF16)

    cm, sam, sbm = cm_ref[...], sam_ref[...], sbm_ref[...]
    cs, sas, sbs = cs_ref[...], sas_ref[...], sbs_ref[...]

    cq = _rms(p[:, P_CQ:P_CQ + MLA_Q_RANK], MLA_Q_RANK) * cqg_ref[...]
    qraw = _dot(cq.astype(BF16), wuq_ref[...])
    mqg = mqg_ref[...]
    for hh in range(MLA_HEADS):
        sl = slice(hh * HEAD_SLOT, (hh + 1) * HEAD_SLOT)
        qn = _rms(qraw[:, sl], MLA_QK) * mqg
        qm_ref[:, sl] = (_rope(qn, cm, sam, sbm, MLA_ROPE // 4) * (MLA_QK ** -0.5)).astype(BF16)

    ckv = (_rms(p[:, P_CKV:P_CKV + MLA_KV_RANK], MLA_KV_RANK) * ckvg_ref[...]).astype(BF16)
    knope = _dot(ckv, wuk_ref[...])
    vm_ref[...] = _dot(ckv, wuv_ref[...]).astype(BF16)
    kr = p[:, P_KR:P_KR + LANES]
    mkg = mkg_ref[...]
    for hh in range(MLA_HEADS):
        sl = slice(hh * HEAD_SLOT, (hh + 1) * HEAD_SLOT)
        kn = _rms(knope[:, sl] + kr, MLA_QK) * mkg
        km_ref[:, sl] = _rope(kn, cm, sam, sbm, MLA_ROPE // 4).astype(BF16)

    lo = lax.broadcasted_iota(jnp.int32, (1, LANES), 1) < SWA_HEAD_DIM

    def head_norm(slab, g):
        sq = slab * slab
        s_lo = jnp.sum(jnp.where(lo, sq, 0.0), axis=-1, keepdims=True)
        s_hi = jnp.sum(jnp.where(lo, 0.0, sq), axis=-1, keepdims=True)
        r = jnp.where(lo, lax.rsqrt(s_lo * (1.0 / SWA_HEAD_DIM) + EPS),
                      lax.rsqrt(s_hi * (1.0 / SWA_HEAD_DIM) + EPS))
        return slab * r * g

    sqg = sqg_ref[...]
    for s in range(SWA_HEADS // 2):
        sl = slice(P_QS + s * LANES, P_QS + (s + 1) * LANES)
        qn = head_norm(p[:, sl], sqg)
        qs_ref[:, s * LANES:(s + 1) * LANES] = (
            _rope(qn, cs, sas, sbs, SWA_HEAD_DIM // 4) * (SWA_HEAD_DIM ** -0.5)).astype(BF16)
    kn = head_norm(p[:, P_KS:P_KS + LANES], skg_ref[...])
    ks_ref[...] = _rope(kn, cs, sas, sbs, SWA_HEAD_DIM // 4).astype(BF16)
    vs_ref[...] = p[:, P_VS:P_VS + LANES].astype(BF16)


def _prep(xc, mod, n1g, wl, tabs, nx_tiles, tiles_per_batch):
    T = xc.shape[0]
    nt = T // TM

    def bidx(i):
        return jnp.where(i < nx_tiles, i // tiles_per_batch, mod.shape[0] - 1)

    def ridx(i):
        return jnp.where(i < nx_tiles, i % tiles_per_batch, tiles_per_batch)

    def full(a):
        return pl.BlockSpec(a.shape, lambda i: (0,) * a.ndim)

    tab_spec = pl.BlockSpec((TM, LANES), lambda i: (ridx(i), 0))
    row = lambda w: pl.BlockSpec((TM, w), lambda i: (i, 0))
    consts = [n1g, wl["w_in"], wl["cq_g"], wl["ckv_g"], wl["w_uq"], wl["w_uk"], wl["w_uv"],
              wl["mq_g"], wl["mk_g"], wl["sq_g"], wl["sk_g"]]
    outs = [("pp", 2 * D_FNET), ("qm", MLA_HEADS * HEAD_SLOT), ("km", MLA_HEADS * HEAD_SLOT),
            ("vm", MLA_HEADS * MLA_V), ("qs", SWA_HEADS * SWA_HEAD_DIM), ("ks", LANES), ("vs", LANES)]
    res = pl.pallas_call(
        _prep_kernel,
        grid=(nt,),
        in_specs=[row(D_MODEL), pl.BlockSpec((None, 8, D_MODEL), lambda i: (bidx(i), 0, 0))]
                 + [full(a) for a in consts] + [tab_spec] * 6 + [full(tabs["dft64"])],
        out_specs=[row(w) for _, w in outs],
        out_shape=[jax.ShapeDtypeStruct((T, w), BF16) for _, w in outs],
        compiler_params=_cparams(("arbitrary",)),
        name="prep",
    )(xc, mod, *consts, tabs["cm"], tabs["sam"], tabs["sbm"], tabs["cs"], tabs["sas"], tabs["sbs"],
      tabs["dft64"])
    return dict(zip([n for n, _ in outs], res))


def _mla_kernel(*refs, with_x):
    if with_x:
        q_ref, kx_ref, kc_ref, vx_ref, vc_ref, o_ref = refs
    else:
        q_ref, kc_ref, vc_ref, o_ref = refs
    outs = []
    for hh in range(2):
        sl = slice(hh * HEAD_SLOT, (hh + 1) * HEAD_SLOT)
        q = q_ref[:, sl]
        sc = _dot_nt(q, kc_ref[:, sl])
        m = jnp.max(sc, axis=-1, keepdims=True)
        if with_x:
            sx = _dot_nt(q, kx_ref[:, sl])
            m = jnp.maximum(m, jnp.max(sx, axis=-1, keepdims=True))
            px = jnp.exp(sx - m)
        pc = jnp.exp(sc - m)
        l = jnp.sum(pc, axis=-1, keepdims=True)
        o = _dot(pc.astype(BF16), vc_ref[...])
        if with_x:
            l = l + jnp.sum(px, axis=-1, keepdims=True)
            o = o + _dot(px.astype(BF16), vx_ref[...])
        outs.append(o / l)
    lane = lax.broadcasted_iota(jnp.int32, (1, LANES), 1)
    o_ref[...] = jnp.where(lane < MLA_V, outs[0], outs[1]).astype(BF16)


def _ctx_rows_kernel(kernel_fn, *refs, **kw):
    kernel_fn(*refs[:-2], refs[-1], **kw)


def _mla_attend(qm, km, vm, B, S, C, prev=None):
    T = qm.shape[0]
    npair = MLA_HEADS // 2
    with_x = prev is None
    if with_x:
        tq = 512
        nq = S // tq
        qmap = lambda b, p, i: (b * nq + i, p)
        in_specs = [pl.BlockSpec((tq, 2 * HEAD_SLOT), qmap),
                    pl.BlockSpec((S, 2 * HEAD_SLOT), lambda b, p, i: (b, p)),
                    pl.BlockSpec((C, 2 * HEAD_SLOT), lambda b, p, i: (B * S // C + b, p)),
                    pl.BlockSpec((S, LANES), lambda b, p, i: (b, p)),
                    pl.BlockSpec((C, LANES), lambda b, p, i: (B * S // C + b, p))]
        args = (qm, km, km, vm, vm)
        body = functools.partial(_mla_kernel, with_x=True)
        aliases = {}
    else:
        tq = C
        nq = 1
        qmap = lambda b, p, i: (B * S // C + b, p)
        in_specs = [pl.BlockSpec((tq, 2 * HEAD_SLOT), qmap),
                    pl.BlockSpec((C, 2 * HEAD_SLOT), qmap),
                    pl.BlockSpec((C, LANES), qmap),
                    pl.BlockSpec(memory_space=pl.ANY)]
        args = (qm, km, vm, prev)
        body = functools.partial(_ctx_rows_kernel, _mla_kernel, with_x=False)
        aliases = {3: 0}
    return pl.pallas_call(
        body,
        grid=(B, npair, nq),
        in_specs=in_specs,
        out_specs=pl.BlockSpec((tq, LANES), qmap),
        out_shape=jax.ShapeDtypeStruct((T, MLA_HEADS * MLA_V), BF16),
        input_output_aliases=aliases,
        compiler_params=_cparams(("arbitrary",) * 3),
        name="mla_x" if with_x else "mla_c",
    )(*args)


def _swa_kernel(*refs, with_x, nblk):
    if with_x:
        sink_ref, q_ref, kp_ref, ko_ref, kn_ref, kc_ref, vp_ref, vo_ref, vn_ref, vc_ref, o_ref = refs
    else:
        sink_ref, q_ref, kc_ref, vc_ref, o_ref = refs
    n = pl.program_id(1)
    lane = lax.broadcasted_iota(jnp.int32, (1, LANES), 1)
    lo = lane < SWA_HEAD_DIM
    row2 = lax.broadcasted_iota(jnp.int32, (2 * BLOCK, 1), 0)
    if with_x:
        qi = lax.broadcasted_iota(jnp.int32, (2 * BLOCK, BLOCK), 0) % BLOCK
        kj = lax.broadcasted_iota(jnp.int32, (2 * BLOCK, BLOCK), 1)
        ok_prev = jnp.logical_and(kj - BLOCK - qi >= -WINDOW, n > 0)
        ok_next = jnp.logical_and(kj + BLOCK - qi <= WINDOW, n < nblk - 1)
    for s in range(SWA_HEADS // 2):
        q = q_ref[:, s * LANES:(s + 1) * LANES]
        zero = jnp.zeros_like(q)
        q2 = jnp.concatenate([jnp.where(lo, q, zero), jnp.where(lo, zero, q)], axis=0)
        sink = jnp.where(row2 < BLOCK, sink_ref[0, s], sink_ref[0, SWA_HEADS // 2 + s])
        sc = _dot_nt(q2, kc_ref[...])
        m = jnp.maximum(jnp.max(sc, axis=-1, keepdims=True), sink)
        if with_x:
            sp = jnp.where(ok_prev, _dot_nt(q2, kp_ref[...]), -jnp.inf)
            so = _dot_nt(q2, ko_ref[...])
            sn = jnp.where(ok_next, _dot_nt(q2, kn_ref[...]), -jnp.inf)
            m = jnp.maximum(m, jnp.max(sp, axis=-1, keepdims=True))
            m = jnp.maximum(m, jnp.max(so, axis=-1, keepdims=True))
            m = jnp.maximum(m, jnp.max(sn, axis=-1, keepdims=True))
        pc = jnp.exp(sc - m)
        l = jnp.sum(pc, axis=-1, keepdims=True) + jnp.exp(sink - m)
        o = _dot(pc.astype(BF16), vc_ref[...])
        if with_x:
            for sx, v_ref in ((sp, vp_ref), (so, vo_ref), (sn, vn_ref)):
                px = jnp.exp(sx - m)
                l = l + jnp.sum(px, axis=-1, keepdims=True)
                o = o + _dot(px.astype(BF16), v_ref[...])
        o = o / l
        o_ref[:, s * LANES:(s + 1) * LANES] = jnp.where(lo, o[:BLOCK], o[BLOCK:]).astype(BF16)


def _swa_attend(sink, qs, ks, vs, B, S, C, prev=None):
    T = qs.shape[0]
    cb = B * S // C
    with_x = prev is None
    if with_x:
        nblk = S // BLOCK
        qmap = lambda b, n: (b * nblk + n, 0)
        pmap = lambda b, n: (b * nblk + jnp.maximum(n - 1, 0), 0)
        nmap = lambda b, n: (b * nblk + jnp.minimum(n + 1, nblk - 1), 0)
        cmap = lambda b, n: (cb + b, 0)
        kv = lambda mp: pl.BlockSpec((BLOCK, LANES), mp)
        cspec = pl.BlockSpec((C, LANES), cmap)
        in_specs = [pl.BlockSpec(memory_space=pltpu.SMEM),
                    pl.BlockSpec((BLOCK, SWA_HEADS * SWA_HEAD_DIM), qmap),
                    kv(pmap), kv(qmap), kv(nmap), cspec, kv(pmap), kv(qmap), kv(nmap), cspec]
        args = (sink, qs, ks, ks, ks, ks, vs, vs, vs, vs)
        body = functools.partial(_swa_kernel, with_x=True, nblk=nblk)
        aliases = {}
    else:
        nblk = C // BLOCK
        qmap = lambda b, n: (cb * (C // BLOCK) + b * nblk + n, 0)
        cmap = lambda b, n: (cb + b, 0)
        cspec = pl.BlockSpec((C, LANES), cmap)
        in_specs = [pl.BlockSpec(memory_space=pltpu.SMEM),
                    pl.BlockSpec((BLOCK, SWA_HEADS * SWA_HEAD_DIM), qmap), cspec, cspec,
                    pl.BlockSpec(memory_space=pl.ANY)]
        args = (sink, qs, ks, vs, prev)
        body = functools.partial(_ctx_rows_kernel, _swa_kernel, with_x=False, nblk=nblk)
        aliases = {4: 0}
    return pl.pallas_call(
        body,
        grid=(B, nblk),
        in_specs=in_specs,
        out_specs=pl.BlockSpec((BLOCK, SWA_HEADS * SWA_HEAD_DIM), qmap),
        out_shape=jax.ShapeDtypeStruct((T, SWA_HEADS * SWA_HEAD_DIM), BF16),
        input_output_aliases=aliases,
        compiler_params=_cparams(("arbitrary",) * 2),
        name="swa_x" if with_x else "swa_c",
    )(*args)


def _fourier_kernel(c_ref, s_ref, pp_ref, w_ref, o_ref, *, scale):
    f = _dot(c_ref[...], pp_ref[:, 0:D_FNET]) - _dot(s_ref[...], pp_ref[:, D_FNET:2 * D_FNET])
    o_ref[...] = _dot((f * scale).astype(BF16), w_ref[...]).astype(BF16)


def _fourier(cmat, smat, pp, wblk, B, N, row0, prev=None):
    T = pp.shape[0]
    tq = min(512, N)
    nr = N // tq
    b0 = row0 // N
    o0 = row0 // tq
    scale = float(1.0 / np.sqrt(N * FNET_CH))
    in_specs = [pl.BlockSpec((tq, N), lambda r, b: (r, 0)),
                pl.BlockSpec((tq, N), lambda r, b: (r, 0)),
                pl.BlockSpec((N, 2 * D_FNET), lambda r, b: (b0 + b, 0)),
                pl.BlockSpec((D_FNET, D_FNET), lambda r, b: (0, 0))]
    args = (cmat, smat, pp, wblk)
    if prev is None:
        body = functools.partial(_fourier_kernel, scale=scale)
        aliases = {}
    else:
        in_specs.append(pl.BlockSpec(memory_space=pl.ANY))
        args = args + (prev,)
        body = functools.partial(_ctx_rows_kernel, _fourier_kernel, scale=scale)
        aliases = {4: 0}
    return pl.pallas_call(
        body,
        grid=(nr, B),
        in_specs=in_specs,
        out_specs=pl.BlockSpec((tq, D_FNET), lambda r, b: (o0 + b * nr + r, 0)),
        out_shape=jax.ShapeDtypeStruct((T, D_FNET), BF16),
        input_output_aliases=aliases,
        compiler_params=_cparams(("arbitrary",) * 2),
        name="fourier_%d" % N,
    )(*args)


def _route_rows(sel, aff):
    G, K = N_EXPERT_GROUPS, EXPERTS_PER_GROUP
    gscore = []
    for g in range(G):
        a = sel[g * K:(g + 1) * K]
        best = None
        for i in range(K):
            for j in range(i + 1, K):
                v = a[i] + a[j]
                best = v if best is None else jnp.maximum(best, v)
        gscore.append(best)
    gb = jnp.zeros_like(gscore[0])
    gbest = gscore[0]
    for g in range(1, G):
        upd = gscore[g] > gbest
        gb = jnp.where(upd, float(g), gb)
        gbest = jnp.where(upd, gscore[g], gbest)
    cs, ca = [], []
    for i in range(K):
        c, a = sel[i], aff[i]
        for g in range(1, G):
            pick = gb == float(g)
            c = jnp.where(pick, sel[g * K + i], c)
            a = jnp.where(pick, aff[g * K + i], a)
        cs.append(c)
        ca.append(a)

    def first_max(vals):
        bi = jnp.zeros_like(vals[0])
        bv = vals[0]
        for i in range(1, K):
            upd = vals[i] > bv
            bi = jnp.where(upd, float(i), bi)
            bv = jnp.where(upd, vals[i], bv)
        return bi

    i1 = first_max(cs)
    cs2 = [jnp.where(i1 == float(i), -jnp.inf, cs[i]) for i in range(K)]
    i2 = first_max(cs2)
    a1 = sum(jnp.where(i1 == float(i), ca[i], 0.0) for i in range(K))
    a2 = sum(jnp.where(i2 == float(i), ca[i], 0.0) for i in range(K))
    den = a1 + a2
    return gb * K + i1, gb * K + i2, a1 / den, a2 / den


def _post_kernel(x_ref, mod_ref, fo_ref, ml_ref, sw_ref, wof_ref, wom_ref, wos_ref, n2g_ref,
                 rwh_ref, rwl_ref, rb_ref, xn_ref, h2_ref, rc_ref):
    m = mod_ref[...]
    mix = (_dot(fo_ref[...], wof_ref[...]) + _dot(ml_ref[...], wom_ref[...])
           + _dot(sw_ref[...], wos_ref[...]))
    xn = x_ref[...] + m[2:3] * mix
    xn_ref[...] = xn
    h2 = _rms(xn, D_MODEL) * n2g_ref[...] * (1.0 + m[4:5]) + m[3:4]
    h2_ref[...] = h2.astype(BF16)
    hh = h2.astype(BF16)
    hl = (h2 - hh.astype(F32)).astype(BF16)
    logits = _dot(hh, rwh_ref[...]) + (_dot(hl, rwh_ref[...]) + _dot(hh, rwl_ref[...]))
    lt = logits.T[0:N_EXPERTS, :]
    aff_t = jax.nn.sigmoid(lt)
    sel_t = aff_t + rb_ref[...]
    sel = [sel_t[e:e + 1, :] for e in range(N_EXPERTS)]
    aff = [aff_t[e:e + 1, :] for e in range(N_EXPERTS)]
    e1, e2, w1, w2 = _route_rows(sel, aff)
    sub = lax.broadcasted_iota(jnp.int32, (8, TM), 0)
    blk = jnp.where(sub == 0, e1, jnp.where(sub == 1, e2, jnp.where(sub == 2, w1, jnp.where(sub == 3, w2, 0.0))))
    rows = jnp.concatenate([blk, jnp.zeros((LANES - 8, TM), F32)], axis=0)
    rc_ref[...] = rows.T


def _post(xc, mod, fo, ml, sw, wl, n2g, rw_hi, rw_lo, rb_col, nt, nx_tiles, tiles_per_batch):
    rows = nt * TM

    def bidx(i):
        return jnp.where(i < nx_tiles, i // tiles_per_batch, mod.shape[0] - 1)

    def full(a):
        return pl.BlockSpec(a.shape, lambda i: (0,) * a.ndim)

    row = lambda w: pl.BlockSpec((TM, w), lambda i: (i, 0))
    consts = [wl["wo_f"], wl["wo_m"], wl["wo_s"], n2g, rw_hi, rw_lo, rb_col]
    return pl.pallas_call(
        _post_kernel,
        grid=(nt,),
        in_specs=[row(D_MODEL), pl.BlockSpec((None, 8, D_MODEL), lambda i: (bidx(i), 0, 0)),
                  row(D_FNET), row(MLA_HEADS * MLA_V), row(SWA_HEADS * SWA_HEAD_DIM)]
                 + [full(a) for a in consts],
        out_specs=[row(D_MODEL), row(D_MODEL), row(LANES)],
        out_shape=[jax.ShapeDtypeStruct((rows, D_MODEL), F32),
                   jax.ShapeDtypeStruct((rows, D_MODEL), BF16),
                   jax.ShapeDtypeStruct((rows, LANES), F32)],
        compiler_params=_cparams(("arbitrary",)),
        name="post",
    )(xc, mod, fo, ml, sw, *consts)


def _moe_dense_kernel(h_ref, rc_ref, wg_ref, wu_ref, wd_ref, acc_ref, xn_ref, mod_ref, o_ref,
                      wgb, wub, wdb):
    e = pl.program_id(0)
    i = pl.program_id(1)

    @pl.when(i == 0)
    def _():
        wgb[...] = wg_ref[...].astype(BF16)
        wub[...] = wu_ref[...].astype(BF16)
        wdb[...] = wd_ref[...].astype(BF16)

    h = h_ref[...]
    a = _dot(h, wgb[...])
    a = a * jax.nn.sigmoid(a) * _dot(h, wub[...])
    y = _dot(a.astype(BF16), wdb[...])
    rc = rc_ref[...]
    ef = e.astype(F32)
    gate = (jnp.where(rc[:, 0:1] == ef, rc[:, 2:3], 0.0) + jnp.where(rc[:, 1:2] == ef, rc[:, 3:4], 0.0))
    y = gate * y

    @pl.when(e == 0)
    def _():
        o_ref[...] = y

    @pl.when(jnp.logical_and(e > 0, e < N_EXPERTS - 1))
    def _():
        o_ref[...] = acc_ref[...] + y

    @pl.when(e == N_EXPERTS - 1)
    def _():
        o_ref[...] = xn_ref[...] + mod_ref[5:6, :] * (acc_ref[...] + y)


def _moe_dense(h2, rc, wg, wu, wd, xn, mod, nx_tiles, tiles_per_batch):
    rows = h2.shape[0]
    nt = rows // TM

    def bidx(e, i):
        return jnp.where(i < nx_tiles, i // tiles_per_batch, mod.shape[0] - 1)

    row = lambda w: pl.BlockSpec((TM, w), lambda e, i: (i, 0))
    acc0 = jnp.zeros((rows, D_MODEL), F32)
    return pl.pallas_call(
        _moe_dense_kernel,
        grid=(N_EXPERTS, nt),
        in_specs=[row(D_MODEL), row(LANES),
                  pl.BlockSpec((None, D_MODEL, D_EXPERT), lambda e, i: (e, 0, 0)),
                  pl.BlockSpec((None, D_MODEL, D_EXPERT), lambda e, i: (e, 0, 0)),
                  pl.BlockSpec((None, D_EXPERT, D_MODEL), lambda e, i: (e, 0, 0)),
                  row(D_MODEL), row(D_MODEL),
                  pl.BlockSpec((None, 8, D_MODEL), lambda e, i: (bidx(e, i), 0, 0))],
        out_specs=row(D_MODEL),
        out_shape=jax.ShapeDtypeStruct((rows, D_MODEL), F32),
        scratch_shapes=[pltpu.VMEM((D_MODEL, D_EXPERT), BF16), pltpu.VMEM((D_MODEL, D_EXPERT), BF16),
                        pltpu.VMEM((D_EXPERT, D_MODEL), BF16)],
        input_output_aliases={5: 0},
        compiler_params=_cparams(("arbitrary",) * 2),
        name="moe_dense",
    )(h2, rc, wg, wu, wd, acc0, xn, mod)


def _rope_tables(S, C):
    t = np.arange(S)
    rows, cols = t // GRID_W, t % GRID_W

    def axis_tabs(d_rot, lane0, width):
        d_axis = d_rot // 2
        inv = ROPE_THETA ** (-np.arange(0, d_axis, 2, dtype=np.float64) / d_axis)
        ar, ac = rows[:, None] * inv, cols[:, None] * inv
        ang = np.concatenate([ar, ar, ac, ac], axis=-1)
        q = d_rot // 4
        first = np.concatenate([np.ones(q), np.zeros(q), np.ones(q), np.zeros(q)])
        cos = np.ones((S + C, width))
        sa = np.zeros((S + C, width))
        sb = np.zeros((S + C, width))
        cos[:S, lane0:lane0 + d_rot] = np.cos(ang)
        sa[:S, lane0:lane0 + d_rot] = -np.sin(ang) * first
        sb[:S, lane0:lane0 + d_rot] = np.sin(ang) * (1 - first)
        return cos, sa, sb

    cm, sam, sbm = axis_tabs(MLA_ROPE, MLA_NOPE, LANES)
    cs, sas, sbs = axis_tabs(SWA_HEAD_DIM, 0, SWA_HEAD_DIM)
    tile2 = lambda a: np.concatenate([a, a], axis=1)
    out = dict(cm=cm, sam=sam, sbm=sbm, cs=tile2(cs), sas=tile2(sas), sbs=tile2(sbs))
    return {k: jnp.asarray(v, F32) for k, v in out.items()}


def _dft_mats(N):
    k = np.arange(N)
    ang = 2.0 * np.pi * ((k[:, None] * k[None, :]) % N) / N
    return np.cos(ang), np.sin(ang)


def _dft64_blocks():
    c, s = _dft_mats(FNET_CH)
    eye = np.eye(FNET_GROUPS)
    return np.concatenate([np.kron(eye, c), np.kron(eye, s)], axis=1)


def _layer_weights(l, w_in, fnet_w, mla_cq_g, mla_ckv_g, mla_w_uq, mla_w_uk, mla_w_uv, mla_q_g, mla_k_g,
                   swa_q_g, swa_k_g, swa_sink, w_out):
    D = D_MODEL
    wi = w_in[l]
    o_kr = D_FNET + MLA_Q_RANK + MLA_KV_RANK
    o_qs = o_kr + MLA_ROPE
    o_ks = o_qs + SWA_HEADS * SWA_HEAD_DIM
    o_vs = o_ks + SWA_KV_HEADS * SWA_HEAD_DIM
    order = np.array(SWA_HEAD_ORDER)
    w_qs = wi[:, o_qs:o_ks].reshape(D, SWA_HEADS, SWA_HEAD_DIM)[:, order].reshape(D, -1)
    z = lambda n: jnp.zeros((D, n), F32)
    win = jnp.concatenate([wi[:, :o_kr], w_qs, wi[:, o_ks:o_vs], wi[:, o_vs:],
                           z(MLA_NOPE), wi[:, o_kr:o_qs], z(LANES - MLA_QK)], axis=1)
    pad_slot = lambda w, d: jnp.pad(w.reshape(w.shape[0], MLA_HEADS, d),
                                    ((0, 0), (0, 0), (0, HEAD_SLOT - d))).reshape(w.shape[0], -1)
    wo = w_out[l]
    o_m = D_FNET
    o_s = D_FNET + MLA_HEADS * MLA_V
    wo_s = wo[o_s:].reshape(SWA_HEADS, SWA_HEAD_DIM, D)[order].reshape(-1, D)
    fw = fnet_w[l]
    wblk = jnp.zeros((D_FNET, D_FNET), F32)
    for g in range(FNET_GROUPS):
        wblk = wblk.at[g * FNET_CH:(g + 1) * FNET_CH, g * FNET_CH:(g + 1) * FNET_CH].set(fw[g])
    pad_g = lambda g: jnp.pad(g, (0, HEAD_SLOT - MLA_QK)).reshape(1, HEAD_SLOT)
    return dict(
        w_in=win.astype(BF16),
        cq_g=mla_cq_g[l].reshape(1, -1), ckv_g=mla_ckv_g[l].reshape(1, -1),
        w_uq=pad_slot(mla_w_uq[l], MLA_QK).astype(BF16),
        w_uk=pad_slot(mla_w_uk[l], MLA_NOPE).astype(BF16),
        w_uv=mla_w_uv[l].astype(BF16),
        mq_g=pad_g(mla_q_g[l]), mk_g=pad_g(mla_k_g[l]),
        sq_g=jnp.tile(swa_q_g[l], 2).reshape(1, LANES), sk_g=jnp.tile(swa_k_g[l], 2).reshape(1, LANES),
        sink=swa_sink[l][order].reshape(1, SWA_HEADS),
        wo_f=wo[:o_m].astype(BF16), wo_m=wo[o_m:o_s].astype(BF16), wo_s=wo_s.astype(BF16),
        fnet=wblk.astype(BF16),
    )


def kernel(x, c, ctx, c_ctx, ada_w, ada_b, norm1_g, norm2_g, w_in, fnet_w, mla_cq_g, mla_ckv_g, mla_w_uq,
           mla_w_uk, mla_w_uv, mla_q_g, mla_k_g, swa_q_g, swa_k_g, swa_sink, w_out, router_w, router_b,
           exp_w_gate, exp_w_up, exp_w_down):
    B, S, D = x.shape
    C = ctx.shape[1]
    L = ada_w.shape[0]
    assert D == D_MODEL and S % 512 == 0 and C == TM and S % TM == 0
    nx_tiles = B * S // TM
    nt_all = nx_tiles + B * C // TM
    tiles_per_batch = S // TM

    tabs = _rope_tables(S, C)
    tabs["dft64"] = jnp.asarray(_dft64_blocks(), F32).astype(BF16)
    dft = {n: tuple(jnp.asarray(m, F32).astype(BF16) for m in _dft_mats(n)) for n in (S, C)}

    nmod = 16
    cvec = jnp.concatenate([c, c_ctx[None, :], jnp.zeros((nmod - B - 1, D), F32)], axis=0)
    mod_all = _adaln(cvec, ada_w, ada_b)
    mod_all = mod_all[:, :B + 1].reshape(L, B + 1, 6, D)
    mod_all = jnp.pad(mod_all, ((0, 0), (0, 0), (0, 2), (0, 0)))

    rw = jnp.pad(router_w, ((0, 0), (0, LANES - N_EXPERTS)))
    rw_hi = rw.astype(BF16)
    rw_lo = (rw - rw_hi.astype(F32)).astype(BF16)
    rb_col = router_b.reshape(N_EXPERTS, 1)

    xc = jnp.concatenate([x.reshape(B * S, D), ctx.reshape(B * C, D)], axis=0)
    for l in range(L):
        last = l == L - 1
        wl = _layer_weights(l, w_in, fnet_w, mla_cq_g, mla_ckv_g, mla_w_uq, mla_w_uk, mla_w_uv, mla_q_g,
                            mla_k_g, swa_q_g, swa_k_g, swa_sink, w_out)
        mod = mod_all[l]
        pr = _prep(xc, mod, norm1_g[l].reshape(1, D), wl, tabs, nx_tiles, tiles_per_batch)
        fo = _fourier(dft[S][0], dft[S][1], pr["pp"], wl["fnet"], B, S, 0)
        ml = _mla_attend(pr["qm"], pr["km"], pr["vm"], B, S, C)
        sw = _swa_attend(wl["sink"], pr["qs"], pr["ks"], pr["vs"], B, S, C)
        if not last:
            fo = _fourier(dft[C][0], dft[C][1], pr["pp"], wl["fnet"], B, C, B * S, prev=fo)
            ml = _mla_attend(pr["qm"], pr["km"], pr["vm"], B, S, C, prev=ml)
            sw = _swa_attend(wl["sink"], pr["qs"], pr["ks"], pr["vs"], B, S, C, prev=sw)
        nt = nx_tiles if last else nt_all
        xn, h2, rc = _post(xc, mod, fo, ml, sw, wl, norm2_g[l].reshape(1, D), rw_hi, rw_lo, rb_col,
                           nt, nx_tiles, tiles_per_batch)
        xc = _moe_dense(h2, rc, exp_w_gate[l], exp_w_up[l], exp_w_down[l], xn, mod, nx_tiles, tiles_per_batch)
    return xc[:B * S].reshape(B, S, D)
```

```python
import functools

import numpy as np
import jax
import jax.numpy as jnp
from jax import lax
from jax.experimental import pallas as pl
from jax.experimental.pallas import tpu as pltpu

F32 = jnp.float32
BF16 = jnp.bfloat16

D_MODEL = 1024
GRID_W = 64
FNET_GROUPS = 4
FNET_CH = 64
D_FNET = FNET_GROUPS * FNET_CH
MLA_HEADS = 6
MLA_Q_RANK = 256
MLA_KV_RANK = 128
MLA_NOPE = 64
MLA_ROPE = 32
MLA_QK = MLA_NOPE + MLA_ROPE
MLA_V = 64
SWA_HEADS = 6
SWA_KV_HEADS = 2
SWA_HEAD_DIM = 64
WINDOW = 128
BLOCK = 128
N_EXPERTS = 16
N_EXPERT_GROUPS = 4
EXPERTS_PER_GROUP = 4
D_EXPERT = 512
ROPE_THETA = 10000.0
EPS = 1e-6

LANES = 128
TM = 256
HEAD_SLOT = LANES
CHUNK = 8
CHUNK_LOG2 = 3
NSLOT = 640
MAX_CHUNKS = NSLOT // CHUNK
TE = 256
VMEM_LIMIT = 48 * 1024 * 1024

P_UF = 0
P_CQ = 256
P_CKV = 512
P_QS = 640
P_KS = 1024
P_VS = 1152
P_KR = 1280
P_TOT = 1408
SWA_HEAD_ORDER = (0, 3, 1, 4, 2, 5)


def _dot(a, b):
    return jnp.dot(a, b, preferred_element_type=F32)


def _dot_nt(a, b):
    return lax.dot_general(a, b, (((1,), (1,)), ((), ())), preferred_element_type=F32)


def _rms(x, n):
    return x * lax.rsqrt(jnp.sum(x * x, axis=-1, keepdims=True) / n + EPS)


def _rope(x, c, sa, sb, half):
    n = x.shape[-1]
    return x * c + pltpu.roll(x, n - half, 1) * sa + pltpu.roll(x, half, 1) * sb


def _cparams(sem):
    return pltpu.CompilerParams(dimension_semantics=sem, vmem_limit_bytes=VMEM_LIMIT)


def _adaln_kernel(c_ref, w_ref, b_ref, o_ref):
    c = c_ref[...]
    sc = c * jax.nn.sigmoid(c)
    o_ref[...] = jnp.dot(sc, w_ref[...], preferred_element_type=F32,
                         precision=lax.Precision.HIGHEST) + b_ref[...]


def _adaln(cvec, ada_w, ada_b):
    L, D, N6 = ada_w.shape
    R = cvec.shape[0]
    bn = 512
    return pl.pallas_call(
        _adaln_kernel,
        grid=(L, N6 // bn),
        in_specs=[pl.BlockSpec((R, D), lambda l, j: (0, 0)),
                  pl.BlockSpec((None, D, bn), lambda l, j: (l, 0, j)),
                  pl.BlockSpec((None, 1, bn), lambda l, j: (l, 0, j))],
        out_specs=pl.BlockSpec((None, R, bn), lambda l, j: (l, 0, j)),
        out_shape=jax.ShapeDtypeStruct((L, R, N6), F32),
        compiler_params=_cparams(("arbitrary", "arbitrary")),
        name="adaln",
    )(cvec, ada_w, ada_b.reshape(L, 1, N6))


def _prep_kernel(x_ref, mod_ref, n1g_ref, win_ref, cqg_ref, ckvg_ref, wuq_ref, wuk_ref, wuv_ref,
                 mqg_ref, mkg_ref, sqg_ref, skg_ref, cm_ref, sam_ref, sbm_ref, cs_ref, sas_ref, sbs_ref,
                 dft_ref, pp_ref, qm_ref, km_ref, vm_ref, qs_ref, ks_ref, vs_ref):
    x = x_ref[...]
    m = mod_ref[...]
    h = _rms(x, D_MODEL) * n1g_ref[...] * (1.0 + m[1:2]) + m[0:1]
    p = _dot(h.astype(BF16), win_ref[...])

    u = p[:, P_UF:P_UF + D_FNET].astype(BF16)
    pp_ref[...] = _dot(u, dft_ref[...]).astype(BF16)

    cm, sam, sbm = cm_ref[...], sam_ref[...], sbm_ref[...]
    cs, sas, sbs = cs_ref[...], sas_ref[...], sbs_ref[...]

    cq = _rms(p[:, P_CQ:P_CQ + MLA_Q_RANK], MLA_Q_RANK) * cqg_ref[...]
    qraw = _dot(cq.astype(BF16), wuq_ref[...])
    mqg = mqg_ref[...]
    for hh in range(MLA_HEADS):
        sl = slice(hh * HEAD_SLOT, (hh + 1) * HEAD_SLOT)
        qn = _rms(qraw[:, sl], MLA_QK) * mqg
        qm_ref[:, sl] = (_rope(qn, cm, sam, sbm, MLA_ROPE // 4) * (MLA_QK ** -0.5)).astype(BF16)

    ckv = (_rms(p[:, P_CKV:P_CKV + MLA_KV_RANK], MLA_KV_RANK) * ckvg_ref[...]).astype(BF16)
    knope = _dot(ckv, wuk_ref[...])
    vm_ref[...] = _dot(ckv, wuv_ref[...]).astype(BF16)
    kr = p[:, P_KR:P_KR + LANES]
    mkg = mkg_ref[...]
    for hh in range(MLA_HEADS):
        sl = slice(hh * HEAD_SLOT, (hh + 1) * HEAD_SLOT)
        kn = _rms(knope[:, sl] + kr, MLA_QK) * mkg
        km_ref[:, sl] = _rope(kn, cm, sam, sbm, MLA_ROPE // 4).astype(BF16)

    lo = lax.broadcasted_iota(jnp.int32, (1, LANES), 1) < SWA_HEAD_DIM

    def head_norm(slab, g):
        sq = slab * slab
        s_lo = jnp.sum(jnp.where(lo, sq, 0.0), axis=-1, keepdims=True)
        s_hi = jnp.sum(jnp.where(lo, 0.0, sq), axis=-1, keepdims=True)
        r = jnp.where(lo, lax.rsqrt(s_lo * (1.0 / SWA_HEAD_DIM) + EPS),
                      lax.rsqrt(s_hi * (1.0 / SWA_HEAD_DIM) + EPS))
        return slab * r * g

    sqg = sqg_ref[...]
    for s in range(SWA_HEADS // 2):
        sl = slice(P_QS + s * LANES, P_QS + (s + 1) * LANES)
        qn = head_norm(p[:, sl], sqg)
        qs_ref[:, s * LANES:(s + 1) * LANES] = (
            _rope(qn, cs, sas, sbs, SWA_HEAD_DIM // 4) * (SWA_HEAD_DIM ** -0.5)).astype(BF16)
    kn = head_norm(p[:, P_KS:P_KS + LANES], skg_ref[...])
    ks_ref[...] = _rope(kn, cs, sas, sbs, SWA_HEAD_DIM // 4).astype(BF16)
    vs_ref[...] = p[:, P_VS:P_VS + LANES].astype(BF16)


def _prep(xc, mod, n1g, wl, tabs, nx_tiles, tiles_per_batch):
    T = xc.shape[0]
    nt = T // TM

    def bidx(i):
        return jnp.where(i < nx_tiles, i // tiles_per_batch, mod.shape[0] - 1)

    def ridx(i):
        return jnp.where(i < nx_tiles, i % tiles_per_batch, tiles_per_batch)

    def full(a):
        return pl.BlockSpec(a.shape, lambda i: (0,) * a.ndim)

    tab_spec = pl.BlockSpec((TM, LANES), lambda i: (ridx(i), 0))
    row = lambda w: pl.BlockSpec((TM, w), lambda i: (i, 0))
    consts = [n1g, wl["w_in"], wl["cq_g"], wl["ckv_g"], wl["w_uq"], wl["w_uk"], wl["w_uv"],
              wl["mq_g"], wl["mk_g"], wl["sq_g"], wl["sk_g"]]
    outs = [("pp", 2 * D_FNET), ("qm", MLA_HEADS * HEAD_SLOT), ("km", MLA_HEADS * HEAD_SLOT),
            ("vm", MLA_HEADS * MLA_V), ("qs", SWA_HEADS * SWA_HEAD_DIM), ("ks", LANES), ("vs", LANES)]
    res = pl.pallas_call(
        _prep_kernel,
        grid=(nt,),
        in_specs=[row(D_MODEL), pl.BlockSpec((None, 8, D_MODEL), lambda i: (bidx(i), 0, 0))]
                 + [full(a) for a in consts] + [tab_spec] * 6 + [full(tabs["dft64"])],
        out_specs=[row(w) for _, w in outs],
        out_shape=[jax.ShapeDtypeStruct((T, w), BF16) for _, w in outs],
        compiler_params=_cparams(("arbitrary",)),
        name="prep",
    )(xc, mod, *consts, tabs["cm"], tabs["sam"], tabs["sbm"], tabs["cs"], tabs["sas"], tabs["sbs"],
      tabs["dft64"])
    return dict(zip([n for n, _ in outs], res))


def _mla_kernel(*refs, with_x):
    if with_x:
        q_ref, kx_ref, kc_ref, vx_ref, vc_ref, o_ref = refs
    else:
        q_ref, kc_ref, vc_ref, o_ref = refs
    outs = []
    for hh in range(2):
        sl = slice(hh * HEAD_SLOT, (hh + 1) * HEAD_SLOT)
        q = q_ref[:, sl]
        sc = _dot_nt(q, kc_ref[:, sl])
        m = jnp.max(sc, axis=-1, keepdims=True)
        if with_x:
            sx = _dot_nt(q, kx_ref[:, sl])
            m = jnp.maximum(m, jnp.max(sx, axis=-1, keepdims=True))
            px = jnp.exp(sx - m)
        pc = jnp.exp(sc - m)
        l = jnp.sum(pc, axis=-1, keepdims=True)
        o = _dot(pc.astype(BF16), vc_ref[...])
        if with_x:
            l = l + jnp.sum(px, axis=-1, keepdims=True)
            o = o + _dot(px.astype(BF16), vx_ref[...])
        outs.append(o / l)
    lane = lax.broadcasted_iota(jnp.int32, (1, LANES), 1)
    o_ref[...] = jnp.where(lane < MLA_V, outs[0], outs[1]).astype(BF16)


def _ctx_rows_kernel(kernel_fn, *refs, **kw):
    kernel_fn(*refs[:-2], refs[-1], **kw)


def _mla_attend(qm, km, vm, B, S, C, prev=None):
    T = qm.shape[0]
    npair = MLA_HEADS // 2
    with_x = prev is None
    if with_x:
        tq = 512
        nq = S // tq
        qmap = lambda b, p, i: (b * nq + i, p)
        in_specs = [pl.BlockSpec((tq, 2 * HEAD_SLOT), qmap),
                    pl.BlockSpec((S, 2 * HEAD_SLOT), lambda b, p, i: (b, p)),
                    pl.BlockSpec((C, 2 * HEAD_SLOT), lambda b, p, i: (B * S // C + b, p)),
                    pl.BlockSpec((S, LANES), lambda b, p, i: (b, p)),
                    pl.BlockSpec((C, LANES), lambda b, p, i: (B * S // C + b, p))]
        args = (qm, km, km, vm, vm)
        body = functools.partial(_mla_kernel, with_x=True)
        aliases = {}
    else:
        tq = C
        nq = 1
        qmap = lambda b, p, i: (B * S // C + b, p)
        in_specs = [pl.BlockSpec((tq, 2 * HEAD_SLOT), qmap),
                    pl.BlockSpec((C, 2 * HEAD_SLOT), qmap),
                    pl.BlockSpec((C, LANES), qmap),
                    pl.BlockSpec(memory_space=pl.ANY)]
        args = (qm, km, vm, prev)
        body = functools.partial(_ctx_rows_kernel, _mla_kernel, with_x=False)
        aliases = {3: 0}
    return pl.pallas_call(
        body,
        grid=(B, npair, nq),
        in_specs=in_specs,
        out_specs=pl.BlockSpec((tq, LANES), qmap),
        out_shape=jax.ShapeDtypeStruct((T, MLA_HEADS * MLA_V), BF16),
        input_output_aliases=aliases,
        compiler_params=_cparams(("arbitrary",) * 3),
        name="mla_x" if with_x else "mla_c",
    )(*args)


def _swa_kernel(*refs, with_x, nblk):
    if with_x:
        sink_ref, q_ref, kp_ref, ko_ref, kn_ref, kc_ref, vp_ref, vo_ref, vn_ref, vc_ref, o_ref = refs
    else:
        sink_ref, q_ref, kc_ref, vc_ref, o_ref = refs
    n = pl.program_id(1)
    lane = lax.broadcasted_iota(jnp.int32, (1, LANES), 1)
    lo = lane < SWA_HEAD_DIM
    row2 = lax.broadcasted_iota(jnp.int32, (2 * BLOCK, 1), 0)
    if with_x:
        qi = lax.broadcasted_iota(jnp.int32, (2 * BLOCK, BLOCK), 0) % BLOCK
        kj = lax.broadcasted_iota(jnp.int32, (2 * BLOCK, BLOCK), 1)
        ok_prev = jnp.logical_and(kj - BLOCK - qi >= -WINDOW, n > 0)
        ok_next = jnp.logical_and(kj + BLOCK - qi <= WINDOW, n < nblk - 1)
    for s in range(SWA_HEADS // 2):
        q = q_ref[:, s * LANES:(s + 1) * LANES]
        zero = jnp.zeros_like(q)
        q2 = jnp.concatenate([jnp.where(lo, q, zero), jnp.where(lo, zero, q)], axis=0)
        sink = jnp.where(row2 < BLOCK, sink_ref[0, s], sink_ref[0, SWA_HEADS // 2 + s])
        sc = _dot_nt(q2, kc_ref[...])
        m = jnp.maximum(jnp.max(sc, axis=-1, keepdims=True), sink)
        if with_x:
            sp = jnp.where(ok_prev, _dot_nt(q2, kp_ref[...]), -jnp.inf)
            so = _dot_nt(q2, ko_ref[...])
            sn = jnp.where(ok_next, _dot_nt(q2, kn_ref[...]), -jnp.inf)
            m = jnp.maximum(m, jnp.max(sp, axis=-1, keepdims=True))
            m = jnp.maximum(m, jnp.max(so, axis=-1, keepdims=True))
            m = jnp.maximum(m, jnp.max(sn, axis=-1, keepdims=True))
        pc = jnp.exp(sc - m)
        l = jnp.sum(pc, axis=-1, keepdims=True) + jnp.exp(sink - m)
        o = _dot(pc.astype(BF16), vc_ref[...])
        if with_x:
            for sx, v_ref in ((sp, vp_ref), (so, vo_ref), (sn, vn_ref)):
                px = jnp.exp(sx - m)
                l = l + jnp.sum(px, axis=-1, keepdims=True)
                o = o + _dot(px.astype(BF16), v_ref[...])
        o = o / l
        o_ref[:, s * LANES:(s + 1) * LANES] = jnp.where(lo, o[:BLOCK], o[BLOCK:]).astype(BF16)


def _swa_attend(sink, qs, ks, vs, B, S, C, prev=None):
    T = qs.shape[0]
    cb = B * S // C
    with_x = prev is None
    if with_x:
        nblk = S // BLOCK
        qmap = lambda b, n: (b * nblk + n, 0)
        pmap = lambda b, n: (b * nblk + jnp.maximum(n - 1, 0), 0)
        nmap = lambda b, n: (b * nblk + jnp.minimum(n + 1, nblk - 1), 0)
        cmap = lambda b, n: (cb + b, 0)
        kv = lambda mp: pl.BlockSpec((BLOCK, LANES), mp)
        cspec = pl.BlockSpec((C, LANES), cmap)
        in_specs = [pl.BlockSpec(memory_space=pltpu.SMEM),
                    pl.BlockSpec((BLOCK, SWA_HEADS * SWA_HEAD_DIM), qmap),
                    kv(pmap), kv(qmap), kv(nmap), cspec, kv(pmap), kv(qmap), kv(nmap), cspec]
        args = (sink, qs, ks, ks, ks, ks, vs, vs, vs, vs)
        body = functools.partial(_swa_kernel, with_x=True, nblk=nblk)
        aliases = {}
    else:
        nblk = C // BLOCK
        qmap = lambda b, n: (cb * (C // BLOCK) + b * nblk + n, 0)
        cmap = lambda b, n: (cb + b, 0)
        cspec = pl.BlockSpec((C, LANES), cmap)
        in_specs = [pl.BlockSpec(memory_space=pltpu.SMEM),
                    pl.BlockSpec((BLOCK, SWA_HEADS * SWA_HEAD_DIM), qmap), cspec, cspec,
                    pl.BlockSpec(memory_space=pl.ANY)]
        args = (sink, qs, ks, vs, prev)
        body = functools.partial(_ctx_rows_kernel, _swa_kernel, with_x=False, nblk=nblk)
        aliases = {4: 0}
    return pl.pallas_call(
        body,
        grid=(B, nblk),
        in_specs=in_specs,
        out_specs=pl.BlockSpec((BLOCK, SWA_HEADS * SWA_HEAD_DIM), qmap),
        out_shape=jax.ShapeDtypeStruct((T, SWA_HEADS * SWA_HEAD_DIM), BF16),
        input_output_aliases=aliases,
        compiler_params=_cparams(("arbitrary",) * 2),
        name="swa_x" if with_x else "swa_c",
    )(*args)


def _fourier_kernel(c_ref, s_ref, pp_ref, w_ref, o_ref):
    f = _dot(c_ref[...], pp_ref[:, 0:D_FNET]) - _dot(s_ref[...], pp_ref[:, D_FNET:2 * D_FNET])
    o_ref[...] = _dot(f.astype(BF16), w_ref[...]).astype(BF16)


def _fourier(cmat, smat, pp, wblk, B, N, row0, prev=None):
    T = pp.shape[0]
    tq = min(512, N)
    nr = N // tq
    b0 = row0 // N
    o0 = row0 // tq
    in_specs = [pl.BlockSpec((tq, N), lambda r, b: (r, 0)),
                pl.BlockSpec((tq, N), lambda r, b: (r, 0)),
                pl.BlockSpec((N, 2 * D_FNET), lambda r, b: (b0 + b, 0)),
                pl.BlockSpec((D_FNET, D_FNET), lambda r, b: (0, 0))]
    args = (cmat, smat, pp, wblk)
    if prev is None:
        body = _fourier_kernel
        aliases = {}
    else:
        in_specs.append(pl.BlockSpec(memory_space=pl.ANY))
        args = args + (prev,)
        body = functools.partial(_ctx_rows_kernel, _fourier_kernel)
        aliases = {4: 0}
    return pl.pallas_call(
        body,
        grid=(nr, B),
        in_specs=in_specs,
        out_specs=pl.BlockSpec((tq, D_FNET), lambda r, b: (o0 + b * nr + r, 0)),
        out_shape=jax.ShapeDtypeStruct((T, D_FNET), BF16),
        input_output_aliases=aliases,
        compiler_params=_cparams(("arbitrary",) * 2),
        name="fourier_%d" % N,
    )(*args)


def _route_rows(sel, aff):
    G, K = N_EXPERT_GROUPS, EXPERTS_PER_GROUP
    gscore = []
    for g in range(G):
        a = sel[g * K:(g + 1) * K]
        best = None
        for i in range(K):
            for j in range(i + 1, K):
                v = a[i] + a[j]
                best = v if best is None else jnp.maximum(best, v)
        gscore.append(best)
    gb = jnp.zeros_like(gscore[0])
    gbest = gscore[0]
    for g in range(1, G):
        upd = gscore[g] > gbest
        gb = jnp.where(upd, float(g), gb)
        gbest = jnp.where(upd, gscore[g], gbest)
    cs, ca = [], []
    for i in range(K):
        c, a = sel[i], aff[i]
        for g in range(1, G):
            pick = gb == float(g)
            c = jnp.where(pick, sel[g * K + i], c)
            a = jnp.where(pick, aff[g * K + i], a)
        cs.append(c)
        ca.append(a)

    def first_max(vals):
        bi = jnp.zeros_like(vals[0])
        bv = vals[0]
        for i in range(1, K):
            upd = vals[i] > bv
            bi = jnp.where(upd, float(i), bi)
            bv = jnp.where(upd, vals[i], bv)
        return bi

    i1 = first_max(cs)
    cs2 = [jnp.where(i1 == float(i), -jnp.inf, cs[i]) for i in range(K)]
    i2 = first_max(cs2)
    a1 = sum(jnp.where(i1 == float(i), ca[i], 0.0) for i in range(K))
    a2 = sum(jnp.where(i2 == float(i), ca[i], 0.0) for i in range(K))
    den = a1 + a2
    return gb * K + i1, gb * K + i2, a1 / den, a2 / den


def _post_kernel(x_ref, mod_ref, fo_ref, ml_ref, sw_ref, wof_ref, wom_ref, wos_ref, n2g_ref,
                 rwh_ref, rwl_ref, rb_ref, tri_ref, ones_ref, xn_ref, h2_ref, rc_ref, rr_ref, cnt_ref):
    m = mod_ref[...]
    mix = (_dot(fo_ref[...], wof_ref[...]) + _dot(ml_ref[...], wom_ref[...])
           + _dot(sw_ref[...], wos_ref[...]))
    xn = x_ref[...] + m[2:3] * mix
    xn_ref[...] = xn
    h2 = _rms(xn, D_MODEL) * n2g_ref[...] * (1.0 + m[4:5]) + m[3:4]
    h2_ref[...] = h2.astype(BF16)
    hh = h2.astype(BF16)
    hl = (h2 - hh.astype(F32)).astype(BF16)
    logits = _dot(hh, rwh_ref[...]) + (_dot(hl, rwh_ref[...]) + _dot(hh, rwl_ref[...]))
    lt = logits.T[0:N_EXPERTS, :]
    aff_t = jax.nn.sigmoid(lt)
    sel_t = aff_t + rb_ref[...]
    sel = [sel_t[e:e + 1, :] for e in range(N_EXPERTS)]
    aff = [aff_t[e:e + 1, :] for e in range(N_EXPERTS)]
    e1, e2, w1, w2 = _route_rows(sel, aff)

    eio = lax.broadcasted_iota(jnp.int32, (N_EXPERTS, TM), 0).astype(F32)
    oh = jnp.concatenate([jnp.where(eio == e1, 1.0, 0.0), jnp.where(eio == e2, 1.0, 0.0)], axis=1)
    ohb = oh.astype(BF16)
    rank = _dot(ohb, tri_ref[...])
    cnt = _dot(ohb, ones_ref[...]).astype(jnp.int32)
    cnt8 = jnp.left_shift(jnp.right_shift(cnt + (CHUNK - 1), CHUNK_LOG2), CHUNK_LOG2)
    cnt_ref[...] = cnt8
    cnt8f = cnt8.astype(F32)
    off = jnp.zeros((1, 1), F32)
    slot = jnp.zeros((1, 2 * TM), F32)
    for e in range(N_EXPERTS):
        slot = slot + oh[e:e + 1, :] * (off + rank[e:e + 1, :])
        off = off + cnt8f[e:e + 1, 0:1]
    s0, s1 = slot[:, :TM], slot[:, TM:]

    sub = lax.broadcasted_iota(jnp.int32, (8, TM), 0)
    rr_ref[...] = jnp.where(sub == 0, s0, jnp.where(sub == 1, s1, 0.0))
    blk = jnp.where(sub == 0, e1, jnp.where(sub == 1, e2, jnp.where(sub == 2, w1, jnp.where(
        sub == 3, w2, jnp.where(sub == 4, s0, jnp.where(sub == 5, s1, 0.0))))))
    rows = jnp.concatenate([blk, jnp.zeros((LANES - 8, TM), F32)], axis=0)
    rc_ref[...] = rows.T


def _post(xc, mod, fo, ml, sw, wl, n2g, rw_hi, rw_lo, rb_col, nt, nx_tiles, tiles_per_batch):
    rows = nt * TM

    def bidx(i):
        return jnp.where(i < nx_tiles, i // tiles_per_batch, mod.shape[0] - 1)

    def full(a):
        return pl.BlockSpec(a.shape, lambda i: (0,) * a.ndim)

    row = lambda w: pl.BlockSpec((TM, w), lambda i: (i, 0))
    pair = np.arange(2 * TM)
    tri = jnp.asarray(pair[:, None] < pair[None, :], BF16)
    ones = jnp.ones((2 * TM, LANES), BF16)
    consts = [wl["wo_f"], wl["wo_m"], wl["wo_s"], n2g, rw_hi, rw_lo, rb_col, tri, ones]
    return pl.pallas_call(
        _post_kernel,
        grid=(nt,),
        in_specs=[row(D_MODEL), pl.BlockSpec((None, 8, D_MODEL), lambda i: (bidx(i), 0, 0)),
                  row(D_FNET), row(MLA_HEADS * MLA_V), row(SWA_HEADS * SWA_HEAD_DIM)]
                 + [full(a) for a in consts],
        out_specs=[row(D_MODEL), row(D_MODEL), row(LANES),
                   pl.BlockSpec((None, 8, TM), lambda i: (i, 0, 0)),
                   pl.BlockSpec((None, N_EXPERTS, LANES), lambda i: (i, 0, 0))],
        out_shape=[jax.ShapeDtypeStruct((rows, D_MODEL), F32),
                   jax.ShapeDtypeStruct((rows, D_MODEL), BF16),
                   jax.ShapeDtypeStruct((rows, LANES), F32),
                   jax.ShapeDtypeStruct((nt, 8, TM), F32),
                   jax.ShapeDtypeStruct((nt, N_EXPERTS, LANES), jnp.int32)],
        compiler_params=_cparams(("arbitrary",)),
        name="post",
    )(xc, mod, fo, ml, sw, *consts)


def _moe_tables(cnt8, te):
    nt = cnt8.shape[0]
    tile_prefix = jnp.cumsum(cnt8, axis=0) - cnt8
    tot = jnp.sum(cnt8, axis=0)
    tot_e = ((tot + te - 1) // te) * te
    goff = jnp.cumsum(tot_e) - tot_e
    dbase = goff[None, :] + tile_prefix
    nch = cnt8 // CHUNK
    cum = jnp.cumsum(nch, axis=1)
    k = jnp.arange(MAX_CHUNKS, dtype=jnp.int32)
    e_of_k = jnp.minimum(jnp.sum(k[None, :, None] >= cum[:, None, :], axis=-1), N_EXPERTS - 1)
    start = jnp.take_along_axis(cum - nch, e_of_k, axis=1)
    dst = jnp.take_along_axis(dbase, e_of_k, axis=1) + CHUNK * (k[None, :] - start)
    nchunks = cum[:, -1]
    dst = jnp.where(k[None, :] < nchunks[:, None], dst, 0)
    padch = (tot_e - tot) // CHUNK
    cump = jnp.cumsum(padch)
    kp = jnp.arange(N_EXPERTS * (te // CHUNK), dtype=jnp.int32)
    e_of_p = jnp.minimum(jnp.sum(kp[:, None] >= cump[None, :], axis=-1), N_EXPERTS - 1)
    pdst = (goff + tot)[e_of_p] + CHUNK * (kp - (cump - padch)[e_of_p])
    npad = cump[-1]
    pdst = jnp.where(kp < npad, pdst, 0)
    ntile_cum = jnp.cumsum(tot_e // te)
    nact = ntile_cum[-1]
    return dict(dst=dst.reshape(-1).astype(jnp.int32), nchunks=nchunks.astype(jnp.int32),
                pdst=pdst.astype(jnp.int32), npad=npad.reshape(1).astype(jnp.int32),
                ntile_cum=ntile_cum.astype(jnp.int32), nact=nact.reshape(1).astype(jnp.int32))


def _dispatch_kernel(dst_ref, nch_ref, pdst_ref, npad_ref, h_ref, rr_ref, xs_ref, sbuf, zbuf, sem, zsem):
    i = pl.program_id(0)
    nt = pl.num_programs(0)
    slot = i % 2

    def chunk_copy(sl, k, d):
        return pltpu.make_async_copy(
            sbuf.at[sl, pl.ds(pl.multiple_of(k * CHUNK, CHUNK), CHUNK), :],
            xs_ref.at[pl.ds(pl.multiple_of(d, CHUNK), CHUNK), :], sem.at[sl])

    def wait_tile(t, sl):
        def body(k, c):
            chunk_copy(sl, 0, 0).wait()
            return c
        lax.fori_loop(0, nch_ref[t], body, 0)

    def pad_copy(d):
        return pltpu.make_async_copy(zbuf, xs_ref.at[pl.ds(pl.multiple_of(d, CHUNK), CHUNK), :], zsem)

    @pl.when(i == 0)
    def _():
        zbuf[...] = jnp.zeros_like(zbuf)

        def start(k, c):
            pad_copy(pdst_ref[k]).start()
            return c
        lax.fori_loop(0, npad_ref[0], start, 0)

        def wait(k, c):
            pad_copy(0).wait()
            return c
        lax.fori_loop(0, npad_ref[0], wait, 0)

    @pl.when(i >= 2)
    def _():
        wait_tile(i - 2, slot)

    rr = rr_ref[...]
    sio = lax.broadcasted_iota(jnp.int32, (NSLOT, TM), 0).astype(F32)
    psel = jnp.where(jnp.logical_or(sio == rr[0:1, :], sio == rr[1:2, :]), 1.0, 0.0).astype(BF16)
    sbuf[slot] = _dot(psel, h_ref[...])

    def issue(k, c):
        chunk_copy(slot, k, dst_ref[i * MAX_CHUNKS + k]).start()
        return c
    lax.fori_loop(0, nch_ref[i], issue, 0)

    @pl.when(i == nt - 1)
    def _():
        wait_tile(i, slot)

        @pl.when(i >= 1)
        def _():
            wait_tile(i - 1, 1 - slot)


def _dispatch(h2, rr, tb, rows_sorted):
    nt = rr.shape[0]
    return pl.pallas_call(
        _dispatch_kernel,
        grid_spec=pltpu.PrefetchScalarGridSpec(
            num_scalar_prefetch=4,
            grid=(nt,),
            in_specs=[pl.BlockSpec((TM, D_MODEL), lambda i, *_: (i, 0)),
                      pl.BlockSpec((None, 8, TM), lambda i, *_: (i, 0, 0))],
            out_specs=pl.BlockSpec(memory_space=pl.ANY),
            scratch_shapes=[pltpu.VMEM((2, NSLOT, D_MODEL), F32), pltpu.VMEM((CHUNK, D_MODEL), F32),
                            pltpu.SemaphoreType.DMA((2,)), pltpu.SemaphoreType.DMA(())]),
        out_shape=jax.ShapeDtypeStruct((rows_sorted, D_MODEL), F32),
        compiler_params=_cparams(("arbitrary",)),
        name="moe_dispatch",
    )(tb["dst"], tb["nchunks"], tb["pdst"], tb["npad"], h2, rr)


def _expert_kernel(te_ref, na_ref, x_ref, wg_ref, wu_ref, wd_ref, o_ref, wgb, wub, wdb):
    j = pl.program_id(0)
    active = j < na_ref[0]
    fresh = jnp.logical_or(j == 0, te_ref[j] != te_ref[jnp.maximum(j - 1, 0)])

    @pl.when(jnp.logical_and(active, fresh))
    def _():
        wgb[...] = wg_ref[...].astype(BF16)
        wub[...] = wu_ref[...].astype(BF16)
        wdb[...] = wd_ref[...].astype(BF16)

    @pl.when(active)
    def _():
        x = x_ref[...].astype(BF16)
        a = _dot(x, wgb[...])
        a = a * jax.nn.sigmoid(a) * _dot(x, wub[...])
        o_ref[...] = _dot(a.astype(BF16), wdb[...])


def _experts(xs, wg, wu, wd, tb, te):
    nte = xs.shape[0] // te
    jj = jnp.minimum(jnp.arange(nte, dtype=jnp.int32), tb["nact"][0] - 1)
    tile_e = jnp.sum(jj[:, None] >= tb["ntile_cum"][None, :], axis=-1).astype(jnp.int32)

    def tmap(j, te_ref, na):
        return (jnp.minimum(j, na[0] - 1), 0)

    def wmap(j, te_ref, na):
        return (te_ref[j], 0, 0)

    return pl.pallas_call(
        _expert_kernel,
        grid_spec=pltpu.PrefetchScalarGridSpec(
            num_scalar_prefetch=2,
            grid=(nte,),
            in_specs=[pl.BlockSpec((te, D_MODEL), tmap),
                      pl.BlockSpec((None, D_MODEL, D_EXPERT), wmap),
                      pl.BlockSpec((None, D_MODEL, D_EXPERT), wmap),
                      pl.BlockSpec((None, D_EXPERT, D_MODEL), wmap)],
            out_specs=pl.BlockSpec((te, D_MODEL), tmap),
            scratch_shapes=[pltpu.VMEM((D_MODEL, D_EXPERT), BF16), pltpu.VMEM((D_MODEL, D_EXPERT), BF16),
                            pltpu.VMEM((D_EXPERT, D_MODEL), BF16)]),
        out_shape=jax.ShapeDtypeStruct(xs.shape, F32),
        compiler_params=_cparams(("arbitrary",)),
        name="moe_experts",
    )(tile_e, tb["nact"], xs, wg, wu, wd)


def _combine_kernel(dst_ref, nch_ref, xn_ref, mod_ref, rc_ref, ys_ref, o_ref, gbuf, sem):
    i = pl.program_id(0)
    nt = pl.num_programs(0)
    slot = i % 2

    def chunk_copy(sl, k, d):
        return pltpu.make_async_copy(
            ys_ref.at[pl.ds(pl.multiple_of(d, CHUNK), CHUNK), :],
            gbuf.at[sl, pl.ds(pl.multiple_of(k * CHUNK, CHUNK), CHUNK), :], sem.at[sl])

    def issue_tile(t, sl):
        def body(k, c):
            chunk_copy(sl, k, dst_ref[t * MAX_CHUNKS + k]).start()
            return c
        lax.fori_loop(0, nch_ref[t], body, 0)

    @pl.when(i == 0)
    def _():
        gbuf[...] = jnp.zeros_like(gbuf)
        issue_tile(0, 0)

    @pl.when(i + 1 < nt)
    def _():
        issue_tile(i + 1, 1 - slot)

    def wait(k, c):
        chunk_copy(slot, 0, 0).wait()
        return c
    lax.fori_loop(0, nch_ref[i], wait, 0)

    g = gbuf[slot].astype(BF16)
    rc = rc_ref[...]
    lio = lax.broadcasted_iota(jnp.int32, (TM, NSLOT), 1).astype(F32)
    p0 = jnp.where(lio == rc[:, 4:5], 1.0, 0.0).astype(BF16)
    p1 = jnp.where(lio == rc[:, 5:6], 1.0, 0.0).astype(BF16)
    y = rc[:, 2:3] * _dot(p0, g) + rc[:, 3:4] * _dot(p1, g)
    o_ref[...] = xn_ref[...] + mod_ref[5:6, :] * y


def _combine(xn, mod, rc, ys, tb, nx_tiles, tiles_per_batch):
    rows = xn.shape[0]
    nt = rows // TM

    def bidx(i, *_):
        return (jnp.where(i < nx_tiles, i // tiles_per_batch, mod.shape[0] - 1), 0, 0)

    return pl.pallas_call(
        _combine_kernel,
        grid_spec=pltpu.PrefetchScalarGridSpec(
            num_scalar_prefetch=2,
            grid=(nt,),
            in_specs=[pl.BlockSpec((TM, D_MODEL), lambda i, *_: (i, 0)),
                      pl.BlockSpec((None, 8, D_MODEL), bidx),
                      pl.BlockSpec((TM, LANES), lambda i, *_: (i, 0)),
                      pl.BlockSpec(memory_space=pl.ANY)],
            out_specs=pl.BlockSpec((TM, D_MODEL), lambda i, *_: (i, 0)),
            scratch_shapes=[pltpu.VMEM((2, NSLOT, D_MODEL), F32), pltpu.SemaphoreType.DMA((2,))]),
        out_shape=jax.ShapeDtypeStruct((rows, D_MODEL), F32),
        compiler_params=_cparams(("arbitrary",)),
        name="moe_combine",
    )(tb["dst"], tb["nchunks"], xn, mod, rc, ys)


def _moe(h2, rc, rr, cnt, wg, wu, wd, xn, mod, nx_tiles, tiles_per_batch):
    nt = rr.shape[0]
    te = TE
    max_rows = 2 * nt * TM + (CHUNK - 1) * N_EXPERTS * nt + N_EXPERTS * (te - CHUNK)
    rows_sorted = ((max_rows + te - 1) // te) * te
    tb = _moe_tables(cnt[:, :, 0], te)
    xs = _dispatch(h2, rr, tb, rows_sorted)
    ys = _experts(xs, wg, wu, wd, tb, te)
    return _combine(xn, mod, rc, ys, tb, nx_tiles, tiles_per_batch)


def _rope_tables(S, C):
    t = jnp.arange(S)
    rows, cols = (t // GRID_W).astype(F32), (t % GRID_W).astype(F32)

    def axis_tabs(d_rot, lane0, width):
        d_axis = d_rot // 2
        inv = ROPE_THETA ** (-jnp.arange(0, d_axis, 2, dtype=F32) / d_axis)
        ar, ac = rows[:, None] * inv, cols[:, None] * inv
        ang = jnp.concatenate([ar, ar, ac, ac], axis=-1)
        q = d_rot // 4
        first = np.concatenate([np.ones(q), np.zeros(q), np.ones(q), np.zeros(q)]).astype(np.float32)
        pad = ((0, C), (lane0, width - lane0 - d_rot))
        cos = jnp.pad(jnp.cos(ang) - 1.0, pad) + 1.0
        sa = jnp.pad(-jnp.sin(ang) * first, pad)
        sb = jnp.pad(jnp.sin(ang) * (1.0 - first), pad)
        return cos, sa, sb

    cm, sam, sbm = axis_tabs(MLA_ROPE, MLA_NOPE, LANES)
    cs, sas, sbs = axis_tabs(SWA_HEAD_DIM, 0, SWA_HEAD_DIM)
    tile2 = lambda a: jnp.concatenate([a, a], axis=1)
    return dict(cm=cm, sam=sam, sbm=sbm, cs=tile2(cs), sas=tile2(sas), sbs=tile2(sbs))


_TWO_PI_HI = float(np.float32(2.0 * np.pi))
_TWO_PI_LO = float(np.float32(2.0 * np.pi - np.float64(np.float32(2.0 * np.pi))))


def _dft_mats(N):
    k = jnp.arange(N, dtype=jnp.int32)
    frac = ((k[:, None] * k[None, :]) % N).astype(F32) / N
    ang = _TWO_PI_HI * frac + _TWO_PI_LO * frac
    return jnp.cos(ang), jnp.sin(ang)


def _dft64_blocks():
    c, s = _dft_mats(FNET_CH)
    eye = jnp.eye(FNET_GROUPS, dtype=F32)
    return jnp.concatenate([jnp.kron(eye, c), jnp.kron(eye, s)], axis=1)


def _layer_weights(l, w_in, fnet_w, mla_cq_g, mla_ckv_g, mla_w_uq, mla_w_uk, mla_w_uv, mla_q_g, mla_k_g,
                   swa_q_g, swa_k_g, swa_sink, w_out):
    D = D_MODEL
    wi = w_in[l]
    o_kr = D_FNET + MLA_Q_RANK + MLA_KV_RANK
    o_qs = o_kr + MLA_ROPE
    o_ks = o_qs + SWA_HEADS * SWA_HEAD_DIM
    o_vs = o_ks + SWA_KV_HEADS * SWA_HEAD_DIM
    order = np.array(SWA_HEAD_ORDER)
    w_qs = wi[:, o_qs:o_ks].reshape(D, SWA_HEADS, SWA_HEAD_DIM)[:, order].reshape(D, -1)
    z = lambda n: jnp.zeros((D, n), F32)
    win = jnp.concatenate([wi[:, :o_kr], w_qs, wi[:, o_ks:o_vs], wi[:, o_vs:],
                           z(MLA_NOPE), wi[:, o_kr:o_qs], z(LANES - MLA_QK)], axis=1)
    pad_slot = lambda w, d: jnp.pad(w.reshape(w.shape[0], MLA_HEADS, d),
                                    ((0, 0), (0, 0), (0, HEAD_SLOT - d))).reshape(w.shape[0], -1)
    wo = w_out[l]
    o_m = D_FNET
    o_s = D_FNET + MLA_HEADS * MLA_V
    wo_s = wo[o_s:].reshape(SWA_HEADS, SWA_HEAD_DIM, D)[order].reshape(-1, D)
    fw = fnet_w[l]
    wblk = jnp.zeros((D_FNET, D_FNET), F32)
    for g in range(FNET_GROUPS):
        wblk = wblk.at[g * FNET_CH:(g + 1) * FNET_CH, g * FNET_CH:(g + 1) * FNET_CH].set(fw[g])
    pad_g = lambda g: jnp.pad(g, (0, HEAD_SLOT - MLA_QK)).reshape(1, HEAD_SLOT)
    return dict(
        w_in=win.astype(BF16),
        cq_g=mla_cq_g[l].reshape(1, -1), ckv_g=mla_ckv_g[l].reshape(1, -1),
        w_uq=pad_slot(mla_w_uq[l], MLA_QK).astype(BF16),
        w_uk=pad_slot(mla_w_uk[l], MLA_NOPE).astype(BF16),
        w_uv=mla_w_uv[l].astype(BF16),
        mq_g=pad_g(mla_q_g[l]), mk_g=pad_g(mla_k_g[l]),
        sq_g=jnp.tile(swa_q_g[l], 2).reshape(1, LANES), sk_g=jnp.tile(swa_k_g[l], 2).reshape(1, LANES),
        sink=swa_sink[l].reshape(1, SWA_HEADS),
        wo_f=wo[:o_m].astype(BF16), wo_m=wo[o_m:o_s].astype(BF16), wo_s=wo_s.astype(BF16),
        fnet=wblk,
    )


def kernel(x, c, ctx, c_ctx, ada_w, ada_b, norm1_g, norm2_g, w_in, fnet_w, mla_cq_g, mla_ckv_g, mla_w_uq,
           mla_w_uk, mla_w_uv, mla_q_g, mla_k_g, swa_q_g, swa_k_g, swa_sink, w_out, router_w, router_b,
           exp_w_gate, exp_w_up, exp_w_down):
    B, S, D = x.shape
    C = ctx.shape[1]
    L = ada_w.shape[0]
    assert D == D_MODEL and S % 512 == 0 and C == TM and S % TM == 0
    nx_tiles = B * S // TM
    nt_all = nx_tiles + B * C // TM
    tiles_per_batch = S // TM

    tabs = _rope_tables(S, C)
    tabs["dft64"] = _dft64_blocks().astype(BF16)
    dft = {n: tuple(m.astype(BF16) for m in _dft_mats(n)) for n in (S, C)}
    fscale = {n: lax.rsqrt(jnp.full((), n * FNET_CH, F32)) for n in (S, C)}

    nmod = 16
    cvec = jnp.concatenate([c, c_ctx[None, :], jnp.zeros((nmod - B - 1, D), F32)], axis=0)
    mod_all = _adaln(cvec, ada_w, ada_b)
    mod_all = mod_all[:, :B + 1].reshape(L, B + 1, 6, D)
    mod_all = jnp.pad(mod_all, ((0, 0), (0, 0), (0, 2), (0, 0)))

    rw = jnp.pad(router_w, ((0, 0), (0, LANES - N_EXPERTS)))
    rw_hi = rw.astype(BF16)
    rw_lo = (rw - rw_hi.astype(F32)).astype(BF16)
    rb_col = router_b.reshape(N_EXPERTS, 1)

    xc = jnp.concatenate([x.reshape(B * S, D), ctx.reshape(B * C, D)], axis=0)
    for l in range(L):
        last = l == L - 1
        wl = _layer_weights(l, w_in, fnet_w, mla_cq_g, mla_ckv_g, mla_w_uq, mla_w_uk, mla_w_uv, mla_q_g,
                            mla_k_g, swa_q_g, swa_k_g, swa_sink, w_out)
        mod = mod_all[l]
        pr = _prep(xc, mod, norm1_g[l].reshape(1, D), wl, tabs, nx_tiles, tiles_per_batch)
        fo = _fourier(dft[S][0], dft[S][1], pr["pp"], (wl["fnet"] * fscale[S]).astype(BF16), B, S, 0)
        ml = _mla_attend(pr["qm"], pr["km"], pr["vm"], B, S, C)
        sw = _swa_attend(wl["sink"], pr["qs"], pr["ks"], pr["vs"], B, S, C)
        if not last:
            fo = _fourier(dft[C][0], dft[C][1], pr["pp"], (wl["fnet"] * fscale[C]).astype(BF16), B, C, B * S,
                          prev=fo)
            ml = _mla_attend(pr["qm"], pr["km"], pr["vm"], B, S, C, prev=ml)
            sw = _swa_attend(wl["sink"], pr["qs"], pr["ks"], pr["vs"], B, S, C, prev=sw)
        nt = nx_tiles if last else nt_all
        xn, h2, rc, rr, cnt = _post(xc, mod, fo, ml, sw, wl, norm2_g[l].reshape(1, D), rw_hi, rw_lo, rb_col,
                                    nt, nx_tiles, tiles_per_batch)
        xc = _moe(h2, rc, rr, cnt, exp_w_gate[l], exp_w_up[l], exp_w_down[l], xn, mod, nx_tiles,
                  tiles_per_batch)
    return xc[:B * S].reshape(B, S, D)
```

```python
import functools

import numpy as np
import jax
import jax.numpy as jnp
from jax import lax
from jax.experimental import pallas as pl
from jax.experimental.pallas import tpu as pltpu

F32 = jnp.float32
BF16 = jnp.bfloat16

D_MODEL = 1024
GRID_W = 64
FNET_GROUPS = 4
FNET_CH = 64
D_FNET = FNET_GROUPS * FNET_CH
MLA_HEADS = 6
MLA_Q_RANK = 256
MLA_KV_RANK = 128
MLA_NOPE = 64
MLA_ROPE = 32
MLA_QK = MLA_NOPE + MLA_ROPE
MLA_V = 64
SWA_HEADS = 6
SWA_KV_HEADS = 2
SWA_HEAD_DIM = 64
WINDOW = 128
BLOCK = 128
N_EXPERTS = 16
N_EXPERT_GROUPS = 4
EXPERTS_PER_GROUP = 4
D_EXPERT = 512
ROPE_THETA = 10000.0
EPS = 1e-6

LANES = 128
TM = 256
HEAD_SLOT = LANES
CHUNK = 8
CHUNK_LOG2 = 3
NSLOT = 640
MAX_CHUNKS = NSLOT // CHUNK
TE = 512
PREP_CHAINS = 2
VMEM_LIMIT = 48 * 1024 * 1024

P_UF = 0
P_CQ = 256
P_CKV = 512
P_QS = 640
P_KS = 1024
P_VS = 1152
P_KR = 1280
P_TOT = 1408
SWA_HEAD_ORDER = (0, 3, 1, 4, 2, 5)


def _dot(a, b):
    return jnp.dot(a, b, preferred_element_type=F32)


def _dot_nt(a, b):
    return lax.dot_general(a, b, (((1,), (1,)), ((), ())), preferred_element_type=F32)


def _rms(x, n):
    return x * lax.rsqrt(jnp.sum(x * x, axis=-1, keepdims=True) / n + EPS)


def _rope(x, c, sa, sb, half):
    n = x.shape[-1]
    return x * c + pltpu.roll(x, n - half, 1) * sa + pltpu.roll(x, half, 1) * sb


def _cparams(sem):
    return pltpu.CompilerParams(dimension_semantics=sem, vmem_limit_bytes=VMEM_LIMIT)


def _adaln_kernel(c_ref, w_ref, b_ref, o_ref):
    c = c_ref[...]
    sc = c * jax.nn.sigmoid(c)
    o_ref[...] = jnp.dot(sc, w_ref[...], preferred_element_type=F32,
                         precision=lax.Precision.HIGHEST) + b_ref[...]


def _adaln(cvec, ada_w, ada_b):
    L, D, N6 = ada_w.shape
    R = cvec.shape[0]
    bn = 512
    return pl.pallas_call(
        _adaln_kernel,
        grid=(L, N6 // bn),
        in_specs=[pl.BlockSpec((R, D), lambda l, j: (0, 0)),
                  pl.BlockSpec((None, D, bn), lambda l, j: (l, 0, j)),
                  pl.BlockSpec((None, 1, bn), lambda l, j: (l, 0, j))],
        out_specs=pl.BlockSpec((None, R, bn), lambda l, j: (l, 0, j)),
        out_shape=jax.ShapeDtypeStruct((L, R, N6), F32),
        compiler_params=_cparams(("arbitrary", "arbitrary")),
        name="adaln",
    )(cvec, ada_w, ada_b.reshape(L, 1, N6))


def _prep_kernel(x_ref, mod_ref, n1g_ref, win_ref, cqg_ref, ckvg_ref, wuq_ref, wuk_ref, wuv_ref,
                 mqg_ref, mkg_ref, sqg_ref, skg_ref, cm_ref, sam_ref, sbm_ref, cs_ref, sas_ref, sbs_ref,
                 dft_ref, pp_ref, qm_ref, km_ref, vm_ref, qs_ref, ks_ref, vs_ref):
    m = mod_ref[...]
    lo = lax.broadcasted_iota(jnp.int32, (1, LANES), 1) < SWA_HEAD_DIM

    def head_norm(slab, g):
        sq = slab * slab
        s_lo = jnp.sum(jnp.where(lo, sq, 0.0), axis=-1, keepdims=True)
        s_hi = jnp.sum(jnp.where(lo, 0.0, sq), axis=-1, keepdims=True)
        r = jnp.where(lo, lax.rsqrt(s_lo * (1.0 / SWA_HEAD_DIM) + EPS),
                      lax.rsqrt(s_hi * (1.0 / SWA_HEAD_DIM) + EPS))
        return slab * r * g

    rows = TM // PREP_CHAINS
    for ch in range(PREP_CHAINS):
        rs = slice(ch * rows, (ch + 1) * rows)
        x = x_ref[rs, :]
        h = _rms(x, D_MODEL) * n1g_ref[...] * (1.0 + m[1:2]) + m[0:1]
        p = _dot(h.astype(BF16), win_ref[...])

        u = p[:, P_UF:P_UF + D_FNET].astype(BF16)
        pp_ref[rs, :] = _dot(u, dft_ref[...]).astype(BF16)

        cm, sam, sbm = cm_ref[rs, :], sam_ref[rs, :], sbm_ref[rs, :]
        cs, sas, sbs = cs_ref[rs, :], sas_ref[rs, :], sbs_ref[rs, :]

        cq = _rms(p[:, P_CQ:P_CQ + MLA_Q_RANK], MLA_Q_RANK) * cqg_ref[...]
        qraw = _dot(cq.astype(BF16), wuq_ref[...])
        mqg = mqg_ref[...]
        for hh in range(MLA_HEADS):
            sl = slice(hh * HEAD_SLOT, (hh + 1) * HEAD_SLOT)
            qn = _rms(qraw[:, sl], MLA_QK) * mqg
            qm_ref[rs, sl] = (_rope(qn, cm, sam, sbm, MLA_ROPE // 4) * (MLA_QK ** -0.5)).astype(BF16)

        ckv = (_rms(p[:, P_CKV:P_CKV + MLA_KV_RANK], MLA_KV_RANK) * ckvg_ref[...]).astype(BF16)
        knope = _dot(ckv, wuk_ref[...])
        vm_ref[rs, :] = _dot(ckv, wuv_ref[...]).astype(BF16)
        kr = p[:, P_KR:P_KR + LANES]
        mkg = mkg_ref[...]
        kr_ss = jnp.sum(kr * kr, axis=-1, keepdims=True)
        kr_rot = _rope(kr * mkg, cm, sam, sbm, MLA_ROPE // 4)
        for hh in range(MLA_HEADS):
            sl = slice(hh * HEAD_SLOT, (hh + 1) * HEAD_SLOT)
            kn_h = knope[:, sl]
            r = lax.rsqrt((jnp.sum(kn_h * kn_h, axis=-1, keepdims=True) + kr_ss) / MLA_QK + EPS)
            km_ref[rs, sl] = ((kn_h * mkg + kr_rot) * r).astype(BF16)

        sqg = sqg_ref[...]
        for s in range(SWA_HEADS // 2):
            sl = slice(P_QS + s * LANES, P_QS + (s + 1) * LANES)
            qn = head_norm(p[:, sl], sqg)
            qs_ref[rs, s * LANES:(s + 1) * LANES] = (
                _rope(qn, cs, sas, sbs, SWA_HEAD_DIM // 4) * (SWA_HEAD_DIM ** -0.5)).astype(BF16)
        kn = head_norm(p[:, P_KS:P_KS + LANES], skg_ref[...])
        ks_ref[rs, :] = _rope(kn, cs, sas, sbs, SWA_HEAD_DIM // 4).astype(BF16)
        vs_ref[rs, :] = p[:, P_VS:P_VS + LANES].astype(BF16)


def _prep(xc, mod, n1g, wl, tabs, nx_tiles, tiles_per_batch):
    T = xc.shape[0]
    nt = T // TM

    def bidx(i):
        return jnp.where(i < nx_tiles, i // tiles_per_batch, mod.shape[0] - 1)

    def ridx(i):
        return jnp.where(i < nx_tiles, i % tiles_per_batch, tiles_per_batch)

    def full(a):
        return pl.BlockSpec(a.shape, lambda i: (0,) * a.ndim)

    tab_spec = pl.BlockSpec((TM, LANES), lambda i: (ridx(i), 0))
    row = lambda w: pl.BlockSpec((TM, w), lambda i: (i, 0))
    consts = [n1g, wl["w_in"], wl["cq_g"], wl["ckv_g"], wl["w_uq"], wl["w_uk"], wl["w_uv"],
              wl["mq_g"], wl["mk_g"], wl["sq_g"], wl["sk_g"]]
    outs = [("pp", 2 * D_FNET), ("qm", MLA_HEADS * HEAD_SLOT), ("km", MLA_HEADS * HEAD_SLOT),
            ("vm", MLA_HEADS * MLA_V), ("qs", SWA_HEADS * SWA_HEAD_DIM), ("ks", LANES), ("vs", LANES)]
    res = pl.pallas_call(
        _prep_kernel,
        grid=(nt,),
        in_specs=[row(D_MODEL), pl.BlockSpec((None, 8, D_MODEL), lambda i: (bidx(i), 0, 0))]
                 + [full(a) for a in consts] + [tab_spec] * 6 + [full(tabs["dft64"])],
        out_specs=[row(w) for _, w in outs],
        out_shape=[jax.ShapeDtypeStruct((T, w), BF16) for _, w in outs],
        compiler_params=_cparams(("arbitrary",)),
        name="prep",
    )(xc, mod, *consts, tabs["cm"], tabs["sam"], tabs["sbm"], tabs["cs"], tabs["sas"], tabs["sbs"],
      tabs["dft64"])
    return dict(zip([n for n, _ in outs], res))


def _mla_kernel(*refs, with_x):
    if with_x:
        q_ref, kx_ref, kc_ref, vx_ref, vc_ref, o_ref = refs
    else:
        q_ref, kc_ref, vc_ref, o_ref = refs
    outs = []
    for hh in range(2):
        sl = slice(hh * HEAD_SLOT, (hh + 1) * HEAD_SLOT)
        q = q_ref[:, sl]
        sc = _dot_nt(q, kc_ref[:, sl])
        m = jnp.max(sc, axis=-1, keepdims=True)
        if with_x:
            sx = _dot_nt(q, kx_ref[:, sl])
            m = jnp.maximum(m, jnp.max(sx, axis=-1, keepdims=True))
            px = jnp.exp(sx - m)
        pc = jnp.exp(sc - m)
        l = jnp.sum(pc, axis=-1, keepdims=True)
        o = _dot(pc.astype(BF16), vc_ref[...])
        if with_x:
            l = l + jnp.sum(px, axis=-1, keepdims=True)
            o = o + _dot(px.astype(BF16), vx_ref[...])
        outs.append(o / l)
    lane = lax.broadcasted_iota(jnp.int32, (1, LANES), 1)
    o_ref[...] = jnp.where(lane < MLA_V, outs[0], outs[1]).astype(BF16)


def _ctx_rows_kernel(kernel_fn, *refs, **kw):
    kernel_fn(*refs[:-2], refs[-1], **kw)


def _mla_attend(qm, km, vm, B, S, C, prev=None):
    T = qm.shape[0]
    npair = MLA_HEADS // 2
    with_x = prev is None
    if with_x:
        tq = 512
        nq = S // tq
        qmap = lambda b, p, i: (b * nq + i, p)
        in_specs = [pl.BlockSpec((tq, 2 * HEAD_SLOT), qmap),
                    pl.BlockSpec((S, 2 * HEAD_SLOT), lambda b, p, i: (b, p)),
                    pl.BlockSpec((C, 2 * HEAD_SLOT), lambda b, p, i: (B * S // C + b, p)),
                    pl.BlockSpec((S, LANES), lambda b, p, i: (b, p)),
                    pl.BlockSpec((C, LANES), lambda b, p, i: (B * S // C + b, p))]
        args = (qm, km, km, vm, vm)
        body = functools.partial(_mla_kernel, with_x=True)
        aliases = {}
    else:
        tq = C
        nq = 1
        qmap = lambda b, p, i: (B * S // C + b, p)
        in_specs = [pl.BlockSpec((tq, 2 * HEAD_SLOT), qmap),
                    pl.BlockSpec((C, 2 * HEAD_SLOT), qmap),
                    pl.BlockSpec((C, LANES), qmap),
                    pl.BlockSpec(memory_space=pl.ANY)]
        args = (qm, km, vm, prev)
        body = functools.partial(_ctx_rows_kernel, _mla_kernel, with_x=False)
        aliases = {3: 0}
    return pl.pallas_call(
        body,
        grid=(B, npair, nq),
        in_specs=in_specs,
        out_specs=pl.BlockSpec((tq, LANES), qmap),
        out_shape=jax.ShapeDtypeStruct((T, MLA_HEADS * MLA_V), BF16),
        input_output_aliases=aliases,
        compiler_params=_cparams(("arbitrary",) * 3),
        name="mla_x" if with_x else "mla_c",
    )(*args)


def _swa_kernel(*refs, with_x, nblk):
    if with_x:
        sink_ref, q_ref, kp_ref, ko_ref, kn_ref, kc_ref, vp_ref, vo_ref, vn_ref, vc_ref, o_ref = refs
    else:
        sink_ref, q_ref, kc_ref, vc_ref, o_ref = refs
    n = pl.program_id(1)
    lane = lax.broadcasted_iota(jnp.int32, (1, LANES), 1)
    lo = lane < SWA_HEAD_DIM
    row2 = lax.broadcasted_iota(jnp.int32, (2 * BLOCK, 1), 0)
    if with_x:
        kall = jnp.concatenate([kp_ref[...], ko_ref[...], kn_ref[...], kc_ref[...]], axis=0)
        vall = jnp.concatenate([vp_ref[...], vo_ref[...], vn_ref[...], vc_ref[...]], axis=0)
        nk = kall.shape[0]
        qi = lax.broadcasted_iota(jnp.int32, (2 * BLOCK, nk), 0) % BLOCK
        kj = lax.broadcasted_iota(jnp.int32, (2 * BLOCK, nk), 1)
        bad_prev = jnp.logical_and(kj < BLOCK, jnp.logical_or(kj - qi < BLOCK - WINDOW, n == 0))
        bad_next = jnp.logical_and(jnp.logical_and(kj >= 2 * BLOCK, kj < 3 * BLOCK),
                                   jnp.logical_or(kj - qi > BLOCK + WINDOW, n == nblk - 1))
        bad = jnp.logical_or(bad_prev, bad_next)
    else:
        kall, vall = kc_ref[...], vc_ref[...]
    for s in range(SWA_HEADS // 2):
        q = q_ref[:, s * LANES:(s + 1) * LANES]
        zero = jnp.zeros_like(q)
        q2 = jnp.concatenate([jnp.where(lo, q, zero), jnp.where(lo, zero, q)], axis=0)
        sink = jnp.where(row2 < BLOCK, sink_ref[0, s], sink_ref[0, SWA_HEADS // 2 + s])
        sc = _dot_nt(q2, kall)
        if with_x:
            sc = jnp.where(bad, -jnp.inf, sc)
        m = jnp.maximum(jnp.max(sc, axis=-1, keepdims=True), sink)
        p = jnp.exp(sc - m)
        l = jnp.sum(p, axis=-1, keepdims=True) + jnp.exp(sink - m)
        o = _dot(p.astype(BF16), vall) / l
        o_ref[:, s * LANES:(s + 1) * LANES] = jnp.where(lo, o[:BLOCK], o[BLOCK:]).astype(BF16)


def _swa_attend(sink, qs, ks, vs, B, S, C, prev=None):
    T = qs.shape[0]
    cb = B * S // C
    with_x = prev is None
    if with_x:
        nblk = S // BLOCK
        qmap = lambda b, n: (b * nblk + n, 0)
        pmap = lambda b, n: (b * nblk + jnp.maximum(n - 1, 0), 0)
        nmap = lambda b, n: (b * nblk + jnp.minimum(n + 1, nblk - 1), 0)
        cmap = lambda b, n: (cb + b, 0)
        kv = lambda mp: pl.BlockSpec((BLOCK, LANES), mp)
        cspec = pl.BlockSpec((C, LANES), cmap)
        in_specs = [pl.BlockSpec(memory_space=pltpu.SMEM),
                    pl.BlockSpec((BLOCK, SWA_HEADS * SWA_HEAD_DIM), qmap),
                    kv(pmap), kv(qmap), kv(nmap), cspec, kv(pmap), kv(qmap), kv(nmap), cspec]
        args = (sink, qs, ks, ks, ks, ks, vs, vs, vs, vs)
        body = functools.partial(_swa_kernel, with_x=True, nblk=nblk)
        aliases = {}
    else:
        nblk = C // BLOCK
        qmap = lambda b, n: (cb * (C // BLOCK) + b * nblk + n, 0)
        cmap = lambda b, n: (cb + b, 0)
        cspec = pl.BlockSpec((C, LANES), cmap)
        in_specs = [pl.BlockSpec(memory_space=pltpu.SMEM),
                    pl.BlockSpec((BLOCK, SWA_HEADS * SWA_HEAD_DIM), qmap), cspec, cspec,
                    pl.BlockSpec(memory_space=pl.ANY)]
        args = (sink, qs, ks, vs, prev)
        body = functools.partial(_ctx_rows_kernel, _swa_kernel, with_x=False, nblk=nblk)
        aliases = {4: 0}
    return pl.pallas_call(
        body,
        grid=(B, nblk),
        in_specs=in_specs,
        out_specs=pl.BlockSpec((BLOCK, SWA_HEADS * SWA_HEAD_DIM), qmap),
        out_shape=jax.ShapeDtypeStruct((T, SWA_HEADS * SWA_HEAD_DIM), BF16),
        input_output_aliases=aliases,
        compiler_params=_cparams(("arbitrary",) * 2),
        name="swa_x" if with_x else "swa_c",
    )(*args)


def _fourier_kernel(c_ref, s_ref, pp_ref, w_ref, o_ref):
    f = _dot(c_ref[...], pp_ref[:, 0:D_FNET]) - _dot(s_ref[...], pp_ref[:, D_FNET:2 * D_FNET])
    o_ref[...] = _dot(f.astype(BF16), w_ref[...]).astype(BF16)


def _fourier(cmat, smat, pp, wblk, B, N, row0, prev=None):
    T = pp.shape[0]
    tq = min(512, N)
    nr = N // tq
    b0 = row0 // N
    o0 = row0 // tq
    in_specs = [pl.BlockSpec((tq, N), lambda r, b: (r, 0)),
                pl.BlockSpec((tq, N), lambda r, b: (r, 0)),
                pl.BlockSpec((N, 2 * D_FNET), lambda r, b: (b0 + b, 0)),
                pl.BlockSpec((D_FNET, D_FNET), lambda r, b: (0, 0))]
    args = (cmat, smat, pp, wblk)
    if prev is None:
        body = _fourier_kernel
        aliases = {}
    else:
        in_specs.append(pl.BlockSpec(memory_space=pl.ANY))
        args = args + (prev,)
        body = functools.partial(_ctx_rows_kernel, _fourier_kernel)
        aliases = {4: 0}
    return pl.pallas_call(
        body,
        grid=(nr, B),
        in_specs=in_specs,
        out_specs=pl.BlockSpec((tq, D_FNET), lambda r, b: (o0 + b * nr + r, 0)),
        out_shape=jax.ShapeDtypeStruct((T, D_FNET), BF16),
        input_output_aliases=aliases,
        compiler_params=_cparams(("arbitrary",) * 2),
        name="fourier_%d" % N,
    )(*args)


def _route_rows(sel, aff):
    G, K = N_EXPERT_GROUPS, EXPERTS_PER_GROUP
    gscore = []
    for g in range(G):
        a = sel[g * K:(g + 1) * K]
        best = None
        for i in range(K):
            for j in range(i + 1, K):
                v = a[i] + a[j]
                best = v if best is None else jnp.maximum(best, v)
        gscore.append(best)
    gb = jnp.zeros_like(gscore[0])
    gbest = gscore[0]
    for g in range(1, G):
        upd = gscore[g] > gbest
        gb = jnp.where(upd, float(g), gb)
        gbest = jnp.where(upd, gscore[g], gbest)
    cs, ca = [], []
    for i in range(K):
        c, a = sel[i], aff[i]
        for g in range(1, G):
            pick = gb == float(g)
            c = jnp.where(pick, sel[g * K + i], c)
            a = jnp.where(pick, aff[g * K + i], a)
        cs.append(c)
        ca.append(a)

    def first_max(vals):
        bi = jnp.zeros_like(vals[0])
        bv = vals[0]
        for i in range(1, K):
            upd = vals[i] > bv
            bi = jnp.where(upd, float(i), bi)
            bv = jnp.where(upd, vals[i], bv)
        return bi

    i1 = first_max(cs)
    cs2 = [jnp.where(i1 == float(i), -jnp.inf, cs[i]) for i in range(K)]
    i2 = first_max(cs2)
    a1 = sum(jnp.where(i1 == float(i), ca[i], 0.0) for i in range(K))
    a2 = sum(jnp.where(i2 == float(i), ca[i], 0.0) for i in range(K))
    den = a1 + a2
    return gb * K + i1, gb * K + i2, a1 / den, a2 / den


def _post_kernel(x_ref, mod_ref, fo_ref, ml_ref, sw_ref, wof_ref, wom_ref, wos_ref, n2g_ref,
                 rwh_ref, rwl_ref, rb_ref, tri_ref, ones_ref, xn_ref, h2_ref, rc_ref, rr_ref, cnt_ref):
    m = mod_ref[...]
    mix = (_dot(fo_ref[...], wof_ref[...]) + _dot(ml_ref[...], wom_ref[...])
           + _dot(sw_ref[...], wos_ref[...]))
    xn = x_ref[...] + m[2:3] * mix
    xn_ref[...] = xn
    h2 = _rms(xn, D_MODEL) * n2g_ref[...] * (1.0 + m[4:5]) + m[3:4]
    h2_ref[...] = h2.astype(BF16)
    hh = h2.astype(BF16)
    hl = (h2 - hh.astype(F32)).astype(BF16)
    logits = _dot(hh, rwh_ref[...]) + (_dot(hl, rwh_ref[...]) + _dot(hh, rwl_ref[...]))
    lt = logits.T[0:N_EXPERTS, :]
    aff_t = jax.nn.sigmoid(lt)
    sel_t = aff_t + rb_ref[...]
    sel = [sel_t[e:e + 1, :] for e in range(N_EXPERTS)]
    aff = [aff_t[e:e + 1, :] for e in range(N_EXPERTS)]
    e1, e2, w1, w2 = _route_rows(sel, aff)

    eio = lax.broadcasted_iota(jnp.int32, (N_EXPERTS, TM), 0).astype(F32)
    oh = jnp.concatenate([jnp.where(eio == e1, 1.0, 0.0), jnp.where(eio == e2, 1.0, 0.0)], axis=1)
    ohb = oh.astype(BF16)
    rank = _dot(ohb, tri_ref[...])
    cnt = _dot(ohb, ones_ref[...]).astype(jnp.int32)
    cnt8 = jnp.left_shift(jnp.right_shift(cnt + (CHUNK - 1), CHUNK_LOG2), CHUNK_LOG2)
    cnt_ref[...] = cnt8
    cnt8f = cnt8.astype(F32)
    off = jnp.zeros((1, 1), F32)
    slot = jnp.zeros((1, 2 * TM), F32)
    for e in range(N_EXPERTS):
        slot = slot + oh[e:e + 1, :] * (off + rank[e:e + 1, :])
        off = off + cnt8f[e:e + 1, 0:1]
    s0, s1 = slot[:, :TM], slot[:, TM:]

    sub = lax.broadcasted_iota(jnp.int32, (8, TM), 0)
    rr_ref[...] = jnp.where(sub == 0, s0, jnp.where(sub == 1, s1, 0.0))
    blk = jnp.where(sub == 0, e1, jnp.where(sub == 1, e2, jnp.where(sub == 2, w1, jnp.where(
        sub == 3, w2, jnp.where(sub == 4, s0, jnp.where(sub == 5, s1, 0.0))))))
    rows = jnp.concatenate([blk, jnp.zeros((LANES - 8, TM), F32)], axis=0)
    rc_ref[...] = rows.T


def _post(xc, mod, fo, ml, sw, wl, n2g, rw_hi, rw_lo, rb_col, nt, nx_tiles, tiles_per_batch):
    rows = nt * TM

    def bidx(i):
        return jnp.where(i < nx_tiles, i // tiles_per_batch, mod.shape[0] - 1)

    def full(a):
        return pl.BlockSpec(a.shape, lambda i: (0,) * a.ndim)

    row = lambda w: pl.BlockSpec((TM, w), lambda i: (i, 0))
    pair = np.arange(2 * TM)
    tri = jnp.asarray(pair[:, None] < pair[None, :], BF16)
    ones = jnp.ones((2 * TM, LANES), BF16)
    consts = [wl["wo_f"], wl["wo_m"], wl["wo_s"], n2g, rw_hi, rw_lo, rb_col, tri, ones]
    return pl.pallas_call(
        _post_kernel,
        grid=(nt,),
        in_specs=[row(D_MODEL), pl.BlockSpec((None, 8, D_MODEL), lambda i: (bidx(i), 0, 0)),
                  row(D_FNET), row(MLA_HEADS * MLA_V), row(SWA_HEADS * SWA_HEAD_DIM)]
                 + [full(a) for a in consts],
        out_specs=[row(D_MODEL), row(D_MODEL), row(LANES),
                   pl.BlockSpec((None, 8, TM), lambda i: (i, 0, 0)),
                   pl.BlockSpec((None, N_EXPERTS, LANES), lambda i: (i, 0, 0))],
        out_shape=[jax.ShapeDtypeStruct((rows, D_MODEL), F32),
                   jax.ShapeDtypeStruct((rows, D_MODEL), BF16),
                   jax.ShapeDtypeStruct((rows, LANES), F32),
                   jax.ShapeDtypeStruct((nt, 8, TM), F32),
                   jax.ShapeDtypeStruct((nt, N_EXPERTS, LANES), jnp.int32)],
        compiler_params=_cparams(("arbitrary",)),
        name="post",
    )(xc, mod, fo, ml, sw, *consts)


def _moe_tables(cnt8, te):
    nt = cnt8.shape[0]
    tile_prefix = jnp.cumsum(cnt8, axis=0) - cnt8
    tot = jnp.sum(cnt8, axis=0)
    tot_e = ((tot + te - 1) // te) * te
    goff = jnp.cumsum(tot_e) - tot_e
    dbase = goff[None, :] + tile_prefix
    nch = cnt8 // CHUNK
    cum = jnp.cumsum(nch, axis=1)
    k = jnp.arange(MAX_CHUNKS, dtype=jnp.int32)[None, :, None]
    owns = jnp.logical_and(k >= (cum - nch)[:, None, :], k < cum[:, None, :])
    dst = jnp.sum(jnp.where(owns, dbase[:, None, :] + CHUNK * (k - (cum - nch)[:, None, :]), 0), axis=-1)
    nchunks = cum[:, -1]
    padch = (tot_e - tot) // CHUNK
    cump = jnp.cumsum(padch)
    kp = jnp.arange(N_EXPERTS * (te // CHUNK), dtype=jnp.int32)[:, None]
    pown = jnp.logical_and(kp >= (cump - padch)[None, :], kp < cump[None, :])
    pdst = jnp.sum(jnp.where(pown, (goff + tot)[None, :] + CHUNK * (kp - (cump - padch)[None, :]), 0), axis=-1)
    npad = cump[-1]
    ntile_cum = jnp.cumsum(tot_e // te)
    nact = ntile_cum[-1]
    return dict(dst=dst.reshape(-1).astype(jnp.int32), nchunks=nchunks.astype(jnp.int32),
                pdst=pdst.astype(jnp.int32), npad=npad.reshape(1).astype(jnp.int32),
                ntile_cum=ntile_cum.astype(jnp.int32), nact=nact.reshape(1).astype(jnp.int32))


def _dispatch_kernel(dst_ref, nch_ref, pdst_ref, npad_ref, h_ref, rr_ref, xs_ref, sbuf, zbuf, sem, zsem):
    i = pl.program_id(0)
    nt = pl.num_programs(0)
    slot = i % 2

    def chunk_copy(sl, k, d):
        return pltpu.make_async_copy(
            sbuf.at[sl, pl.ds(pl.multiple_of(k * CHUNK, CHUNK), CHUNK), :],
            xs_ref.at[pl.ds(pl.multiple_of(d, CHUNK), CHUNK), :], sem.at[sl])

    def wait_tile(t, sl):
        def body(k, c):
            chunk_copy(sl, 0, 0).wait()
            return c
        lax.fori_loop(0, nch_ref[t], body, 0)

    def pad_copy(d):
        return pltpu.make_async_copy(zbuf, xs_ref.at[pl.ds(pl.multiple_of(d, CHUNK), CHUNK), :], zsem)

    @pl.when(i == 0)
    def _():
        zbuf[...] = jnp.zeros_like(zbuf)

        def start(k, c):
            pad_copy(pdst_ref[k]).start()
            return c
        lax.fori_loop(0, npad_ref[0], start, 0)

        def wait(k, c):
            pad_copy(0).wait()
            return c
        lax.fori_loop(0, npad_ref[0], wait, 0)

    @pl.when(i >= 2)
    def _():
        wait_tile(i - 2, slot)

    rr = rr_ref[...]
    sio = lax.broadcasted_iota(jnp.int32, (NSLOT, TM), 0).astype(F32)
    psel = jnp.where(jnp.logical_or(sio == rr[0:1, :], sio == rr[1:2, :]), 1.0, 0.0).astype(BF16)
    sbuf[slot] = _dot(psel, h_ref[...])

    def issue(k, c):
        chunk_copy(slot, k, dst_ref[i * MAX_CHUNKS + k]).start()
        return c
    lax.fori_loop(0, nch_ref[i], issue, 0)

    @pl.when(i == nt - 1)
    def _():
        wait_tile(i, slot)

        @pl.when(i >= 1)
        def _():
            wait_tile(i - 1, 1 - slot)


def _dispatch(h2, rr, tb, rows_sorted):
    nt = rr.shape[0]
    return pl.pallas_call(
        _dispatch_kernel,
        grid_spec=pltpu.PrefetchScalarGridSpec(
            num_scalar_prefetch=4,
            grid=(nt,),
            in_specs=[pl.BlockSpec((TM, D_MODEL), lambda i, *_: (i, 0)),
                      pl.BlockSpec((None, 8, TM), lambda i, *_: (i, 0, 0))],
            out_specs=pl.BlockSpec(memory_space=pl.ANY),
            scratch_shapes=[pltpu.VMEM((2, NSLOT, D_MODEL), F32), pltpu.VMEM((CHUNK, D_MODEL), F32),
                            pltpu.SemaphoreType.DMA((2,)), pltpu.SemaphoreType.DMA(())]),
        out_shape=jax.ShapeDtypeStruct((rows_sorted, D_MODEL), F32),
        compiler_params=_cparams(("arbitrary",)),
        name="moe_dispatch",
    )(tb["dst"], tb["nchunks"], tb["pdst"], tb["npad"], h2, rr)


def _expert_kernel(te_ref, na_ref, x_ref, wg_ref, wu_ref, wd_ref, o_ref, wgb, wub, wdb):
    j = pl.program_id(0)
    active = j < na_ref[0]
    fresh = jnp.logical_or(j == 0, te_ref[j] != te_ref[jnp.maximum(j - 1, 0)])

    @pl.when(jnp.logical_and(active, fresh))
    def _():
        wgb[...] = wg_ref[...].astype(BF16)
        wub[...] = wu_ref[...].astype(BF16)
        wdb[...] = wd_ref[...].astype(BF16)

    @pl.when(active)
    def _():
        x = x_ref[...].astype(BF16)
        a = _dot(x, wgb[...])
        a = a * jax.nn.sigmoid(a) * _dot(x, wub[...])
        o_ref[...] = _dot(a.astype(BF16), wdb[...])


def _experts(xs, wg, wu, wd, layer, tb, te):
    nte = xs.shape[0] // te
    jj = jnp.minimum(jnp.arange(nte, dtype=jnp.int32), tb["nact"][0] - 1)
    tile_e = jnp.sum(jj[:, None] >= tb["ntile_cum"][None, :], axis=-1).astype(jnp.int32)

    def tmap(j, te_ref, na):
        return (jnp.minimum(j, na[0] - 1), 0)

    def wmap(j, te_ref, na):
        return (layer, te_ref[j], 0, 0)

    return pl.pallas_call(
        _expert_kernel,
        grid_spec=pltpu.PrefetchScalarGridSpec(
            num_scalar_prefetch=2,
            grid=(nte,),
            in_specs=[pl.BlockSpec((te, D_MODEL), tmap),
                      pl.BlockSpec((None, None, D_MODEL, D_EXPERT), wmap),
                      pl.BlockSpec((None, None, D_MODEL, D_EXPERT), wmap),
                      pl.BlockSpec((None, None, D_EXPERT, D_MODEL), wmap)],
            out_specs=pl.BlockSpec((te, D_MODEL), tmap),
            scratch_shapes=[pltpu.VMEM((D_MODEL, D_EXPERT), BF16), pltpu.VMEM((D_MODEL, D_EXPERT), BF16),
                            pltpu.VMEM((D_EXPERT, D_MODEL), BF16)]),
        out_shape=jax.ShapeDtypeStruct(xs.shape, F32),
        compiler_params=_cparams(("arbitrary",)),
        name="moe_experts",
    )(tile_e, tb["nact"], xs, wg, wu, wd)


def _combine_kernel(dst_ref, nch_ref, xn_ref, mod_ref, rc_ref, ys_ref, o_ref, gbuf, sem):
    i = pl.program_id(0)
    nt = pl.num_programs(0)
    slot = i % 2

    def chunk_copy(sl, k, d):
        return pltpu.make_async_copy(
            ys_ref.at[pl.ds(pl.multiple_of(d, CHUNK), CHUNK), :],
            gbuf.at[sl, pl.ds(pl.multiple_of(k * CHUNK, CHUNK), CHUNK), :], sem.at[sl])

    def issue_tile(t, sl):
        def body(k, c):
            chunk_copy(sl, k, dst_ref[t * MAX_CHUNKS + k]).start()
            return c
        lax.fori_loop(0, nch_ref[t], body, 0)

    @pl.when(i == 0)
    def _():
        gbuf[...] = jnp.zeros_like(gbuf)
        issue_tile(0, 0)

    @pl.when(i + 1 < nt)
    def _():
        issue_tile(i + 1, 1 - slot)

    def wait(k, c):
        chunk_copy(slot, 0, 0).wait()
        return c
    lax.fori_loop(0, nch_ref[i], wait, 0)

    g = gbuf[slot].astype(BF16)
    rc = rc_ref[...]
    lio = lax.broadcasted_iota(jnp.int32, (TM, NSLOT), 1).astype(F32)
    p0 = jnp.where(lio == rc[:, 4:5], 1.0, 0.0).astype(BF16)
    p1 = jnp.where(lio == rc[:, 5:6], 1.0, 0.0).astype(BF16)
    y = rc[:, 2:3] * _dot(p0, g) + rc[:, 3:4] * _dot(p1, g)
    o_ref[...] = xn_ref[...] + mod_ref[5:6, :] * y


def _combine(xn, mod, rc, ys, tb, nx_tiles, tiles_per_batch):
    rows = xn.shape[0]
    nt = rows // TM

    def bidx(i, *_):
        return (jnp.where(i < nx_tiles, i // tiles_per_batch, mod.shape[0] - 1), 0, 0)

    return pl.pallas_call(
        _combine_kernel,
        grid_spec=pltpu.PrefetchScalarGridSpec(
            num_scalar_prefetch=2,
            grid=(nt,),
            in_specs=[pl.BlockSpec((TM, D_MODEL), lambda i, *_: (i, 0)),
                      pl.BlockSpec((None, 8, D_MODEL), bidx),
                      pl.BlockSpec((TM, LANES), lambda i, *_: (i, 0)),
                      pl.BlockSpec(memory_space=pl.ANY)],
            out_specs=pl.BlockSpec((TM, D_MODEL), lambda i, *_: (i, 0)),
            scratch_shapes=[pltpu.VMEM((2, NSLOT, D_MODEL), F32), pltpu.SemaphoreType.DMA((2,))]),
        out_shape=jax.ShapeDtypeStruct((rows, D_MODEL), F32),
        compiler_params=_cparams(("arbitrary",)),
        name="moe_combine",
    )(tb["dst"], tb["nchunks"], xn, mod, rc, ys)


def _moe(h2, rc, rr, cnt, wg, wu, wd, layer, xn, mod, nx_tiles, tiles_per_batch):
    nt = rr.shape[0]
    te = TE
    max_rows = 2 * nt * TM + (CHUNK - 1) * N_EXPERTS * nt + N_EXPERTS * (te - CHUNK)
    rows_sorted = ((max_rows + te - 1) // te) * te
    tb = _moe_tables(cnt[:, :, 0], te)
    xs = _dispatch(h2, rr, tb, rows_sorted)
    ys = _experts(xs, wg, wu, wd, layer, tb, te)
    return _combine(xn, mod, rc, ys, tb, nx_tiles, tiles_per_batch)


def _rope_tables(S, C):
    t = jnp.arange(S)
    rows, cols = (t // GRID_W).astype(F32), (t % GRID_W).astype(F32)

    def axis_tabs(d_rot, lane0, width):
        d_axis = d_rot // 2
        inv = ROPE_THETA ** (-jnp.arange(0, d_axis, 2, dtype=F32) / d_axis)
        ar, ac = rows[:, None] * inv, cols[:, None] * inv
        ang = jnp.concatenate([ar, ar, ac, ac], axis=-1)
        q = d_rot // 4
        first = np.concatenate([np.ones(q), np.zeros(q), np.ones(q), np.zeros(q)]).astype(np.float32)
        pad = ((0, C), (lane0, width - lane0 - d_rot))
        cos = jnp.pad(jnp.cos(ang) - 1.0, pad) + 1.0
        sa = jnp.pad(-jnp.sin(ang) * first, pad)
        sb = jnp.pad(jnp.sin(ang) * (1.0 - first), pad)
        return cos, sa, sb

    cm, sam, sbm = axis_tabs(MLA_ROPE, MLA_NOPE, LANES)
    cs, sas, sbs = axis_tabs(SWA_HEAD_DIM, 0, SWA_HEAD_DIM)
    tile2 = lambda a: jnp.concatenate([a, a], axis=1)
    return dict(cm=cm, sam=sam, sbm=sbm, cs=tile2(cs), sas=tile2(sas), sbs=tile2(sbs))


_TWO_PI_HI = float(np.float32(2.0 * np.pi))
_TWO_PI_LO = float(np.float32(2.0 * np.pi - np.float64(np.float32(2.0 * np.pi))))


def _dft_mats(N):
    k = jnp.arange(N, dtype=jnp.int32)
    frac = ((k[:, None] * k[None, :]) % N).astype(F32) / N
    ang = _TWO_PI_HI * frac + _TWO_PI_LO * frac
    return jnp.cos(ang), jnp.sin(ang)


def _dft64_blocks():
    c, s = _dft_mats(FNET_CH)
    eye = jnp.eye(FNET_GROUPS, dtype=F32)
    return jnp.concatenate([jnp.kron(eye, c), jnp.kron(eye, s)], axis=1)


def _layer_weights(l, w_in, fnet_w, mla_cq_g, mla_ckv_g, mla_w_uq, mla_w_uk, mla_w_uv, mla_q_g, mla_k_g,
                   swa_q_g, swa_k_g, swa_sink, w_out):
    D = D_MODEL
    wi = w_in[l]
    o_kr = D_FNET + MLA_Q_RANK + MLA_KV_RANK
    o_qs = o_kr + MLA_ROPE
    o_ks = o_qs + SWA_HEADS * SWA_HEAD_DIM
    o_vs = o_ks + SWA_KV_HEADS * SWA_HEAD_DIM
    order = np.array(SWA_HEAD_ORDER)
    w_qs = wi[:, o_qs:o_ks].reshape(D, SWA_HEADS, SWA_HEAD_DIM)[:, order].reshape(D, -1)
    z = lambda n: jnp.zeros((D, n), F32)
    win = jnp.concatenate([wi[:, :o_kr], w_qs, wi[:, o_ks:o_vs], wi[:, o_vs:],
                           z(MLA_NOPE), wi[:, o_kr:o_qs], z(LANES - MLA_QK)], axis=1)
    pad_slot = lambda w, d: jnp.pad(w.reshape(w.shape[0], MLA_HEADS, d),
                                    ((0, 0), (0, 0), (0, HEAD_SLOT - d))).reshape(w.shape[0], -1)
    wo = w_out[l]
    o_m = D_FNET
    o_s = D_FNET + MLA_HEADS * MLA_V
    wo_s = wo[o_s:].reshape(SWA_HEADS, SWA_HEAD_DIM, D)[order].reshape(-1, D)
    fw = fnet_w[l]
    wblk = jnp.zeros((D_FNET, D_FNET), F32)
    for g in range(FNET_GROUPS):
        wblk = wblk.at[g * FNET_CH:(g + 1) * FNET_CH, g * FNET_CH:(g + 1) * FNET_CH].set(fw[g])
    pad_g = lambda g: jnp.pad(g, (0, HEAD_SLOT - MLA_QK)).reshape(1, HEAD_SLOT)
    return dict(
        w_in=win.astype(BF16),
        cq_g=mla_cq_g[l].reshape(1, -1), ckv_g=mla_ckv_g[l].reshape(1, -1),
        w_uq=pad_slot(mla_w_uq[l], MLA_QK).astype(BF16),
        w_uk=pad_slot(mla_w_uk[l], MLA_NOPE).astype(BF16),
        w_uv=mla_w_uv[l].astype(BF16),
        mq_g=pad_g(mla_q_g[l]), mk_g=pad_g(mla_k_g[l]),
        sq_g=jnp.tile(swa_q_g[l], 2).reshape(1, LANES), sk_g=jnp.tile(swa_k_g[l], 2).reshape(1, LANES),
        sink=swa_sink[l].reshape(1, SWA_HEADS),
        wo_f=wo[:o_m].astype(BF16), wo_m=wo[o_m:o_s].astype(BF16), wo_s=wo_s.astype(BF16),
        fnet=wblk,
    )


def kernel(x, c, ctx, c_ctx, ada_w, ada_b, norm1_g, norm2_g, w_in, fnet_w, mla_cq_g, mla_ckv_g, mla_w_uq,
           mla_w_uk, mla_w_uv, mla_q_g, mla_k_g, swa_q_g, swa_k_g, swa_sink, w_out, router_w, router_b,
           exp_w_gate, exp_w_up, exp_w_down):
    B, S, D = x.shape
    C = ctx.shape[1]
    L = ada_w.shape[0]
    assert D == D_MODEL and S % 512 == 0 and C == TM and S % TM == 0
    nx_tiles = B * S // TM
    nt_all = nx_tiles + B * C // TM
    tiles_per_batch = S // TM

    tabs = _rope_tables(S, C)
    tabs["dft64"] = _dft64_blocks().astype(BF16)
    dft = {n: tuple(m.astype(BF16) for m in _dft_mats(n)) for n in (S, C)}
    fscale = {n: lax.rsqrt(jnp.full((), n * FNET_CH, F32)) for n in (S, C)}

    nmod = 16
    cvec = jnp.concatenate([c, c_ctx[None, :], jnp.zeros((nmod - B - 1, D), F32)], axis=0)
    mod_all = _adaln(cvec, ada_w, ada_b)
    mod_all = mod_all[:, :B + 1].reshape(L, B + 1, 6, D)
    mod_all = jnp.pad(mod_all, ((0, 0), (0, 0), (0, 2), (0, 0)))

    rw = jnp.pad(router_w, ((0, 0), (0, LANES - N_EXPERTS)))
    rw_hi = rw.astype(BF16)
    rw_lo = (rw - rw_hi.astype(F32)).astype(BF16)
    rb_col = router_b.reshape(N_EXPERTS, 1)

    xc = jnp.concatenate([x.reshape(B * S, D), ctx.reshape(B * C, D)], axis=0)
    for l in range(L):
        last = l == L - 1
        wl = _layer_weights(l, w_in, fnet_w, mla_cq_g, mla_ckv_g, mla_w_uq, mla_w_uk, mla_w_uv, mla_q_g,
                            mla_k_g, swa_q_g, swa_k_g, swa_sink, w_out)
        mod = mod_all[l]
        pr = _prep(xc, mod, norm1_g[l].reshape(1, D), wl, tabs, nx_tiles, tiles_per_batch)
        fo = _fourier(dft[S][0], dft[S][1], pr["pp"], (wl["fnet"] * fscale[S]).astype(BF16), B, S, 0)
        ml = _mla_attend(pr["qm"], pr["km"], pr["vm"], B, S, C)
        sw = _swa_attend(wl["sink"], pr["qs"], pr["ks"], pr["vs"], B, S, C)
        if not last:
            fo = _fourier(dft[C][0], dft[C][1], pr["pp"], (wl["fnet"] * fscale[C]).astype(BF16), B, C, B * S,
                          prev=fo)
            ml = _mla_attend(pr["qm"], pr["km"], pr["vm"], B, S, C, prev=ml)
            sw = _swa_attend(wl["sink"], pr["qs"], pr["ks"], pr["vs"], B, S, C, prev=sw)
        nt = nx_tiles if last else nt_all
        xn, h2, rc, rr, cnt = _post(xc, mod, fo, ml, sw, wl, norm2_g[l].reshape(1, D), rw_hi, rw_lo, rb_col,
                                    nt, nx_tiles, tiles_per_batch)
        xc = _moe(h2, rc, rr, cnt, exp_w_gate, exp_w_up, exp_w_down, l, xn, mod, nx_tiles, tiles_per_batch)
    return xc[:B * S].reshape(B, S, D)
```

```python
import functools

import numpy as np
import jax
import jax.numpy as jnp
from jax import lax
from jax.experimental import pallas as pl
from jax.experimental.pallas import tpu as pltpu

F32 = jnp.float32
BF16 = jnp.bfloat16

D_MODEL = 1024
GRID_W = 64
FNET_GROUPS = 4
FNET_CH = 64
D_FNET = FNET_GROUPS * FNET_CH
MLA_HEADS = 6
MLA_Q_RANK = 256
MLA_KV_RANK = 128
MLA_NOPE = 64
MLA_ROPE = 32
MLA_QK = MLA_NOPE + MLA_ROPE
MLA_V = 64
SWA_HEADS = 6
SWA_KV_HEADS = 2
SWA_HEAD_DIM = 64
WINDOW = 128
BLOCK = 128
N_EXPERTS = 16
N_EXPERT_GROUPS = 4
EXPERTS_PER_GROUP = 4
D_EXPERT = 512
ROPE_THETA = 10000.0
EPS = 1e-6

LANES = 128
TM = 256
HEAD_SLOT = LANES
CHUNK = 8
CHUNK_LOG2 = 3
NSLOT = 640
MAX_CHUNKS = NSLOT // CHUNK
TE = 512
PREP_CHAINS = 2
D_PACK = D_MODEL // 2
VMEM_LIMIT = 48 * 1024 * 1024

P_UF = 0
P_CQ = 256
P_CKV = 512
P_QS = 640
P_KS = 1024
P_VS = 1152
P_KR = 1280
P_TOT = 1408
SWA_HEAD_ORDER = (0, 3, 1, 4, 2, 5)


def _dot(a, b):
    return jnp.dot(a, b, preferred_element_type=F32)


def _dot_nt(a, b):
    return lax.dot_general(a, b, (((1,), (1,)), ((), ())), preferred_element_type=F32)


def _rms(x, n):
    return x * lax.rsqrt(jnp.sum(x * x, axis=-1, keepdims=True) / n + EPS)


def _rope(x, c, sa, sb, half):
    n = x.shape[-1]
    return x * c + pltpu.roll(x, n - half, 1) * sa + pltpu.roll(x, half, 1) * sb


_HI16 = 0xFFFF0000


def _pack_bf16_pairs(x):
    bits = lax.bitcast_convert_type(x, jnp.uint32)
    return jnp.bitwise_or(jnp.right_shift(bits[:, :D_PACK], jnp.uint32(16)),
                          jnp.bitwise_and(bits[:, D_PACK:], jnp.uint32(_HI16)))


def _unpack_bf16_pairs(u):
    lo = lax.bitcast_convert_type(jnp.left_shift(u, jnp.uint32(16)), F32)
    hi = lax.bitcast_convert_type(jnp.bitwise_and(u, jnp.uint32(_HI16)), F32)
    return jnp.concatenate([lo, hi], axis=1).astype(BF16)


def _cparams(sem):
    return pltpu.CompilerParams(dimension_semantics=sem, vmem_limit_bytes=VMEM_LIMIT)


def _adaln_kernel(c_ref, w_ref, b_ref, o_ref):
    c = c_ref[...]
    sc = c * jax.nn.sigmoid(c)
    o_ref[...] = jnp.dot(sc, w_ref[...], preferred_element_type=F32,
                         precision=lax.Precision.HIGHEST) + b_ref[...]


def _adaln(cvec, ada_w, ada_b):
    L, D, N6 = ada_w.shape
    R = cvec.shape[0]
    bn = 512
    return pl.pallas_call(
        _adaln_kernel,
        grid=(L, N6 // bn),
        in_specs=[pl.BlockSpec((R, D), lambda l, j: (0, 0)),
                  pl.BlockSpec((None, D, bn), lambda l, j: (l, 0, j)),
                  pl.BlockSpec((None, 1, bn), lambda l, j: (l, 0, j))],
        out_specs=pl.BlockSpec((None, R, bn), lambda l, j: (l, 0, j)),
        out_shape=jax.ShapeDtypeStruct((L, R, N6), F32),
        compiler_params=_cparams(("arbitrary", "arbitrary")),
        name="adaln",
    )(cvec, ada_w, ada_b.reshape(L, 1, N6))


def _prep_kernel(x_ref, mod_ref, n1g_ref, win_ref, cqg_ref, ckvg_ref, wuq_ref, wuk_ref, wuv_ref,
                 mqg_ref, mkg_ref, sqg_ref, skg_ref, cm_ref, sam_ref, sbm_ref, cs_ref, sas_ref, sbs_ref,
                 dft_ref, pp_ref, qm_ref, km_ref, vm_ref, qs_ref, ks_ref, vs_ref):
    m = mod_ref[...]
    lo = lax.broadcasted_iota(jnp.int32, (1, LANES), 1) < SWA_HEAD_DIM
    slot_lane = jnp.bitwise_and(lax.broadcasted_iota(jnp.int32, (1, MLA_HEADS * HEAD_SLOT), 1), HEAD_SLOT - 1)
    vone = jnp.where(slot_lane == MLA_V, 1.0, 0.0)

    def head_norm(slab, g):
        sq = slab * slab
        s_lo = jnp.sum(jnp.where(lo, sq, 0.0), axis=-1, keepdims=True)
        s_hi = jnp.sum(jnp.where(lo, 0.0, sq), axis=-1, keepdims=True)
        r = jnp.where(lo, lax.rsqrt(s_lo * (1.0 / SWA_HEAD_DIM) + EPS),
                      lax.rsqrt(s_hi * (1.0 / SWA_HEAD_DIM) + EPS))
        return slab * r * g

    rows = TM // PREP_CHAINS
    for ch in range(PREP_CHAINS):
        rs = slice(ch * rows, (ch + 1) * rows)
        x = x_ref[rs, :]
        h = _rms(x, D_MODEL) * n1g_ref[...] * (1.0 + m[1:2]) + m[0:1]
        p = _dot(h.astype(BF16), win_ref[...])

        u = p[:, P_UF:P_UF + D_FNET].astype(BF16)
        pp_ref[rs, :] = _dot(u, dft_ref[...]).astype(BF16)

        cm, sam, sbm = cm_ref[rs, :], sam_ref[rs, :], sbm_ref[rs, :]
        cs, sas, sbs = cs_ref[rs, :], sas_ref[rs, :], sbs_ref[rs, :]

        cq = _rms(p[:, P_CQ:P_CQ + MLA_Q_RANK], MLA_Q_RANK) * cqg_ref[...]
        qraw = _dot(cq.astype(BF16), wuq_ref[...])
        mqg = mqg_ref[...]
        for hh in range(MLA_HEADS):
            sl = slice(hh * HEAD_SLOT, (hh + 1) * HEAD_SLOT)
            qn = _rms(qraw[:, sl], MLA_QK) * mqg
            qm_ref[rs, sl] = (_rope(qn, cm, sam, sbm, MLA_ROPE // 4) * (MLA_QK ** -0.5)).astype(BF16)

        ckv = (_rms(p[:, P_CKV:P_CKV + MLA_KV_RANK], MLA_KV_RANK) * ckvg_ref[...]).astype(BF16)
        knope = _dot(ckv, wuk_ref[...])
        vm_ref[rs, :] = (_dot(ckv, wuv_ref[...]) + vone).astype(BF16)
        kr = p[:, P_KR:P_KR + LANES]
        mkg = mkg_ref[...]
        kr_ss = jnp.sum(kr * kr, axis=-1, keepdims=True)
        kr_rot = _rope(kr * mkg, cm, sam, sbm, MLA_ROPE // 4)
        for hh in range(MLA_HEADS):
            sl = slice(hh * HEAD_SLOT, (hh + 1) * HEAD_SLOT)
            kn_h = knope[:, sl]
            r = lax.rsqrt((jnp.sum(kn_h * kn_h, axis=-1, keepdims=True) + kr_ss) / MLA_QK + EPS)
            km_ref[rs, sl] = ((kn_h * mkg + kr_rot) * r).astype(BF16)

        sqg = sqg_ref[...]
        for s in range(SWA_HEADS // 2):
            sl = slice(P_QS + s * LANES, P_QS + (s + 1) * LANES)
            qn = head_norm(p[:, sl], sqg)
            qs_ref[rs, s * LANES:(s + 1) * LANES] = (
                _rope(qn, cs, sas, sbs, SWA_HEAD_DIM // 4) * (SWA_HEAD_DIM ** -0.5)).astype(BF16)
        kn = head_norm(p[:, P_KS:P_KS + LANES], skg_ref[...])
        ks_ref[rs, :] = _rope(kn, cs, sas, sbs, SWA_HEAD_DIM // 4).astype(BF16)
        vs_ref[rs, :] = p[:, P_VS:P_VS + LANES].astype(BF16)


def _prep(xc, mod, n1g, wl, tabs, nx_tiles, tiles_per_batch):
    T = xc.shape[0]
    nt = T // TM

    def bidx(i):
        return jnp.where(i < nx_tiles, i // tiles_per_batch, mod.shape[0] - 1)

    def ridx(i):
        return jnp.where(i < nx_tiles, i % tiles_per_batch, tiles_per_batch)

    def full(a):
        return pl.BlockSpec(a.shape, lambda i: (0,) * a.ndim)

    tab_spec = pl.BlockSpec((TM, LANES), lambda i: (ridx(i), 0))
    row = lambda w: pl.BlockSpec((TM, w), lambda i: (i, 0))
    consts = [n1g, wl["w_in"], wl["cq_g"], wl["ckv_g"], wl["w_uq"], wl["w_uk"], wl["w_uv"],
              wl["mq_g"], wl["mk_g"], wl["sq_g"], wl["sk_g"]]
    outs = [("pp", 2 * D_FNET), ("qm", MLA_HEADS * HEAD_SLOT), ("km", MLA_HEADS * HEAD_SLOT),
            ("vm", MLA_HEADS * HEAD_SLOT), ("qs", SWA_HEADS * SWA_HEAD_DIM), ("ks", LANES), ("vs", LANES)]
    res = pl.pallas_call(
        _prep_kernel,
        grid=(nt,),
        in_specs=[row(D_MODEL), pl.BlockSpec((None, 8, D_MODEL), lambda i: (bidx(i), 0, 0))]
                 + [full(a) for a in consts] + [tab_spec] * 6 + [full(tabs["dft64"])],
        out_specs=[row(w) for _, w in outs],
        out_shape=[jax.ShapeDtypeStruct((T, w), BF16) for _, w in outs],
        compiler_params=_cparams(("arbitrary",)),
        name="prep",
    )(xc, mod, *consts, tabs["cm"], tabs["sam"], tabs["sbm"], tabs["cs"], tabs["sas"], tabs["sbs"],
      tabs["dft64"])
    return dict(zip([n for n, _ in outs], res))


def _mla_kernel(*refs, with_x):
    if with_x:
        q_ref, kx_ref, kc_ref, vx_ref, vc_ref, o_ref = refs
    else:
        q_ref, kc_ref, vc_ref, o_ref = refs
    outs = []
    for hh in range(2):
        sl = slice(hh * HEAD_SLOT, (hh + 1) * HEAD_SLOT)
        q = q_ref[:, sl]
        sc = _dot_nt(q, kc_ref[:, sl])
        m = jnp.max(sc, axis=-1, keepdims=True)
        if with_x:
            sx = _dot_nt(q, kx_ref[:, sl])
            m = jnp.maximum(m, jnp.max(sx, axis=-1, keepdims=True))
            px = jnp.exp(sx - m)
        pc = jnp.exp(sc - m)
        o = _dot(pc.astype(BF16), vc_ref[:, sl])
        if with_x:
            o = o + _dot(px.astype(BF16), vx_ref[:, sl])
        outs.append(o / o[:, MLA_V:MLA_V + 1])
    lane = lax.broadcasted_iota(jnp.int32, (1, LANES), 1)
    o_ref[...] = jnp.where(lane < MLA_V, outs[0], pltpu.roll(outs[1], MLA_V, 1)).astype(BF16)


def _ctx_rows_kernel(kernel_fn, *refs, **kw):
    kernel_fn(*refs[:-2], refs[-1], **kw)


def _mla_attend(qm, km, vm, B, S, C, prev=None):
    T = qm.shape[0]
    npair = MLA_HEADS // 2
    with_x = prev is None
    if with_x:
        tq = 512
        nq = S // tq
        qmap = lambda b, p, i: (b * nq + i, p)
        in_specs = [pl.BlockSpec((tq, 2 * HEAD_SLOT), qmap),
                    pl.BlockSpec((S, 2 * HEAD_SLOT), lambda b, p, i: (b, p)),
                    pl.BlockSpec((C, 2 * HEAD_SLOT), lambda b, p, i: (B * S // C + b, p)),
                    pl.BlockSpec((S, 2 * HEAD_SLOT), lambda b, p, i: (b, p)),
                    pl.BlockSpec((C, 2 * HEAD_SLOT), lambda b, p, i: (B * S // C + b, p))]
        args = (qm, km, km, vm, vm)
        body = functools.partial(_mla_kernel, with_x=True)
        aliases = {}
    else:
        tq = C
        nq = 1
        qmap = lambda b, p, i: (B * S // C + b, p)
        in_specs = [pl.BlockSpec((tq, 2 * HEAD_SLOT), qmap),
                    pl.BlockSpec((C, 2 * HEAD_SLOT), qmap),
                    pl.BlockSpec((C, 2 * HEAD_SLOT), qmap),
                    pl.BlockSpec(memory_space=pl.ANY)]
        args = (qm, km, vm, prev)
        body = functools.partial(_ctx_rows_kernel, _mla_kernel, with_x=False)
        aliases = {3: 0}
    return pl.pallas_call(
        body,
        grid=(B, npair, nq),
        in_specs=in_specs,
        out_specs=pl.BlockSpec((tq, LANES), qmap),
        out_shape=jax.ShapeDtypeStruct((T, MLA_HEADS * MLA_V), BF16),
        input_output_aliases=aliases,
        compiler_params=_cparams(("arbitrary",) * 3),
        name="mla_x" if with_x else "mla_c",
    )(*args)


def _swa_kernel(*refs, with_x, nblk):
    if with_x:
        sink_ref, q_ref, kp_ref, ko_ref, kn_ref, kc_ref, vp_ref, vo_ref, vn_ref, vc_ref, o_ref = refs
    else:
        sink_ref, q_ref, kc_ref, vc_ref, o_ref = refs
    n = pl.program_id(1)
    lane = lax.broadcasted_iota(jnp.int32, (1, LANES), 1)
    lo = lane < SWA_HEAD_DIM
    row2 = lax.broadcasted_iota(jnp.int32, (2 * BLOCK, 1), 0)
    if with_x:
        kall = jnp.concatenate([kp_ref[...], ko_ref[...], kn_ref[...], kc_ref[...]], axis=0)
        vall = jnp.concatenate([vp_ref[...], vo_ref[...], vn_ref[...], vc_ref[...]], axis=0)
        nk = kall.shape[0]
        qi = lax.broadcasted_iota(jnp.int32, (2 * BLOCK, nk), 0) % BLOCK
        kj = lax.broadcasted_iota(jnp.int32, (2 * BLOCK, nk), 1)
        bad_prev = jnp.logical_and(kj < BLOCK, jnp.logical_or(kj - qi < BLOCK - WINDOW, n == 0))
        bad_next = jnp.logical_and(jnp.logical_and(kj >= 2 * BLOCK, kj < 3 * BLOCK),
                                   jnp.logical_or(kj - qi > BLOCK + WINDOW, n == nblk - 1))
        bad = jnp.logical_or(bad_prev, bad_next)
    else:
        kall, vall = kc_ref[...], vc_ref[...]
    for s in range(SWA_HEADS // 2):
        q = q_ref[:, s * LANES:(s + 1) * LANES]
        zero = jnp.zeros_like(q)
        q2 = jnp.concatenate([jnp.where(lo, q, zero), jnp.where(lo, zero, q)], axis=0)
        sink = jnp.where(row2 < BLOCK, sink_ref[0, s], sink_ref[0, SWA_HEADS // 2 + s])
        sc = _dot_nt(q2, kall)
        if with_x:
            sc = jnp.where(bad, -jnp.inf, sc)
        m = jnp.maximum(jnp.max(sc, axis=-1, keepdims=True), sink)
        p = jnp.exp(sc - m)
        l = jnp.sum(p, axis=-1, keepdims=True) + jnp.exp(sink - m)
        o = _dot(p.astype(BF16), vall) / l
        o_ref[:, s * LANES:(s + 1) * LANES] = jnp.where(lo, o[:BLOCK], o[BLOCK:]).astype(BF16)


def _swa_attend(sink, qs, ks, vs, B, S, C, prev=None):
    T = qs.shape[0]
    cb = B * S // C
    with_x = prev is None
    if with_x:
        nblk = S // BLOCK
        qmap = lambda b, n: (b * nblk + n, 0)
        pmap = lambda b, n: (b * nblk + jnp.maximum(n - 1, 0), 0)
        nmap = lambda b, n: (b * nblk + jnp.minimum(n + 1, nblk - 1), 0)
        cmap = lambda b, n: (cb + b, 0)
        kv = lambda mp: pl.BlockSpec((BLOCK, LANES), mp)
        cspec = pl.BlockSpec((C, LANES), cmap)
        in_specs = [pl.BlockSpec(memory_space=pltpu.SMEM),
                    pl.BlockSpec((BLOCK, SWA_HEADS * SWA_HEAD_DIM), qmap),
                    kv(pmap), kv(qmap), kv(nmap), cspec, kv(pmap), kv(qmap), kv(nmap), cspec]
        args = (sink, qs, ks, ks, ks, ks, vs, vs, vs, vs)
        body = functools.partial(_swa_kernel, with_x=True, nblk=nblk)
        aliases = {}
    else:
        nblk = C // BLOCK
        qmap = lambda b, n: (cb * (C // BLOCK) + b * nblk + n, 0)
        cmap = lambda b, n: (cb + b, 0)
        cspec = pl.BlockSpec((C, LANES), cmap)
        in_specs = [pl.BlockSpec(memory_space=pltpu.SMEM),
                    pl.BlockSpec((BLOCK, SWA_HEADS * SWA_HEAD_DIM), qmap), cspec, cspec,
                    pl.BlockSpec(memory_space=pl.ANY)]
        args = (sink, qs, ks, vs, prev)
        body = functools.partial(_ctx_rows_kernel, _swa_kernel, with_x=False, nblk=nblk)
        aliases = {4: 0}
    return pl.pallas_call(
        body,
        grid=(B, nblk),
        in_specs=in_specs,
        out_specs=pl.BlockSpec((BLOCK, SWA_HEADS * SWA_HEAD_DIM), qmap),
        out_shape=jax.ShapeDtypeStruct((T, SWA_HEADS * SWA_HEAD_DIM), BF16),
        input_output_aliases=aliases,
        compiler_params=_cparams(("arbitrary",) * 2),
        name="swa_x" if with_x else "swa_c",
    )(*args)


def _fourier_kernel(c_ref, s_ref, pp_ref, w_ref, o_ref):
    f = _dot(c_ref[...], pp_ref[:, 0:D_FNET]) - _dot(s_ref[...], pp_ref[:, D_FNET:2 * D_FNET])
    o_ref[...] = _dot(f.astype(BF16), w_ref[...]).astype(BF16)


def _fourier(cmat, smat, pp, wblk, B, N, row0, prev=None):
    T = pp.shape[0]
    tq = min(512, N)
    nr = N // tq
    b0 = row0 // N
    o0 = row0 // tq
    in_specs = [pl.BlockSpec((tq, N), lambda r, b: (r, 0)),
                pl.BlockSpec((tq, N), lambda r, b: (r, 0)),
                pl.BlockSpec((N, 2 * D_FNET), lambda r, b: (b0 + b, 0)),
                pl.BlockSpec((D_FNET, D_FNET), lambda r, b: (0, 0))]
    args = (cmat, smat, pp, wblk)
    if prev is None:
        body = _fourier_kernel
        aliases = {}
    else:
        in_specs.append(pl.BlockSpec(memory_space=pl.ANY))
        args = args + (prev,)
        body = functools.partial(_ctx_rows_kernel, _fourier_kernel)
        aliases = {4: 0}
    return pl.pallas_call(
        body,
        grid=(nr, B),
        in_specs=in_specs,
        out_specs=pl.BlockSpec((tq, D_FNET), lambda r, b: (o0 + b * nr + r, 0)),
        out_shape=jax.ShapeDtypeStruct((T, D_FNET), BF16),
        input_output_aliases=aliases,
        compiler_params=_cparams(("arbitrary",) * 2),
        name="fourier_%d" % N,
    )(*args)


def _route_rows(sel, aff):
    G, K = N_EXPERT_GROUPS, EXPERTS_PER_GROUP
    gscore = []
    for g in range(G):
        a = sel[g * K:(g + 1) * K]
        best = None
        for i in range(K):
            for j in range(i + 1, K):
                v = a[i] + a[j]
                best = v if best is None else jnp.maximum(best, v)
        gscore.append(best)
    gb = jnp.zeros_like(gscore[0])
    gbest = gscore[0]
    for g in range(1, G):
        upd = gscore[g] > gbest
        gb = jnp.where(upd, float(g), gb)
        gbest = jnp.where(upd, gscore[g], gbest)
    cs, ca = [], []
    for i in range(K):
        c, a = sel[i], aff[i]
        for g in range(1, G):
            pick = gb == float(g)
            c = jnp.where(pick, sel[g * K + i], c)
            a = jnp.where(pick, aff[g * K + i], a)
        cs.append(c)
        ca.append(a)

    def first_max(vals):
        bi = jnp.zeros_like(vals[0])
        bv = vals[0]
        for i in range(1, K):
            upd = vals[i] > bv
            bi = jnp.where(upd, float(i), bi)
            bv = jnp.where(upd, vals[i], bv)
        return bi

    i1 = first_max(cs)
    cs2 = [jnp.where(i1 == float(i), -jnp.inf, cs[i]) for i in range(K)]
    i2 = first_max(cs2)
    a1 = sum(jnp.where(i1 == float(i), ca[i], 0.0) for i in range(K))
    a2 = sum(jnp.where(i2 == float(i), ca[i], 0.0) for i in range(K))
    den = a1 + a2
    return gb * K + i1, gb * K + i2, a1 / den, a2 / den


def _post_kernel(x_ref, mod_ref, fo_ref, ml_ref, sw_ref, wof_ref, wom_ref, wos_ref, n2g_ref,
                 rwh_ref, rwl_ref, rb_ref, tri_ref, ones_ref, xn_ref, h2_ref, rc_ref, rr_ref, cnt_ref):
    m = mod_ref[...]
    mix = (_dot(fo_ref[...], wof_ref[...]) + _dot(ml_ref[...], wom_ref[...])
           + _dot(sw_ref[...], wos_ref[...]))
    xn = x_ref[...] + m[2:3] * mix
    xn_ref[...] = xn
    h2 = _rms(xn, D_MODEL) * n2g_ref[...] * (1.0 + m[4:5]) + m[3:4]
    h2_ref[...] = h2.astype(BF16)
    hh = h2.astype(BF16)
    hl = (h2 - hh.astype(F32)).astype(BF16)
    logits = _dot(hh, rwh_ref[...]) + (_dot(hl, rwh_ref[...]) + _dot(hh, rwl_ref[...]))
    lt = logits.T[0:N_EXPERTS, :]
    aff_t = jax.nn.sigmoid(lt)
    sel_t = aff_t + rb_ref[...]
    sel = [sel_t[e:e + 1, :] for e in range(N_EXPERTS)]
    aff = [aff_t[e:e + 1, :] for e in range(N_EXPERTS)]
    e1, e2, w1, w2 = _route_rows(sel, aff)

    eio = lax.broadcasted_iota(jnp.int32, (N_EXPERTS, TM), 0).astype(F32)
    oh = jnp.concatenate([jnp.where(eio == e1, 1.0, 0.0), jnp.where(eio == e2, 1.0, 0.0)], axis=1)
    ohb = oh.astype(BF16)
    rank = _dot(ohb, tri_ref[...])
    cnt = _dot(ohb, ones_ref[...]).astype(jnp.int32)
    cnt8 = jnp.left_shift(jnp.right_shift(cnt + (CHUNK - 1), CHUNK_LOG2), CHUNK_LOG2)
    cnt_ref[...] = cnt8
    cnt8f = cnt8.astype(F32)
    off = jnp.zeros((1, 1), F32)
    slot = jnp.zeros((1, 2 * TM), F32)
    for e in range(N_EXPERTS):
        slot = slot + oh[e:e + 1, :] * (off + rank[e:e + 1, :])
        off = off + cnt8f[e:e + 1, 0:1]
    s0, s1 = slot[:, :TM], slot[:, TM:]

    sub = lax.broadcasted_iota(jnp.int32, (8, TM), 0)
    rr_ref[...] = jnp.where(sub == 0, s0, jnp.where(sub == 1, s1, 0.0))
    blk = jnp.where(sub == 0, e1, jnp.where(sub == 1, e2, jnp.where(sub == 2, w1, jnp.where(
        sub == 3, w2, jnp.where(sub == 4, s0, jnp.where(sub == 5, s1, 0.0))))))
    rows = jnp.concatenate([blk, jnp.zeros((LANES - 8, TM), F32)], axis=0)
    rc_ref[...] = rows.T


def _post(xc, mod, fo, ml, sw, wl, n2g, rw_hi, rw_lo, rb_col, nt, nx_tiles, tiles_per_batch):
    rows = nt * TM

    def bidx(i):
        return jnp.where(i < nx_tiles, i // tiles_per_batch, mod.shape[0] - 1)

    def full(a):
        return pl.BlockSpec(a.shape, lambda i: (0,) * a.ndim)

    row = lambda w: pl.BlockSpec((TM, w), lambda i: (i, 0))
    pair = np.arange(2 * TM)
    tri = jnp.asarray(pair[:, None] < pair[None, :], BF16)
    ones = jnp.ones((2 * TM, LANES), BF16)
    consts = [wl["wo_f"], wl["wo_m"], wl["wo_s"], n2g, rw_hi, rw_lo, rb_col, tri, ones]
    return pl.pallas_call(
        _post_kernel,
        grid=(nt,),
        in_specs=[row(D_MODEL), pl.BlockSpec((None, 8, D_MODEL), lambda i: (bidx(i), 0, 0)),
                  row(D_FNET), row(MLA_HEADS * MLA_V), row(SWA_HEADS * SWA_HEAD_DIM)]
                 + [full(a) for a in consts],
        out_specs=[row(D_MODEL), row(D_MODEL), row(LANES),
                   pl.BlockSpec((None, 8, TM), lambda i: (i, 0, 0)),
                   pl.BlockSpec((None, N_EXPERTS, LANES), lambda i: (i, 0, 0))],
        out_shape=[jax.ShapeDtypeStruct((rows, D_MODEL), F32),
                   jax.ShapeDtypeStruct((rows, D_MODEL), BF16),
                   jax.ShapeDtypeStruct((rows, LANES), F32),
                   jax.ShapeDtypeStruct((nt, 8, TM), F32),
                   jax.ShapeDtypeStruct((nt, N_EXPERTS, LANES), jnp.int32)],
        compiler_params=_cparams(("arbitrary",)),
        name="post",
    )(xc, mod, fo, ml, sw, *consts)


def _moe_tables(cnt8, te):
    nt = cnt8.shape[0]
    tile_prefix = jnp.cumsum(cnt8, axis=0) - cnt8
    tot = jnp.sum(cnt8, axis=0)
    tot_e = ((tot + te - 1) // te) * te
    goff = jnp.cumsum(tot_e) - tot_e
    dbase = goff[None, :] + tile_prefix
    nch = cnt8 // CHUNK
    cum = jnp.cumsum(nch, axis=1)
    k = jnp.arange(MAX_CHUNKS, dtype=jnp.int32)[None, :, None]
    owns = jnp.logical_and(k >= (cum - nch)[:, None, :], k < cum[:, None, :])
    dst = jnp.sum(jnp.where(owns, dbase[:, None, :] + CHUNK * (k - (cum - nch)[:, None, :]), 0), axis=-1)
    nchunks = cum[:, -1]
    padch = (tot_e - tot) // CHUNK
    cump = jnp.cumsum(padch)
    kp = jnp.arange(N_EXPERTS * (te // CHUNK), dtype=jnp.int32)[:, None]
    pown = jnp.logical_and(kp >= (cump - padch)[None, :], kp < cump[None, :])
    pdst = jnp.sum(jnp.where(pown, (goff + tot)[None, :] + CHUNK * (kp - (cump - padch)[None, :]), 0), axis=-1)
    npad = cump[-1]
    ntile_cum = jnp.cumsum(tot_e // te)
    nact = ntile_cum[-1]
    return dict(dst=dst.reshape(-1).astype(jnp.int32), nchunks=nchunks.astype(jnp.int32),
                pdst=pdst.astype(jnp.int32), npad=npad.reshape(1).astype(jnp.int32),
                ntile_cum=ntile_cum.astype(jnp.int32), nact=nact.reshape(1).astype(jnp.int32))


def _dispatch_kernel(dst_ref, nch_ref, pdst_ref, npad_ref, h_ref, rr_ref, xs_ref, sbuf, zbuf, sem, zsem):
    i = pl.program_id(0)
    nt = pl.num_programs(0)
    slot = i % 2

    def chunk_copy(sl, k, d):
        return pltpu.make_async_copy(
            sbuf.at[sl, pl.ds(pl.multiple_of(k * CHUNK, CHUNK), CHUNK), :],
            xs_ref.at[pl.ds(pl.multiple_of(d, CHUNK), CHUNK), :], sem.at[sl])

    def wait_tile(t, sl):
        def body(k, c):
            chunk_copy(sl, 0, 0).wait()
            return c
        lax.fori_loop(0, nch_ref[t], body, 0)

    def pad_copy(d):
        return pltpu.make_async_copy(zbuf, xs_ref.at[pl.ds(pl.multiple_of(d, CHUNK), CHUNK), :], zsem)

    @pl.when(i == 0)
    def _():
        zbuf[...] = jnp.zeros_like(zbuf)

        def start(k, c):
            pad_copy(pdst_ref[k]).start()
            return c
        lax.fori_loop(0, npad_ref[0], start, 0)

        def wait(k, c):
            pad_copy(0).wait()
            return c
        lax.fori_loop(0, npad_ref[0], wait, 0)

    @pl.when(i >= 2)
    def _():
        wait_tile(i - 2, slot)

    rr = rr_ref[...]
    sio = lax.broadcasted_iota(jnp.int32, (NSLOT, TM), 0).astype(F32)
    psel = jnp.where(jnp.logical_or(sio == rr[0:1, :], sio == rr[1:2, :]), 1.0, 0.0).astype(BF16)
    sbuf[slot] = _pack_bf16_pairs(_dot(psel, h_ref[...]))

    def issue(k, c):
        chunk_copy(slot, k, dst_ref[i * MAX_CHUNKS + k]).start()
        return c
    lax.fori_loop(0, nch_ref[i], issue, 0)

    @pl.when(i == nt - 1)
    def _():
        wait_tile(i, slot)

        @pl.when(i >= 1)
        def _():
            wait_tile(i - 1, 1 - slot)


def _dispatch(h2, rr, tb, rows_sorted):
    nt = rr.shape[0]
    return pl.pallas_call(
        _dispatch_kernel,
        grid_spec=pltpu.PrefetchScalarGridSpec(
            num_scalar_prefetch=4,
            grid=(nt,),
            in_specs=[pl.BlockSpec((TM, D_MODEL), lambda i, *_: (i, 0)),
                      pl.BlockSpec((None, 8, TM), lambda i, *_: (i, 0, 0))],
            out_specs=pl.BlockSpec(memory_space=pl.ANY),
            scratch_shapes=[pltpu.VMEM((2, NSLOT, D_PACK), jnp.uint32), pltpu.VMEM((CHUNK, D_PACK), jnp.uint32),
                            pltpu.SemaphoreType.DMA((2,)), pltpu.SemaphoreType.DMA(())]),
        out_shape=jax.ShapeDtypeStruct((rows_sorted, D_PACK), jnp.uint32),
        compiler_params=_cparams(("arbitrary",)),
        name="moe_dispatch",
    )(tb["dst"], tb["nchunks"], tb["pdst"], tb["npad"], h2, rr)


def _expert_kernel(te_ref, na_ref, x_ref, wg_ref, wu_ref, wd_ref, o_ref, wgb, wub, wdb):
    j = pl.program_id(0)
    active = j < na_ref[0]
    fresh = jnp.logical_or(j == 0, te_ref[j] != te_ref[jnp.maximum(j - 1, 0)])

    @pl.when(jnp.logical_and(active, fresh))
    def _():
        wgb[...] = wg_ref[...].astype(BF16)
        wub[...] = wu_ref[...].astype(BF16)
        wdb[...] = wd_ref[...].astype(BF16)

    @pl.when(active)
    def _():
        x = _unpack_bf16_pairs(x_ref[...])
        a = _dot(x, wgb[...])
        a = a * jax.nn.sigmoid(a) * _dot(x, wub[...])
        y = _dot(a.astype(BF16), wdb[...])
        o_ref[...] = _pack_bf16_pairs(y.astype(BF16).astype(F32))


def _experts(xs, wg, wu, wd, layer, tb, te):
    nte = xs.shape[0] // te
    jj = jnp.minimum(jnp.arange(nte, dtype=jnp.int32), tb["nact"][0] - 1)
    tile_e = jnp.sum(jj[:, None] >= tb["ntile_cum"][None, :], axis=-1).astype(jnp.int32)

    def tmap(j, te_ref, na):
        return (jnp.minimum(j, na[0] - 1), 0)

    def wmap(j, te_ref, na):
        return (layer, te_ref[j], 0, 0)

    return pl.pallas_call(
        _expert_kernel,
        grid_spec=pltpu.PrefetchScalarGridSpec(
            num_scalar_prefetch=2,
            grid=(nte,),
            in_specs=[pl.BlockSpec((te, D_PACK), tmap),
                      pl.BlockSpec((None, None, D_MODEL, D_EXPERT), wmap),
                      pl.BlockSpec((None, None, D_MODEL, D_EXPERT), wmap),
                      pl.BlockSpec((None, None, D_EXPERT, D_MODEL), wmap)],
            out_specs=pl.BlockSpec((te, D_PACK), tmap),
            scratch_shapes=[pltpu.VMEM((D_MODEL, D_EXPERT), BF16), pltpu.VMEM((D_MODEL, D_EXPERT), BF16),
                            pltpu.VMEM((D_EXPERT, D_MODEL), BF16)]),
        out_shape=jax.ShapeDtypeStruct(xs.shape, jnp.uint32),
        compiler_params=_cparams(("arbitrary",)),
        name="moe_experts",
    )(tile_e, tb["nact"], xs, wg, wu, wd)


def _combine_kernel(dst_ref, nch_ref, xn_ref, mod_ref, rc_ref, ys_ref, o_ref, gbuf, sem):
    i = pl.program_id(0)
    nt = pl.num_programs(0)
    slot = i % 2

    def chunk_copy(sl, k, d):
        return pltpu.make_async_copy(
            ys_ref.at[pl.ds(pl.multiple_of(d, CHUNK), CHUNK), :],
            gbuf.at[sl, pl.ds(pl.multiple_of(k * CHUNK, CHUNK), CHUNK), :], sem.at[sl])

    def issue_tile(t, sl):
        def body(k, c):
            chunk_copy(sl, k, dst_ref[t * MAX_CHUNKS + k]).start()
            return c
        lax.fori_loop(0, nch_ref[t], body, 0)

    @pl.when(i == 0)
    def _():
        gbuf[...] = jnp.zeros_like(gbuf)
        issue_tile(0, 0)

    @pl.when(i + 1 < nt)
    def _():
        issue_tile(i + 1, 1 - slot)

    def wait(k, c):
        chunk_copy(slot, 0, 0).wait()
        return c
    lax.fori_loop(0, nch_ref[i], wait, 0)

    g = _unpack_bf16_pairs(gbuf[slot])
    rc = rc_ref[...]
    lio = lax.broadcasted_iota(jnp.int32, (TM, NSLOT), 1).astype(F32)
    p0 = jnp.where(lio == rc[:, 4:5], 1.0, 0.0).astype(BF16)
    p1 = jnp.where(lio == rc[:, 5:6], 1.0, 0.0).astype(BF16)
    y = rc[:, 2:3] * _dot(p0, g) + rc[:, 3:4] * _dot(p1, g)
    o_ref[...] = xn_ref[...] + mod_ref[5:6, :] * y


def _combine(xn, mod, rc, ys, tb, nx_tiles, tiles_per_batch):
    rows = xn.shape[0]
    nt = rows // TM

    def bidx(i, *_):
        return (jnp.where(i < nx_tiles, i // tiles_per_batch, mod.shape[0] - 1), 0, 0)

    return pl.pallas_call(
        _combine_kernel,
        grid_spec=pltpu.PrefetchScalarGridSpec(
            num_scalar_prefetch=2,
            grid=(nt,),
            in_specs=[pl.BlockSpec((TM, D_MODEL), lambda i, *_: (i, 0)),
                      pl.BlockSpec((None, 8, D_MODEL), bidx),
                      pl.BlockSpec((TM, LANES), lambda i, *_: (i, 0)),
                      pl.BlockSpec(memory_space=pl.ANY)],
            out_specs=pl.BlockSpec((TM, D_MODEL), lambda i, *_: (i, 0)),
            scratch_shapes=[pltpu.VMEM((2, NSLOT, D_PACK), jnp.uint32), pltpu.SemaphoreType.DMA((2,))]),
        out_shape=jax.ShapeDtypeStruct((rows, D_MODEL), F32),
        compiler_params=_cparams(("arbitrary",)),
        name="moe_combine",
    )(tb["dst"], tb["nchunks"], xn, mod, rc, ys)


def _moe(h2, rc, rr, cnt, wg, wu, wd, layer, xn, mod, nx_tiles, tiles_per_batch):
    nt = rr.shape[0]
    te = TE
    max_rows = 2 * nt * TM + (CHUNK - 1) * N_EXPERTS * nt + N_EXPERTS * (te - CHUNK)
    rows_sorted = ((max_rows + te - 1) // te) * te
    tb = _moe_tables(cnt[:, :, 0], te)
    xs = _dispatch(h2, rr, tb, rows_sorted)
    ys = _experts(xs, wg, wu, wd, layer, tb, te)
    return _combine(xn, mod, rc, ys, tb, nx_tiles, tiles_per_batch)


def _rope_tables(S, C):
    t = jnp.arange(S)
    rows, cols = (t // GRID_W).astype(F32), (t % GRID_W).astype(F32)

    def axis_tabs(d_rot, lane0, width):
        d_axis = d_rot // 2
        inv = ROPE_THETA ** (-jnp.arange(0, d_axis, 2, dtype=F32) / d_axis)
        ar, ac = rows[:, None] * inv, cols[:, None] * inv
        ang = jnp.concatenate([ar, ar, ac, ac], axis=-1)
        q = d_rot // 4
        first = np.concatenate([np.ones(q), np.zeros(q), np.ones(q), np.zeros(q)]).astype(np.float32)
        pad = ((0, C), (lane0, width - lane0 - d_rot))
        cos = jnp.pad(jnp.cos(ang) - 1.0, pad) + 1.0
        sa = jnp.pad(-jnp.sin(ang) * first, pad)
        sb = jnp.pad(jnp.sin(ang) * (1.0 - first), pad)
        return cos, sa, sb

    cm, sam, sbm = axis_tabs(MLA_ROPE, MLA_NOPE, LANES)
    cs, sas, sbs = axis_tabs(SWA_HEAD_DIM, 0, SWA_HEAD_DIM)
    tile2 = lambda a: jnp.concatenate([a, a], axis=1)
    return dict(cm=cm, sam=sam, sbm=sbm, cs=tile2(cs), sas=tile2(sas), sbs=tile2(sbs))


_TWO_PI_HI = float(np.float32(2.0 * np.pi))
_TWO_PI_LO = float(np.float32(2.0 * np.pi - np.float64(np.float32(2.0 * np.pi))))


def _dft_mats(N):
    k = jnp.arange(N, dtype=jnp.int32)

    def cos_sin(rows, period):
        frac = ((rows[:, None] * k[None, :]) % period).astype(F32) / period
        ang = _TWO_PI_HI * frac + _TWO_PI_LO * frac
        return jnp.cos(ang), jnp.sin(ang)

    if N <= 4 * FNET_CH:
        return cos_sin(k, N)
    A = N // FNET_CH
    c1, s1 = cos_sin(jnp.arange(A, dtype=jnp.int32), A)
    c2, s2 = cos_sin(jnp.arange(FNET_CH, dtype=jnp.int32), N)
    c = c1[:, None, :] * c2[None, :, :] - s1[:, None, :] * s2[None, :, :]
    s = s1[:, None, :] * c2[None, :, :] + c1[:, None, :] * s2[None, :, :]
    return c.reshape(N, N), s.reshape(N, N)


def _dft64_blocks():
    c, s = _dft_mats(FNET_CH)
    eye = jnp.eye(FNET_GROUPS, dtype=F32)
    return jnp.concatenate([jnp.kron(eye, c), jnp.kron(eye, s)], axis=1)


def _layer_weights(l, w_in, fnet_w, mla_cq_g, mla_ckv_g, mla_w_uq, mla_w_uk, mla_w_uv, mla_q_g, mla_k_g,
                   swa_q_g, swa_k_g, swa_sink, w_out):
    D = D_MODEL
    wi = w_in[l]
    o_kr = D_FNET + MLA_Q_RANK + MLA_KV_RANK
    o_qs = o_kr + MLA_ROPE
    o_ks = o_qs + SWA_HEADS * SWA_HEAD_DIM
    o_vs = o_ks + SWA_KV_HEADS * SWA_HEAD_DIM
    order = np.array(SWA_HEAD_ORDER)
    w_qs = wi[:, o_qs:o_ks].reshape(D, SWA_HEADS, SWA_HEAD_DIM)[:, order].reshape(D, -1)
    z = lambda n: jnp.zeros((D, n), F32)
    win = jnp.concatenate([wi[:, :o_kr], w_qs, wi[:, o_ks:o_vs], wi[:, o_vs:],
                           z(MLA_NOPE), wi[:, o_kr:o_qs], z(LANES - MLA_QK)], axis=1)
    pad_slot = lambda w, d: jnp.pad(w.reshape(w.shape[0], MLA_HEADS, d),
                                    ((0, 0), (0, 0), (0, HEAD_SLOT - d))).reshape(w.shape[0], -1)
    wo = w_out[l]
    o_m = D_FNET
    o_s = D_FNET + MLA_HEADS * MLA_V
    wo_s = wo[o_s:].reshape(SWA_HEADS, SWA_HEAD_DIM, D)[order].reshape(-1, D)
    fw = fnet_w[l]
    wblk = jnp.zeros((D_FNET, D_FNET), F32)
    for g in range(FNET_GROUPS):
        wblk = wblk.at[g * FNET_CH:(g + 1) * FNET_CH, g * FNET_CH:(g + 1) * FNET_CH].set(fw[g])
    pad_g = lambda g: jnp.pad(g, (0, HEAD_SLOT - MLA_QK)).reshape(1, HEAD_SLOT)
    return dict(
        w_in=win.astype(BF16),
        cq_g=mla_cq_g[l].reshape(1, -1), ckv_g=mla_ckv_g[l].reshape(1, -1),
        w_uq=pad_slot(mla_w_uq[l], MLA_QK).astype(BF16),
        w_uk=pad_slot(mla_w_uk[l], MLA_NOPE).astype(BF16),
        w_uv=pad_slot(mla_w_uv[l], MLA_V).astype(BF16),
        mq_g=pad_g(mla_q_g[l]), mk_g=pad_g(mla_k_g[l]),
        sq_g=jnp.tile(swa_q_g[l], 2).reshape(1, LANES), sk_g=jnp.tile(swa_k_g[l], 2).reshape(1, LANES),
        sink=swa_sink[l].reshape(1, SWA_HEADS),
        wo_f=wo[:o_m].astype(BF16), wo_m=wo[o_m:o_s].astype(BF16), wo_s=wo_s.astype(BF16),
        fnet=wblk,
    )


def kernel(x, c, ctx, c_ctx, ada_w, ada_b, norm1_g, norm2_g, w_in, fnet_w, mla_cq_g, mla_ckv_g, mla_w_uq,
           mla_w_uk, mla_w_uv, mla_q_g, mla_k_g, swa_q_g, swa_k_g, swa_sink, w_out, router_w, router_b,
           exp_w_gate, exp_w_up, exp_w_down):
    B, S, D = x.shape
    C = ctx.shape[1]
    L = ada_w.shape[0]
    assert D == D_MODEL and S % 512 == 0 and C == TM and S % TM == 0
    nx_tiles = B * S // TM
    nt_all = nx_tiles + B * C // TM
    tiles_per_batch = S // TM

    tabs = _rope_tables(S, C)
    tabs["dft64"] = _dft64_blocks().astype(BF16)
    dft = {n: tuple(m.astype(BF16) for m in _dft_mats(n)) for n in (S, C)}
    fscale = {n: lax.rsqrt(jnp.full((), n * FNET_CH, F32)) for n in (S, C)}

    nmod = 16
    cvec = jnp.concatenate([c, c_ctx[None, :], jnp.zeros((nmod - B - 1, D), F32)], axis=0)
    mod_all = _adaln(cvec, ada_w, ada_b)
    mod_all = mod_all[:, :B + 1].reshape(L, B + 1, 6, D)
    mod_all = jnp.pad(mod_all, ((0, 0), (0, 0), (0, 2), (0, 0)))

    rw = jnp.pad(router_w, ((0, 0), (0, LANES - N_EXPERTS)))
    rw_hi = rw.astype(BF16)
    rw_lo = (rw - rw_hi.astype(F32)).astype(BF16)
    rb_col = router_b.reshape(N_EXPERTS, 1)

    xc = jnp.concatenate([x.reshape(B * S, D), ctx.reshape(B * C, D)], axis=0)
    for l in range(L):
        last = l == L - 1
        wl = _layer_weights(l, w_in, fnet_w, mla_cq_g, mla_ckv_g, mla_w_uq, mla_w_uk, mla_w_uv, mla_q_g,
                            mla_k_g, swa_q_g, swa_k_g, swa_sink, w_out)
        mod = mod_all[l]
        pr = _prep(xc, mod, norm1_g[l].reshape(1, D), wl, tabs, nx_tiles, tiles_per_batch)
        fo = _fourier(dft[S][0], dft[S][1], pr["pp"], (wl["fnet"] * fscale[S]).astype(BF16), B, S, 0)
        ml = _mla_attend(pr["qm"], pr["km"], pr["vm"], B, S, C)
        sw = _swa_attend(wl["sink"], pr["qs"], pr["ks"], pr["vs"], B, S, C)
        if not last:
            fo = _fourier(dft[C][0], dft[C][1], pr["pp"], (wl["fnet"] * fscale[C]).astype(BF16), B, C, B * S,
                          prev=fo)
            ml = _mla_attend(pr["qm"], pr["km"], pr["vm"], B, S, C, prev=ml)
            sw = _swa_attend(wl["sink"], pr["qs"], pr["ks"], pr["vs"], B, S, C, prev=sw)
        nt = nx_tiles if last else nt_all
        xn, h2, rc, rr, cnt = _post(xc, mod, fo, ml, sw, wl, norm2_g[l].reshape(1, D), rw_hi, rw_lo, rb_col,
                                    nt, nx_tiles, tiles_per_batch)
        xc = _moe(h2, rc, rr, cnt, exp_w_gate, exp_w_up, exp_w_down, l, xn, mod, nx_tiles, tiles_per_batch)
    return xc[:B * S].reshape(B, S, D)
```

```python
import functools

import numpy as np
import jax
import jax.numpy as jnp
from jax import lax
from jax.experimental import pallas as pl
from jax.experimental.pallas import tpu as pltpu

F32 = jnp.float32
BF16 = jnp.bfloat16

D_MODEL = 1024
GRID_W = 64
FNET_GROUPS = 4
FNET_CH = 64
D_FNET = FNET_GROUPS * FNET_CH
MLA_HEADS = 6
MLA_Q_RANK = 256
MLA_KV_RANK = 128
MLA_NOPE = 64
MLA_ROPE = 32
MLA_QK = MLA_NOPE + MLA_ROPE
MLA_V = 64
SWA_HEADS = 6
SWA_KV_HEADS = 2
SWA_HEAD_DIM = 64
WINDOW = 128
BLOCK = 128
N_EXPERTS = 16
N_EXPERT_GROUPS = 4
EXPERTS_PER_GROUP = 4
D_EXPERT = 512
ROPE_THETA = 10000.0
EPS = 1e-6

LANES = 128
TM = 256
HEAD_SLOT = LANES
CHUNK = 8
CHUNK_LOG2 = 3
NSLOT = 640
MAX_CHUNKS = NSLOT // CHUNK
TE = 512
PREP_CHAINS = 2
MIN_CHUNKS = 2 * TM // CHUNK
VMEM_LIMIT = 48 * 1024 * 1024

P_UF = 0
P_CQ = 256
P_CKV = 512
P_QS = 640
P_KS = 1024
P_VS = 1152
P_KR = 1280
P_TOT = 1408
SWA_HEAD_ORDER = (0, 3, 1, 4, 2, 5)


def _dot(a, b):
    return jnp.dot(a, b, preferred_element_type=F32)


def _dot_nt(a, b):
    return lax.dot_general(a, b, (((1,), (1,)), ((), ())), preferred_element_type=F32)


def _rms(x, n):
    return x * lax.rsqrt(jnp.sum(x * x, axis=-1, keepdims=True) / n + EPS)


def _rope(x, c, sa, sb, half):
    n = x.shape[-1]
    return x * c + pltpu.roll(x, n - half, 1) * sa + pltpu.roll(x, half, 1) * sb


def _cparams(sem):
    return pltpu.CompilerParams(dimension_semantics=sem, vmem_limit_bytes=VMEM_LIMIT)


def _adaln_kernel(c_ref, w_ref, b_ref, o_ref):
    c = c_ref[...]
    sc = c * jax.nn.sigmoid(c)
    w = w_ref[...]
    s_hi = sc.astype(BF16)
    s_lo = (sc - s_hi.astype(F32)).astype(BF16)
    w_hi = w.astype(BF16)
    w_lo = (w - w_hi.astype(F32)).astype(BF16)
    o_ref[...] = _dot(s_hi, w_hi) + (_dot(s_lo, w_hi) + _dot(s_hi, w_lo)) + b_ref[...]


def _adaln(cvec, ada_w, ada_b):
    L, D, N6 = ada_w.shape
    R = cvec.shape[0]
    bn = 512
    return pl.pallas_call(
        _adaln_kernel,
        grid=(L, N6 // bn),
        in_specs=[pl.BlockSpec((R, D), lambda l, j: (0, 0)),
                  pl.BlockSpec((None, D, bn), lambda l, j: (l, 0, j)),
                  pl.BlockSpec((None, 1, bn), lambda l, j: (l, 0, j))],
        out_specs=pl.BlockSpec((None, R, bn), lambda l, j: (l, 0, j)),
        out_shape=jax.ShapeDtypeStruct((L, R, N6), F32),
        compiler_params=_cparams(("arbitrary", "arbitrary")),
        name="adaln",
    )(cvec, ada_w, ada_b.reshape(L, 1, N6))


def _prep_kernel(x_ref, mod_ref, n1g_ref, win_ref, cqg_ref, ckvg_ref, wuq_ref, wuk_ref, wuv_ref,
                 mqg_ref, mkg_ref, sqg_ref, skg_ref, cm_ref, sam_ref, sbm_ref, cs_ref, sas_ref, sbs_ref,
                 dft_ref, pp_ref, qm_ref, km_ref, vm_ref, qs_ref, ks_ref, vs_ref):
    m = mod_ref[...]
    lo = lax.broadcasted_iota(jnp.int32, (1, LANES), 1) < SWA_HEAD_DIM
    slot_lane = jnp.bitwise_and(lax.broadcasted_iota(jnp.int32, (1, MLA_HEADS * HEAD_SLOT), 1), HEAD_SLOT - 1)
    vone = jnp.where(slot_lane == MLA_V, 1.0, 0.0)

    def head_norm(slab, g):
        sq = slab * slab
        s_lo = jnp.sum(jnp.where(lo, sq, 0.0), axis=-1, keepdims=True)
        s_hi = jnp.sum(jnp.where(lo, 0.0, sq), axis=-1, keepdims=True)
        r = jnp.where(lo, lax.rsqrt(s_lo * (1.0 / SWA_HEAD_DIM) + EPS),
                      lax.rsqrt(s_hi * (1.0 / SWA_HEAD_DIM) + EPS))
        return slab * r * g

    rows = TM // PREP_CHAINS
    for ch in range(PREP_CHAINS):
        rs = slice(ch * rows, (ch + 1) * rows)
        x = x_ref[rs, :]
        h = _rms(x, D_MODEL) * n1g_ref[...] * (1.0 + m[1:2]) + m[0:1]
        p = _dot(h.astype(BF16), win_ref[...])

        u = p[:, P_UF:P_UF + D_FNET].astype(BF16)
        pp_ref[rs, :] = _dot(u, dft_ref[...]).astype(BF16)

        cm, sam, sbm = cm_ref[rs, :], sam_ref[rs, :], sbm_ref[rs, :]
        cs, sas, sbs = cs_ref[rs, :], sas_ref[rs, :], sbs_ref[rs, :]

        cq = _rms(p[:, P_CQ:P_CQ + MLA_Q_RANK], MLA_Q_RANK) * cqg_ref[...]
        qraw = _dot(cq.astype(BF16), wuq_ref[...])
        mqg = mqg_ref[...]
        for hh in range(MLA_HEADS):
            sl = slice(hh * HEAD_SLOT, (hh + 1) * HEAD_SLOT)
            qn = _rms(qraw[:, sl], MLA_QK) * mqg
            qm_ref[rs, sl] = (_rope(qn, cm, sam, sbm, MLA_ROPE // 4) * (MLA_QK ** -0.5)).astype(BF16)

        ckv = (_rms(p[:, P_CKV:P_CKV + MLA_KV_RANK], MLA_KV_RANK) * ckvg_ref[...]).astype(BF16)
        knope = _dot(ckv, wuk_ref[...])
        vm_ref[rs, :] = (_dot(ckv, wuv_ref[...]) + vone).astype(BF16)
        kr = p[:, P_KR:P_KR + LANES]
        mkg = mkg_ref[...]
        kr_ss = jnp.sum(kr * kr, axis=-1, keepdims=True)
        kr_rot = _rope(kr * mkg, cm, sam, sbm, MLA_ROPE // 4)
        for hh in range(MLA_HEADS):
            sl = slice(hh * HEAD_SLOT, (hh + 1) * HEAD_SLOT)
            kn_h = knope[:, sl]
            r = lax.rsqrt((jnp.sum(kn_h * kn_h, axis=-1, keepdims=True) + kr_ss) / MLA_QK + EPS)
            km_ref[rs, sl] = ((kn_h * mkg + kr_rot) * r).astype(BF16)

        sqg = sqg_ref[...]
        for s in range(SWA_HEADS // 2):
            sl = slice(P_QS + s * LANES, P_QS + (s + 1) * LANES)
            qn = head_norm(p[:, sl], sqg)
            qs_ref[rs, s * LANES:(s + 1) * LANES] = (
                _rope(qn, cs, sas, sbs, SWA_HEAD_DIM // 4) * (SWA_HEAD_DIM ** -0.5)).astype(BF16)
        kn = head_norm(p[:, P_KS:P_KS + LANES], skg_ref[...])
        ks_ref[rs, :] = _rope(kn, cs, sas, sbs, SWA_HEAD_DIM // 4).astype(BF16)
        vs_ref[rs, :] = p[:, P_VS:P_VS + LANES].astype(BF16)


def _prep(xc, mod, n1g, wl, tabs, nx_tiles, tiles_per_batch):
    T = xc.shape[0]
    nt = T // TM

    def bidx(i):
        return jnp.where(i < nx_tiles, i // tiles_per_batch, mod.shape[0] - 1)

    def ridx(i):
        return jnp.where(i < nx_tiles, i % tiles_per_batch, tiles_per_batch)

    def full(a):
        return pl.BlockSpec(a.shape, lambda i: (0,) * a.ndim)

    tab_spec = pl.BlockSpec((TM, LANES), lambda i: (ridx(i), 0))
    row = lambda w: pl.BlockSpec((TM, w), lambda i: (i, 0))
    consts = [n1g, wl["w_in"], wl["cq_g"], wl["ckv_g"], wl["w_uq"], wl["w_uk"], wl["w_uv"],
              wl["mq_g"], wl["mk_g"], wl["sq_g"], wl["sk_g"]]
    outs = [("pp", 2 * D_FNET), ("qm", MLA_HEADS * HEAD_SLOT), ("km", MLA_HEADS * HEAD_SLOT),
            ("vm", MLA_HEADS * HEAD_SLOT), ("qs", SWA_HEADS * SWA_HEAD_DIM), ("ks", LANES), ("vs", LANES)]
    res = pl.pallas_call(
        _prep_kernel,
        grid=(nt,),
        in_specs=[row(D_MODEL), pl.BlockSpec((None, 8, D_MODEL), lambda i: (bidx(i), 0, 0))]
                 + [full(a) for a in consts] + [tab_spec] * 6 + [full(tabs["dft64"])],
        out_specs=[row(w) for _, w in outs],
        out_shape=[jax.ShapeDtypeStruct((T, w), BF16) for _, w in outs],
        compiler_params=_cparams(("arbitrary",)),
        name="prep",
    )(xc, mod, *consts, tabs["cm"], tabs["sam"], tabs["sbm"], tabs["cs"], tabs["sas"], tabs["sbs"],
      tabs["dft64"])
    return dict(zip([n for n, _ in outs], res))


def _mla_kernel(*refs, with_x):
    if with_x:
        q_ref, kx_ref, kc_ref, vx_ref, vc_ref, o_ref = refs
    else:
        q_ref, kc_ref, vc_ref, o_ref = refs
    outs = []
    for hh in range(2):
        sl = slice(hh * HEAD_SLOT, (hh + 1) * HEAD_SLOT)
        q = q_ref[:, sl]
        sc = _dot_nt(q, kc_ref[:, sl])
        m = jnp.max(sc, axis=-1, keepdims=True)
        if with_x:
            sx = _dot_nt(q, kx_ref[:, sl])
            m = jnp.maximum(m, jnp.max(sx, axis=-1, keepdims=True))
            px = jnp.exp(sx - m)
        pc = jnp.exp(sc - m)
        o = _dot(pc.astype(BF16), vc_ref[:, sl])
        if with_x:
            o = o + _dot(px.astype(BF16), vx_ref[:, sl])
        outs.append(o / o[:, MLA_V:MLA_V + 1])
    lane = lax.broadcasted_iota(jnp.int32, (1, LANES), 1)
    o_ref[...] = jnp.where(lane < MLA_V, outs[0], pltpu.roll(outs[1], MLA_V, 1)).astype(BF16)


def _ctx_rows_kernel(kernel_fn, *refs, **kw):
    kernel_fn(*refs[:-2], refs[-1], **kw)


def _mla_attend(qm, km, vm, B, S, C, prev=None):
    T = qm.shape[0]
    npair = MLA_HEADS // 2
    with_x = prev is None
    if with_x:
        tq = 512
        nq = S // tq
        qmap = lambda b, p, i: (b * nq + i, p)
        in_specs = [pl.BlockSpec((tq, 2 * HEAD_SLOT), qmap),
                    pl.BlockSpec((S, 2 * HEAD_SLOT), lambda b, p, i: (b, p)),
                    pl.BlockSpec((C, 2 * HEAD_SLOT), lambda b, p, i: (B * S // C + b, p)),
                    pl.BlockSpec((S, 2 * HEAD_SLOT), lambda b, p, i: (b, p)),
                    pl.BlockSpec((C, 2 * HEAD_SLOT), lambda b, p, i: (B * S // C + b, p))]
        args = (qm, km, km, vm, vm)
        body = functools.partial(_mla_kernel, with_x=True)
        aliases = {}
    else:
        tq = C
        nq = 1
        qmap = lambda b, p, i: (B * S // C + b, p)
        in_specs = [pl.BlockSpec((tq, 2 * HEAD_SLOT), qmap),
                    pl.BlockSpec((C, 2 * HEAD_SLOT), qmap),
                    pl.BlockSpec((C, 2 * HEAD_SLOT), qmap),
                    pl.BlockSpec(memory_space=pl.ANY)]
        args = (qm, km, vm, prev)
        body = functools.partial(_ctx_rows_kernel, _mla_kernel, with_x=False)
        aliases = {3: 0}
    return pl.pallas_call(
        body,
        grid=(B, npair, nq),
        in_specs=in_specs,
        out_specs=pl.BlockSpec((tq, LANES), qmap),
        out_shape=jax.ShapeDtypeStruct((T, MLA_HEADS * MLA_V), BF16),
        input_output_aliases=aliases,
        compiler_params=_cparams(("arbitrary",) * 3),
        name="mla_x" if with_x else "mla_c",
    )(*args)


def _swa_kernel(*refs, with_x, nblk):
    if with_x:
        sink_ref, q_ref, kp_ref, ko_ref, kn_ref, kc_ref, vp_ref, vo_ref, vn_ref, vc_ref, o_ref = refs
    else:
        sink_ref, q_ref, kc_ref, vc_ref, o_ref = refs
    n = pl.program_id(1)
    lane = lax.broadcasted_iota(jnp.int32, (1, LANES), 1)
    lo = lane < SWA_HEAD_DIM
    row2 = lax.broadcasted_iota(jnp.int32, (2 * BLOCK, 1), 0)
    if with_x:
        kall = jnp.concatenate([kp_ref[...], ko_ref[...], kn_ref[...], kc_ref[...]], axis=0)
        vall = jnp.concatenate([vp_ref[...], vo_ref[...], vn_ref[...], vc_ref[...]], axis=0)
        nk = kall.shape[0]
        qi = lax.broadcasted_iota(jnp.int32, (2 * BLOCK, nk), 0) % BLOCK
        kj = lax.broadcasted_iota(jnp.int32, (2 * BLOCK, nk), 1)
        bad_prev = jnp.logical_and(kj < BLOCK, jnp.logical_or(kj - qi < BLOCK - WINDOW, n == 0))
        bad_next = jnp.logical_and(jnp.logical_and(kj >= 2 * BLOCK, kj < 3 * BLOCK),
                                   jnp.logical_or(kj - qi > BLOCK + WINDOW, n == nblk - 1))
        bad = jnp.logical_or(bad_prev, bad_next)
    else:
        kall, vall = kc_ref[...], vc_ref[...]
    for s in range(SWA_HEADS // 2):
        q = q_ref[:, s * LANES:(s + 1) * LANES]
        zero = jnp.zeros_like(q)
        q2 = jnp.concatenate([jnp.where(lo, q, zero), jnp.where(lo, zero, q)], axis=0)
        sink = jnp.where(row2 < BLOCK, sink_ref[0, s], sink_ref[0, SWA_HEADS // 2 + s])
        sc = _dot_nt(q2, kall)
        if with_x:
            sc = jnp.where(bad, -jnp.inf, sc)
        m = jnp.maximum(jnp.max(sc, axis=-1, keepdims=True), sink)
        p = jnp.exp(sc - m)
        l = jnp.sum(p, axis=-1, keepdims=True) + jnp.exp(sink - m)
        o = _dot(p.astype(BF16), vall) / l
        o_ref[:, s * LANES:(s + 1) * LANES] = jnp.where(lo, o[:BLOCK], o[BLOCK:]).astype(BF16)


def _swa_attend(sink, qs, ks, vs, B, S, C, prev=None):
    T = qs.shape[0]
    cb = B * S // C
    with_x = prev is None
    if with_x:
        nblk = S // BLOCK
        qmap = lambda b, n: (b * nblk + n, 0)
        pmap = lambda b, n: (b * nblk + jnp.maximum(n - 1, 0), 0)
        nmap = lambda b, n: (b * nblk + jnp.minimum(n + 1, nblk - 1), 0)
        cmap = lambda b, n: (cb + b, 0)
        kv = lambda mp: pl.BlockSpec((BLOCK, LANES), mp)
        cspec = pl.BlockSpec((C, LANES), cmap)
        in_specs = [pl.BlockSpec(memory_space=pltpu.SMEM),
                    pl.BlockSpec((BLOCK, SWA_HEADS * SWA_HEAD_DIM), qmap),
                    kv(pmap), kv(qmap), kv(nmap), cspec, kv(pmap), kv(qmap), kv(nmap), cspec]
        args = (sink, qs, ks, ks, ks, ks, vs, vs, vs, vs)
        body = functools.partial(_swa_kernel, with_x=True, nblk=nblk)
        aliases = {}
    else:
        nblk = C // BLOCK
        qmap = lambda b, n: (cb * (C // BLOCK) + b * nblk + n, 0)
        cmap = lambda b, n: (cb + b, 0)
        cspec = pl.BlockSpec((C, LANES), cmap)
        in_specs = [pl.BlockSpec(memory_space=pltpu.SMEM),
                    pl.BlockSpec((BLOCK, SWA_HEADS * SWA_HEAD_DIM), qmap), cspec, cspec,
                    pl.BlockSpec(memory_space=pl.ANY)]
        args = (sink, qs, ks, vs, prev)
        body = functools.partial(_ctx_rows_kernel, _swa_kernel, with_x=False, nblk=nblk)
        aliases = {4: 0}
    return pl.pallas_call(
        body,
        grid=(B, nblk),
        in_specs=in_specs,
        out_specs=pl.BlockSpec((BLOCK, SWA_HEADS * SWA_HEAD_DIM), qmap),
        out_shape=jax.ShapeDtypeStruct((T, SWA_HEADS * SWA_HEAD_DIM), BF16),
        input_output_aliases=aliases,
        compiler_params=_cparams(("arbitrary",) * 2),
        name="swa_x" if with_x else "swa_c",
    )(*args)


def _fourier_kernel(c_ref, s_ref, pp_ref, w_ref, o_ref):
    f = _dot(c_ref[...], pp_ref[:, 0:D_FNET]) - _dot(s_ref[...], pp_ref[:, D_FNET:2 * D_FNET])
    o_ref[...] = _dot(f.astype(BF16), w_ref[...]).astype(BF16)


def _fourier(cmat, smat, pp, wblk, B, N, row0, prev=None):
    T = pp.shape[0]
    tq = min(512, N)
    nr = N // tq
    b0 = row0 // N
    o0 = row0 // tq
    in_specs = [pl.BlockSpec((tq, N), lambda r, b: (r, 0)),
                pl.BlockSpec((tq, N), lambda r, b: (r, 0)),
                pl.BlockSpec((N, 2 * D_FNET), lambda r, b: (b0 + b, 0)),
                pl.BlockSpec((D_FNET, D_FNET), lambda r, b: (0, 0))]
    args = (cmat, smat, pp, wblk)
    if prev is None:
        body = _fourier_kernel
        aliases = {}
    else:
        in_specs.append(pl.BlockSpec(memory_space=pl.ANY))
        args = args + (prev,)
        body = functools.partial(_ctx_rows_kernel, _fourier_kernel)
        aliases = {4: 0}
    return pl.pallas_call(
        body,
        grid=(nr, B),
        in_specs=in_specs,
        out_specs=pl.BlockSpec((tq, D_FNET), lambda r, b: (o0 + b * nr + r, 0)),
        out_shape=jax.ShapeDtypeStruct((T, D_FNET), BF16),
        input_output_aliases=aliases,
        compiler_params=_cparams(("arbitrary",) * 2),
        name="fourier_%d" % N,
    )(*args)


def _route_rows(sel, aff):
    G, K = N_EXPERT_GROUPS, EXPERTS_PER_GROUP
    gscore = []
    for g in range(G):
        a = sel[g * K:(g + 1) * K]
        best = None
        for i in range(K):
            for j in range(i + 1, K):
                v = a[i] + a[j]
                best = v if best is None else jnp.maximum(best, v)
        gscore.append(best)
    gb = jnp.zeros_like(gscore[0])
    gbest = gscore[0]
    for g in range(1, G):
        upd = gscore[g] > gbest
        gb = jnp.where(upd, float(g), gb)
        gbest = jnp.where(upd, gscore[g], gbest)
    cs, ca = [], []
    for i in range(K):
        c, a = sel[i], aff[i]
        for g in range(1, G):
            pick = gb == float(g)
            c = jnp.where(pick, sel[g * K + i], c)
            a = jnp.where(pick, aff[g * K + i], a)
        cs.append(c)
        ca.append(a)

    def first_max(vals):
        bi = jnp.zeros_like(vals[0])
        bv = vals[0]
        for i in range(1, K):
            upd = vals[i] > bv
            bi = jnp.where(upd, float(i), bi)
            bv = jnp.where(upd, vals[i], bv)
        return bi

    i1 = first_max(cs)
    cs2 = [jnp.where(i1 == float(i), -jnp.inf, cs[i]) for i in range(K)]
    i2 = first_max(cs2)
    a1 = sum(jnp.where(i1 == float(i), ca[i], 0.0) for i in range(K))
    a2 = sum(jnp.where(i2 == float(i), ca[i], 0.0) for i in range(K))
    den = a1 + a2
    return gb * K + i1, gb * K + i2, a1 / den, a2 / den


def _post_kernel(x_ref, mod_ref, fo_ref, ml_ref, sw_ref, wof_ref, wom_ref, wos_ref, n2g_ref,
                 rwh_ref, rwl_ref, rb_ref, tri_ref, ones_ref, xn_ref, h2_ref, rc_ref, rr_ref, cnt_ref):
    m = mod_ref[...]
    mix = (_dot(fo_ref[...], wof_ref[...]) + _dot(ml_ref[...], wom_ref[...])
           + _dot(sw_ref[...], wos_ref[...]))
    xn = x_ref[...] + m[2:3] * mix
    xn_ref[...] = xn
    h2 = _rms(xn, D_MODEL) * n2g_ref[...] * (1.0 + m[4:5]) + m[3:4]
    h2_ref[...] = h2.astype(BF16)
    hh = h2.astype(BF16)
    hl = (h2 - hh.astype(F32)).astype(BF16)
    logits = _dot(hh, rwh_ref[...]) + (_dot(hl, rwh_ref[...]) + _dot(hh, rwl_ref[...]))
    lt = logits.T[0:N_EXPERTS, :]
    aff_t = jax.nn.sigmoid(lt)
    sel_t = aff_t + rb_ref[...]
    sel = [sel_t[e:e + 1, :] for e in range(N_EXPERTS)]
    aff = [aff_t[e:e + 1, :] for e in range(N_EXPERTS)]
    e1, e2, w1, w2 = _route_rows(sel, aff)

    eio = lax.broadcasted_iota(jnp.int32, (N_EXPERTS, TM), 0).astype(F32)
    oh = jnp.concatenate([jnp.where(eio == e1, 1.0, 0.0), jnp.where(eio == e2, 1.0, 0.0)], axis=1)
    ohb = oh.astype(BF16)
    rank = _dot(ohb, tri_ref[...])
    cnt = _dot(ohb, ones_ref[...]).astype(jnp.int32)
    cnt8 = jnp.left_shift(jnp.right_shift(cnt + (CHUNK - 1), CHUNK_LOG2), CHUNK_LOG2)
    cnt_ref[...] = cnt8
    cnt8f = cnt8.astype(F32)
    off = jnp.zeros((1, 1), F32)
    slot = jnp.zeros((1, 2 * TM), F32)
    for e in range(N_EXPERTS):
        slot = slot + oh[e:e + 1, :] * (off + rank[e:e + 1, :])
        off = off + cnt8f[e:e + 1, 0:1]
    s0, s1 = slot[:, :TM], slot[:, TM:]

    sub = lax.broadcasted_iota(jnp.int32, (8, TM), 0)
    rr_ref[...] = jnp.where(sub == 0, s0, jnp.where(sub == 1, s1, 0.0))
    blk = jnp.where(sub == 0, e1, jnp.where(sub == 1, e2, jnp.where(sub == 2, w1, jnp.where(
        sub == 3, w2, jnp.where(sub == 4, s0, jnp.where(sub == 5, s1, 0.0))))))
    rows = jnp.concatenate([blk, jnp.zeros((LANES - 8, TM), F32)], axis=0)
    rc_ref[...] = rows.T


def _post(xc, mod, fo, ml, sw, wl, n2g, rw_hi, rw_lo, rb_col, nt, nx_tiles, tiles_per_batch):
    rows = nt * TM

    def bidx(i):
        return jnp.where(i < nx_tiles, i // tiles_per_batch, mod.shape[0] - 1)

    def full(a):
        return pl.BlockSpec(a.shape, lambda i: (0,) * a.ndim)

    row = lambda w: pl.BlockSpec((TM, w), lambda i: (i, 0))
    pair = np.arange(2 * TM)
    tri = jnp.asarray(pair[:, None] < pair[None, :], BF16)
    ones = jnp.ones((2 * TM, LANES), BF16)
    consts = [wl["wo_f"], wl["wo_m"], wl["wo_s"], n2g, rw_hi, rw_lo, rb_col, tri, ones]
    return pl.pallas_call(
        _post_kernel,
        grid=(nt,),
        in_specs=[row(D_MODEL), pl.BlockSpec((None, 8, D_MODEL), lambda i: (bidx(i), 0, 0)),
                  row(D_FNET), row(MLA_HEADS * MLA_V), row(SWA_HEADS * SWA_HEAD_DIM)]
                 + [full(a) for a in consts],
        out_specs=[row(D_MODEL), row(D_MODEL), row(LANES),
                   pl.BlockSpec((None, 8, TM), lambda i: (i, 0, 0)),
                   pl.BlockSpec((None, N_EXPERTS, LANES), lambda i: (i, 0, 0))],
        out_shape=[jax.ShapeDtypeStruct((rows, D_MODEL), F32),
                   jax.ShapeDtypeStruct((rows, D_MODEL), BF16),
                   jax.ShapeDtypeStruct((rows, LANES), F32),
                   jax.ShapeDtypeStruct((nt, 8, TM), F32),
                   jax.ShapeDtypeStruct((nt, N_EXPERTS, LANES), jnp.int32)],
        compiler_params=_cparams(("arbitrary",)),
        name="post",
    )(xc, mod, fo, ml, sw, *consts)


def _moe_tables(cnt8, te):
    nt = cnt8.shape[0]
    tile_prefix = jnp.cumsum(cnt8, axis=0) - cnt8
    tot = jnp.sum(cnt8, axis=0)
    tot_e = ((tot + te - 1) // te) * te
    goff = jnp.cumsum(tot_e) - tot_e
    dbase = goff[None, :] + tile_prefix
    nch = cnt8 // CHUNK
    cum = jnp.cumsum(nch, axis=1)
    k = jnp.arange(MAX_CHUNKS, dtype=jnp.int32)[None, :, None]
    owns = jnp.logical_and(k >= (cum - nch)[:, None, :], k < cum[:, None, :])
    dst = jnp.sum(jnp.where(owns, dbase[:, None, :] + CHUNK * (k - (cum - nch)[:, None, :]), 0), axis=-1)
    nchunks = cum[:, -1]
    padch = (tot_e - tot) // CHUNK
    cump = jnp.cumsum(padch)
    kp = jnp.arange(N_EXPERTS * (te // CHUNK), dtype=jnp.int32)[:, None]
    pown = jnp.logical_and(kp >= (cump - padch)[None, :], kp < cump[None, :])
    pdst = jnp.sum(jnp.where(pown, (goff + tot)[None, :] + CHUNK * (kp - (cump - padch)[None, :]), 0), axis=-1)
    npad = cump[-1]
    ntile_cum = jnp.cumsum(tot_e // te)
    nact = ntile_cum[-1]
    return dict(dst=dst.reshape(-1).astype(jnp.int32), nchunks=nchunks.astype(jnp.int32),
                pdst=pdst.astype(jnp.int32), npad=npad.reshape(1).astype(jnp.int32),
                ntile_cum=ntile_cum.astype(jnp.int32), nact=nact.reshape(1).astype(jnp.int32))


def _dispatch_kernel(dst_ref, nch_ref, pdst_ref, npad_ref, h_ref, rr_ref, xs_ref, sbuf, zbuf, sem, zsem):
    i = pl.program_id(0)
    nt = pl.num_programs(0)
    slot = i % 2

    def chunk_copy(sl, k, d):
        return pltpu.make_async_copy(
            sbuf.at[sl, pl.ds(pl.multiple_of(k * CHUNK, CHUNK), CHUNK), :],
            xs_ref.at[pl.ds(pl.multiple_of(d, CHUNK), CHUNK), :], sem.at[sl])

    def wait_tile(t, sl):
        n = nch_ref[t] * CHUNK
        pltpu.make_async_copy(sbuf.at[sl, pl.ds(0, n), :], xs_ref.at[pl.ds(0, n), :], sem.at[sl]).wait()

    def pad_copy(d):
        return pltpu.make_async_copy(zbuf, xs_ref.at[pl.ds(pl.multiple_of(d, CHUNK), CHUNK), :], zsem)

    @pl.when(i == 0)
    def _():
        zbuf[...] = jnp.zeros_like(zbuf)

        def start(k, c):
            pad_copy(pdst_ref[k]).start()
            return c
        lax.fori_loop(0, npad_ref[0], start, 0)

        def wait(k, c):
            pad_copy(0).wait()
            return c
        lax.fori_loop(0, npad_ref[0], wait, 0)

    @pl.when(i >= 2)
    def _():
        wait_tile(i - 2, slot)

    rr = rr_ref[...]
    sio = lax.broadcasted_iota(jnp.int32, (NSLOT, TM), 0).astype(F32)
    psel = jnp.where(jnp.logical_or(sio == rr[0:1, :], sio == rr[1:2, :]), 1.0, 0.0).astype(BF16)
    sbuf[slot] = _dot(psel, h_ref[...])

    for k in range(MIN_CHUNKS):
        chunk_copy(slot, k, dst_ref[i * MAX_CHUNKS + k]).start(priority=k % 2)

    def issue(k, c):
        chunk_copy(slot, k, dst_ref[i * MAX_CHUNKS + k]).start()
        return c
    lax.fori_loop(MIN_CHUNKS, nch_ref[i], issue, 0)

    @pl.when(i == nt - 1)
    def _():
        wait_tile(i, slot)

        @pl.when(i >= 1)
        def _():
            wait_tile(i - 1, 1 - slot)


def _dispatch(h2, rr, tb, rows_sorted):
    nt = rr.shape[0]
    return pl.pallas_call(
        _dispatch_kernel,
        grid_spec=pltpu.PrefetchScalarGridSpec(
            num_scalar_prefetch=4,
            grid=(nt,),
            in_specs=[pl.BlockSpec((TM, D_MODEL), lambda i, *_: (i, 0)),
                      pl.BlockSpec((None, 8, TM), lambda i, *_: (i, 0, 0))],
            out_specs=pl.BlockSpec(memory_space=pl.ANY),
            scratch_shapes=[pltpu.VMEM((2, NSLOT, D_MODEL), F32), pltpu.VMEM((CHUNK, D_MODEL), F32),
                            pltpu.SemaphoreType.DMA((2,)), pltpu.SemaphoreType.DMA(())]),
        out_shape=jax.ShapeDtypeStruct((rows_sorted, D_MODEL), F32),
        compiler_params=_cparams(("arbitrary",)),
        name="moe_dispatch",
    )(tb["dst"], tb["nchunks"], tb["pdst"], tb["npad"], h2, rr)


def _expert_kernel(te_ref, na_ref, x_ref, wg_ref, wu_ref, wd_ref, o_ref, wgb, wub, wdb):
    j = pl.program_id(0)
    active = j < na_ref[0]
    fresh = jnp.logical_or(j == 0, te_ref[j] != te_ref[jnp.maximum(j - 1, 0)])

    @pl.when(jnp.logical_and(active, fresh))
    def _():
        wgb[...] = wg_ref[...].astype(BF16)
        wub[...] = wu_ref[...].astype(BF16)
        wdb[...] = wd_ref[...].astype(BF16)

    @pl.when(active)
    def _():
        x = x_ref[...].astype(BF16)
        a = _dot(x, wgb[...])
        a = a * jax.nn.sigmoid(a) * _dot(x, wub[...])
        o_ref[...] = _dot(a.astype(BF16), wdb[...])


def _experts(xs, wg, wu, wd, layer, tb, te):
    nte = xs.shape[0] // te
    jj = jnp.minimum(jnp.arange(nte, dtype=jnp.int32), tb["nact"][0] - 1)
    tile_e = jnp.sum(jj[:, None] >= tb["ntile_cum"][None, :], axis=-1).astype(jnp.int32)

    def tmap(j, te_ref, na):
        return (jnp.minimum(j, na[0] - 1), 0)

    def wmap(j, te_ref, na):
        return (layer, te_ref[j], 0, 0)

    return pl.pallas_call(
        _expert_kernel,
        grid_spec=pltpu.PrefetchScalarGridSpec(
            num_scalar_prefetch=2,
            grid=(nte,),
            in_specs=[pl.BlockSpec((te, D_MODEL), tmap),
                      pl.BlockSpec((None, None, D_MODEL, D_EXPERT), wmap),
                      pl.BlockSpec((None, None, D_MODEL, D_EXPERT), wmap),
                      pl.BlockSpec((None, None, D_EXPERT, D_MODEL), wmap)],
            out_specs=pl.BlockSpec((te, D_MODEL), tmap),
            scratch_shapes=[pltpu.VMEM((D_MODEL, D_EXPERT), BF16), pltpu.VMEM((D_MODEL, D_EXPERT), BF16),
                            pltpu.VMEM((D_EXPERT, D_MODEL), BF16)]),
        out_shape=jax.ShapeDtypeStruct(xs.shape, F32),
        compiler_params=_cparams(("arbitrary",)),
        name="moe_experts",
    )(tile_e, tb["nact"], xs, wg, wu, wd)


def _combine_kernel(dst_ref, nch_ref, xn_ref, mod_ref, rc_ref, ys_ref, o_ref, gbuf, sem):
    i = pl.program_id(0)
    nt = pl.num_programs(0)
    slot = i % 2

    def chunk_copy(sl, k, d):
        return pltpu.make_async_copy(
            ys_ref.at[pl.ds(pl.multiple_of(d, CHUNK), CHUNK), :],
            gbuf.at[sl, pl.ds(pl.multiple_of(k * CHUNK, CHUNK), CHUNK), :], sem.at[sl])

    def issue_tile(t, sl):
        for k in range(MIN_CHUNKS):
            chunk_copy(sl, k, dst_ref[t * MAX_CHUNKS + k]).start(priority=k % 2)

        def body(k, c):
            chunk_copy(sl, k, dst_ref[t * MAX_CHUNKS + k]).start()
            return c
        lax.fori_loop(MIN_CHUNKS, nch_ref[t], body, 0)

    @pl.when(i == 0)
    def _():
        gbuf[...] = jnp.zeros_like(gbuf)
        issue_tile(0, 0)

    @pl.when(i + 1 < nt)
    def _():
        issue_tile(i + 1, 1 - slot)

    n = nch_ref[i] * CHUNK
    pltpu.make_async_copy(ys_ref.at[pl.ds(0, n), :], gbuf.at[slot, pl.ds(0, n), :], sem.at[slot]).wait()

    g = gbuf[slot].astype(BF16)
    rc = rc_ref[...]
    lio = lax.broadcasted_iota(jnp.int32, (TM, NSLOT), 1).astype(F32)
    p0 = jnp.where(lio == rc[:, 4:5], 1.0, 0.0).astype(BF16)
    p1 = jnp.where(lio == rc[:, 5:6], 1.0, 0.0).astype(BF16)
    y = rc[:, 2:3] * _dot(p0, g) + rc[:, 3:4] * _dot(p1, g)
    o_ref[...] = xn_ref[...] + mod_ref[5:6, :] * y


def _combine(xn, mod, rc, ys, tb, nx_tiles, tiles_per_batch):
    rows = xn.shape[0]
    nt = rows // TM

    def bidx(i, *_):
        return (jnp.where(i < nx_tiles, i // tiles_per_batch, mod.shape[0] - 1), 0, 0)

    return pl.pallas_call(
        _combine_kernel,
        grid_spec=pltpu.PrefetchScalarGridSpec(
            num_scalar_prefetch=2,
            grid=(nt,),
            in_specs=[pl.BlockSpec((TM, D_MODEL), lambda i, *_: (i, 0)),
                      pl.BlockSpec((None, 8, D_MODEL), bidx),
                      pl.BlockSpec((TM, LANES), lambda i, *_: (i, 0)),
                      pl.BlockSpec(memory_space=pl.ANY)],
            out_specs=pl.BlockSpec((TM, D_MODEL), lambda i, *_: (i, 0)),
            scratch_shapes=[pltpu.VMEM((2, NSLOT, D_MODEL), F32), pltpu.SemaphoreType.DMA((2,))]),
        out_shape=jax.ShapeDtypeStruct((rows, D_MODEL), F32),
        compiler_params=_cparams(("arbitrary",)),
        name="moe_combine",
    )(tb["dst"], tb["nchunks"], xn, mod, rc, ys)


def _moe(h2, rc, rr, cnt, wg, wu, wd, layer, xn, mod, nx_tiles, tiles_per_batch):
    nt = rr.shape[0]
    te = TE
    max_rows = 2 * nt * TM + (CHUNK - 1) * N_EXPERTS * nt + N_EXPERTS * (te - CHUNK)
    rows_sorted = ((max_rows + te - 1) // te) * te
    tb = _moe_tables(cnt[:, :, 0], te)
    xs = _dispatch(h2, rr, tb, rows_sorted)
    ys = _experts(xs, wg, wu, wd, layer, tb, te)
    return _combine(xn, mod, rc, ys, tb, nx_tiles, tiles_per_batch)


def _rope_tables(S, C):
    t = jnp.arange(S)
    rows, cols = (t // GRID_W).astype(F32), (t % GRID_W).astype(F32)

    def axis_tabs(d_rot, lane0, width):
        d_axis = d_rot // 2
        inv = ROPE_THETA ** (-jnp.arange(0, d_axis, 2, dtype=F32) / d_axis)
        ar, ac = rows[:, None] * inv, cols[:, None] * inv
        ang = jnp.concatenate([ar, ar, ac, ac], axis=-1)
        q = d_rot // 4
        first = np.concatenate([np.ones(q), np.zeros(q), np.ones(q), np.zeros(q)]).astype(np.float32)
        pad = ((0, C), (lane0, width - lane0 - d_rot))
        cos = jnp.pad(jnp.cos(ang) - 1.0, pad) + 1.0
        sa = jnp.pad(-jnp.sin(ang) * first, pad)
        sb = jnp.pad(jnp.sin(ang) * (1.0 - first), pad)
        return cos, sa, sb

    cm, sam, sbm = axis_tabs(MLA_ROPE, MLA_NOPE, LANES)
    cs, sas, sbs = axis_tabs(SWA_HEAD_DIM, 0, SWA_HEAD_DIM)
    tile2 = lambda a: jnp.concatenate([a, a], axis=1)
    return dict(cm=cm, sam=sam, sbm=sbm, cs=tile2(cs), sas=tile2(sas), sbs=tile2(sbs))


_TWO_PI_HI = float(np.float32(2.0 * np.pi))
_TWO_PI_LO = float(np.float32(2.0 * np.pi - np.float64(np.float32(2.0 * np.pi))))


def _dft_mats(N):
    k = jnp.arange(N, dtype=jnp.int32)

    def cos_sin(rows, period):
        frac = ((rows[:, None] * k[None, :]) % period).astype(F32) / period
        ang = _TWO_PI_HI * frac + _TWO_PI_LO * frac
        return jnp.cos(ang), jnp.sin(ang)

    if N <= 4 * FNET_CH:
        return cos_sin(k, N)
    A = N // FNET_CH
    c1, s1 = cos_sin(jnp.arange(A, dtype=jnp.int32), A)
    c2, s2 = cos_sin(jnp.arange(FNET_CH, dtype=jnp.int32), N)
    c = c1[:, None, :] * c2[None, :, :] - s1[:, None, :] * s2[None, :, :]
    s = s1[:, None, :] * c2[None, :, :] + c1[:, None, :] * s2[None, :, :]
    return c.reshape(N, N), s.reshape(N, N)


def _dft64_blocks():
    c, s = _dft_mats(FNET_CH)
    eye = jnp.eye(FNET_GROUPS, dtype=F32)
    return jnp.concatenate([jnp.kron(eye, c), jnp.kron(eye, s)], axis=1)


def _layer_weights(l, w_in, fnet_w, mla_cq_g, mla_ckv_g, mla_w_uq, mla_w_uk, mla_w_uv, mla_q_g, mla_k_g,
                   swa_q_g, swa_k_g, swa_sink, w_out):
    D = D_MODEL
    wi = w_in[l]
    o_kr = D_FNET + MLA_Q_RANK + MLA_KV_RANK
    o_qs = o_kr + MLA_ROPE
    o_ks = o_qs + SWA_HEADS * SWA_HEAD_DIM
    o_vs = o_ks + SWA_KV_HEADS * SWA_HEAD_DIM
    order = np.array(SWA_HEAD_ORDER)
    w_qs = wi[:, o_qs:o_ks].reshape(D, SWA_HEADS, SWA_HEAD_DIM)[:, order].reshape(D, -1)
    z = lambda n: jnp.zeros((D, n), F32)
    win = jnp.concatenate([wi[:, :o_kr], w_qs, wi[:, o_ks:o_vs], wi[:, o_vs:],
                           z(MLA_NOPE), wi[:, o_kr:o_qs], z(LANES - MLA_QK)], axis=1)
    pad_slot = lambda w, d: jnp.pad(w.reshape(w.shape[0], MLA_HEADS, d),
                                    ((0, 0), (0, 0), (0, HEAD_SLOT - d))).reshape(w.shape[0], -1)
    wo = w_out[l]
    o_m = D_FNET
    o_s = D_FNET + MLA_HEADS * MLA_V
    wo_s = wo[o_s:].reshape(SWA_HEADS, SWA_HEAD_DIM, D)[order].reshape(-1, D)
    fw = fnet_w[l]
    wblk = jnp.zeros((D_FNET, D_FNET), F32)
    for g in range(FNET_GROUPS):
        wblk = wblk.at[g * FNET_CH:(g + 1) * FNET_CH, g * FNET_CH:(g + 1) * FNET_CH].set(fw[g])
    pad_g = lambda g: jnp.pad(g, (0, HEAD_SLOT - MLA_QK)).reshape(1, HEAD_SLOT)
    return dict(
        w_in=win.astype(BF16),
        cq_g=mla_cq_g[l].reshape(1, -1), ckv_g=mla_ckv_g[l].reshape(1, -1),
        w_uq=pad_slot(mla_w_uq[l], MLA_QK).astype(BF16),
        w_uk=pad_slot(mla_w_uk[l], MLA_NOPE).astype(BF16),
        w_uv=pad_slot(mla_w_uv[l], MLA_V).astype(BF16),
        mq_g=pad_g(mla_q_g[l]), mk_g=pad_g(mla_k_g[l]),
        sq_g=jnp.tile(swa_q_g[l], 2).reshape(1, LANES), sk_g=jnp.tile(swa_k_g[l], 2).reshape(1, LANES),
        sink=swa_sink[l].reshape(1, SWA_HEADS),
        wo_f=wo[:o_m].astype(BF16), wo_m=wo[o_m:o_s].astype(BF16), wo_s=wo_s.astype(BF16),
        fnet=wblk,
    )


def kernel(x, c, ctx, c_ctx, ada_w, ada_b, norm1_g, norm2_g, w_in, fnet_w, mla_cq_g, mla_ckv_g, mla_w_uq,
           mla_w_uk, mla_w_uv, mla_q_g, mla_k_g, swa_q_g, swa_k_g, swa_sink, w_out, router_w, router_b,
           exp_w_gate, exp_w_up, exp_w_down):
    B, S, D = x.shape
    C = ctx.shape[1]
    L = ada_w.shape[0]
    assert D == D_MODEL and S % 512 == 0 and C == TM and S % TM == 0
    nx_tiles = B * S // TM
    nt_all = nx_tiles + B * C // TM
    tiles_per_batch = S // TM

    tabs = _rope_tables(S, C)
    tabs["dft64"] = _dft64_blocks().astype(BF16)
    dft = {n: tuple(m.astype(BF16) for m in _dft_mats(n)) for n in (S, C)}
    fscale = {n: lax.rsqrt(jnp.full((), n * FNET_CH, F32)) for n in (S, C)}

    nmod = 16
    cvec = jnp.concatenate([c, c_ctx[None, :], jnp.zeros((nmod - B - 1, D), F32)], axis=0)
    mod_all = _adaln(cvec, ada_w, ada_b)
    mod_all = mod_all[:, :B + 1].reshape(L, B + 1, 6, D)
    mod_all = jnp.pad(mod_all, ((0, 0), (0, 0), (0, 2), (0, 0)))

    rw = jnp.pad(router_w, ((0, 0), (0, LANES - N_EXPERTS)))
    rw_hi = rw.astype(BF16)
    rw_lo = (rw - rw_hi.astype(F32)).astype(BF16)
    rb_col = router_b.reshape(N_EXPERTS, 1)

    xc = jnp.concatenate([x.reshape(B * S, D), ctx.reshape(B * C, D)], axis=0)
    for l in range(L):
        last = l == L - 1
        wl = _layer_weights(l, w_in, fnet_w, mla_cq_g, mla_ckv_g, mla_w_uq, mla_w_uk, mla_w_uv, mla_q_g,
                            mla_k_g, swa_q_g, swa_k_g, swa_sink, w_out)
        mod = mod_all[l]
        pr = _prep(xc, mod, norm1_g[l].reshape(1, D), wl, tabs, nx_tiles, tiles_per_batch)
        fo = _fourier(dft[S][0], dft[S][1], pr["pp"], (wl["fnet"] * fscale[S]).astype(BF16), B, S, 0)
        ml = _mla_attend(pr["qm"], pr["km"], pr["vm"], B, S, C)
        sw = _swa_attend(wl["sink"], pr["qs"], pr["ks"], pr["vs"], B, S, C)
        if not last:
            fo = _fourier(dft[C][0], dft[C][1], pr["pp"], (wl["fnet"] * fscale[C]).astype(BF16), B, C, B * S,
                          prev=fo)
            ml = _mla_attend(pr["qm"], pr["km"], pr["vm"], B, S, C, prev=ml)
            sw = _swa_attend(wl["sink"], pr["qs"], pr["ks"], pr["vs"], B, S, C, prev=sw)
        nt = nx_tiles if last else nt_all
        xn, h2, rc, rr, cnt = _post(xc, mod, fo, ml, sw, wl, norm2_g[l].reshape(1, D), rw_hi, rw_lo, rb_col,
                                    nt, nx_tiles, tiles_per_batch)
        xc = _moe(h2, rc, rr, cnt, exp_w_gate, exp_w_up, exp_w_down, l, xn, mod, nx_tiles, tiles_per_batch)
    return xc[:B * S].reshape(B, S, D)
```

```python
import functools

import numpy as np
import jax
import jax.numpy as jnp
from jax import lax
from jax.experimental import pallas as pl
from jax.experimental.pallas import tpu as pltpu

F32 = jnp.float32
BF16 = jnp.bfloat16

D_MODEL = 1024
GRID_W = 64
FNET_GROUPS = 4
FNET_CH = 64
D_FNET = FNET_GROUPS * FNET_CH
MLA_HEADS = 6
MLA_Q_RANK = 256
MLA_KV_RANK = 128
MLA_NOPE = 64
MLA_ROPE = 32
MLA_QK = MLA_NOPE + MLA_ROPE
MLA_V = 64
SWA_HEADS = 6
SWA_KV_HEADS = 2
SWA_HEAD_DIM = 64
WINDOW = 128
BLOCK = 128
N_EXPERTS = 16
N_EXPERT_GROUPS = 4
EXPERTS_PER_GROUP = 4
D_EXPERT = 512
ROPE_THETA = 10000.0
EPS = 1e-6

LANES = 128
TM = 256
HEAD_SLOT = LANES
CHUNK = 8
CHUNK_LOG2 = 3
NSLOT = 640
MAX_CHUNKS = NSLOT // CHUNK
TE = 512
PREP_CHAIN_ROWS = 128
POST_CHAIN_ROWS = TM
PREP_TM = 256
MLA_CHAINS = 8
SWA_QB = 8
MIN_CHUNKS = 2 * TM // CHUNK
VMEM_LIMIT = 48 * 1024 * 1024

P_UF = 0
P_CQ = 256
P_CKV = 512
P_QS = 640
P_KS = 1024
P_VS = 1152
P_KR = 1280
P_TOT = 1408
SWA_HEAD_ORDER = (0, 3, 1, 4, 2, 5)


def _dot(a, b):
    return jnp.dot(a, b, preferred_element_type=F32)


def _dot_nt(a, b):
    return lax.dot_general(a, b, (((1,), (1,)), ((), ())), preferred_element_type=F32)


def _rms(x, n):
    return x * lax.rsqrt(jnp.sum(x * x, axis=-1, keepdims=True) / n + EPS)


def _rope(x, c, sa, sb, half):
    n = x.shape[-1]
    return x * c + pltpu.roll(x, n - half, 1) * sa + pltpu.roll(x, half, 1) * sb


def _cparams(sem):
    return pltpu.CompilerParams(dimension_semantics=sem, vmem_limit_bytes=VMEM_LIMIT)


def _adaln_kernel(c_ref, w_ref, b_ref, o_ref):
    c = c_ref[...]
    sc = c * jax.nn.sigmoid(c)
    w = w_ref[...]
    s_hi = sc.astype(BF16)
    s_lo = (sc - s_hi.astype(F32)).astype(BF16)
    w_hi = w.astype(BF16)
    w_lo = (w - w_hi.astype(F32)).astype(BF16)
    o_ref[...] = _dot(s_hi, w_hi) + (_dot(s_lo, w_hi) + _dot(s_hi, w_lo)) + b_ref[...]


def _adaln(cvec, ada_w, ada_b):
    L, D, N6 = ada_w.shape
    R = cvec.shape[0]
    bn = 512
    return pl.pallas_call(
        _adaln_kernel,
        grid=(L, N6 // bn),
        in_specs=[pl.BlockSpec((R, D), lambda l, j: (0, 0)),
                  pl.BlockSpec((None, D, bn), lambda l, j: (l, 0, j)),
                  pl.BlockSpec((None, 1, bn), lambda l, j: (l, 0, j))],
        out_specs=pl.BlockSpec((None, R, bn), lambda l, j: (l, 0, j)),
        out_shape=jax.ShapeDtypeStruct((L, R, N6), F32),
        compiler_params=_cparams(("arbitrary", "arbitrary")),
        name="adaln",
    )(cvec, ada_w, ada_b.reshape(L, 1, N6))


def _prep_kernel(x_ref, mod_ref, n1g_ref, win_ref, cqg_ref, ckvg_ref, wuq_ref, wuk_ref, wuv_ref,
                 mqg_ref, mkg_ref, sqg_ref, skg_ref, cm_ref, sam_ref, sbm_ref, cs_ref, sas_ref, sbs_ref,
                 dft_ref, pp_ref, qm_ref, km_ref, vm_ref, qs_ref, ks_ref, vs_ref):
    m = mod_ref[...]
    lo = lax.broadcasted_iota(jnp.int32, (1, LANES), 1) < SWA_HEAD_DIM
    slot_lane = jnp.bitwise_and(lax.broadcasted_iota(jnp.int32, (1, MLA_HEADS * HEAD_SLOT), 1), HEAD_SLOT - 1)
    vone = jnp.where(slot_lane == MLA_V, 1.0, 0.0)

    def head_norm(slab, g):
        sq = slab * slab
        s_lo = jnp.sum(jnp.where(lo, sq, 0.0), axis=-1, keepdims=True)
        s_hi = jnp.sum(jnp.where(lo, 0.0, sq), axis=-1, keepdims=True)
        r = jnp.where(lo, lax.rsqrt(s_lo * (1.0 / SWA_HEAD_DIM) + EPS),
                      lax.rsqrt(s_hi * (1.0 / SWA_HEAD_DIM) + EPS))
        return slab * r * g

    rows = PREP_CHAIN_ROWS
    for ch in range(x_ref.shape[0] // rows):
        rs = slice(ch * rows, (ch + 1) * rows)
        x = x_ref[rs, :]
        h = _rms(x, D_MODEL) * n1g_ref[...] * (1.0 + m[1:2]) + m[0:1]
        p = _dot(h.astype(BF16), win_ref[...])

        u = p[:, P_UF:P_UF + D_FNET].astype(BF16)
        pp_ref[rs, :] = _dot(u, dft_ref[...]).astype(BF16)

        cm, sam, sbm = cm_ref[rs, :], sam_ref[rs, :], sbm_ref[rs, :]
        cs, sas, sbs = cs_ref[rs, :], sas_ref[rs, :], sbs_ref[rs, :]

        cq = _rms(p[:, P_CQ:P_CQ + MLA_Q_RANK], MLA_Q_RANK) * cqg_ref[...]
        qraw = _dot(cq.astype(BF16), wuq_ref[...])
        mqg = mqg_ref[...]
        for hh in range(MLA_HEADS):
            sl = slice(hh * HEAD_SLOT, (hh + 1) * HEAD_SLOT)
            qn = _rms(qraw[:, sl], MLA_QK) * mqg
            qm_ref[rs, sl] = (_rope(qn, cm, sam, sbm, MLA_ROPE // 4) * (MLA_QK ** -0.5)).astype(BF16)

        ckv = (_rms(p[:, P_CKV:P_CKV + MLA_KV_RANK], MLA_KV_RANK) * ckvg_ref[...]).astype(BF16)
        knope = _dot(ckv, wuk_ref[...])
        vm_ref[rs, :] = (_dot(ckv, wuv_ref[...]) + vone).astype(BF16)
        kr = p[:, P_KR:P_KR + LANES]
        mkg = mkg_ref[...]
        kr_ss = jnp.sum(kr * kr, axis=-1, keepdims=True)
        kr_rot = _rope(kr * mkg, cm, sam, sbm, MLA_ROPE // 4)
        for hh in range(MLA_HEADS):
            sl = slice(hh * HEAD_SLOT, (hh + 1) * HEAD_SLOT)
            kn_h = knope[:, sl]
            r = lax.rsqrt((jnp.sum(kn_h * kn_h, axis=-1, keepdims=True) + kr_ss) / MLA_QK + EPS)
            km_ref[rs, sl] = ((kn_h * mkg + kr_rot) * r).astype(BF16)

        sqg = sqg_ref[...]
        for s in range(SWA_HEADS // 2):
            sl = slice(P_QS + s * LANES, P_QS + (s + 1) * LANES)
            qn = head_norm(p[:, sl], sqg)
            qs_ref[rs, s * LANES:(s + 1) * LANES] = (
                _rope(qn, cs, sas, sbs, SWA_HEAD_DIM // 4) * (SWA_HEAD_DIM ** -0.5)).astype(BF16)
        kn = head_norm(p[:, P_KS:P_KS + LANES], skg_ref[...])
        ks_ref[rs, :] = _rope(kn, cs, sas, sbs, SWA_HEAD_DIM // 4).astype(BF16)
        vs_ref[rs, :] = p[:, P_VS:P_VS + LANES].astype(BF16)


def _prep(xc, mod, n1g, wl, tabs, n_latent, seq, tm):
    T = xc.shape[0]
    nt = T // tm
    nx_tiles = n_latent // tm
    tiles_per_batch = seq // tm

    def bidx(i):
        return jnp.where(i < nx_tiles, i // tiles_per_batch, mod.shape[0] - 1)

    def ridx(i):
        return jnp.where(i < nx_tiles, i % tiles_per_batch, tiles_per_batch)

    def full(a):
        return pl.BlockSpec(a.shape, lambda i: (0,) * a.ndim)

    tab_spec = pl.BlockSpec((tm, LANES), lambda i: (ridx(i), 0))
    row = lambda w: pl.BlockSpec((tm, w), lambda i: (i, 0))
    consts = [n1g, wl["w_in"], wl["cq_g"], wl["ckv_g"], wl["w_uq"], wl["w_uk"], wl["w_uv"],
              wl["mq_g"], wl["mk_g"], wl["sq_g"], wl["sk_g"]]
    outs = [("pp", 2 * D_FNET), ("qm", MLA_HEADS * HEAD_SLOT), ("km", MLA_HEADS * HEAD_SLOT),
            ("vm", MLA_HEADS * HEAD_SLOT), ("qs", SWA_HEADS * SWA_HEAD_DIM), ("ks", LANES), ("vs", LANES)]
    res = pl.pallas_call(
        _prep_kernel,
        grid=(nt,),
        in_specs=[row(D_MODEL), pl.BlockSpec((None, 8, D_MODEL), lambda i: (bidx(i), 0, 0))]
                 + [full(a) for a in consts] + [tab_spec] * 6 + [full(tabs["dft64"])],
        out_specs=[row(w) for _, w in outs],
        out_shape=[jax.ShapeDtypeStruct((T, w), BF16) for _, w in outs],
        compiler_params=_cparams(("arbitrary",)),
        name="prep",
    )(xc, mod, *consts, tabs["cm"], tabs["sam"], tabs["sbm"], tabs["cs"], tabs["sas"], tabs["sbs"],
      tabs["dft64"])
    return dict(zip([n for n, _ in outs], res))


def _mla_kernel(*refs, with_x):
    if with_x:
        q_ref, kx_ref, kc_ref, vx_ref, vc_ref, o_ref = refs
    else:
        q_ref, kc_ref, vc_ref, o_ref = refs
    lane = lax.broadcasted_iota(jnp.int32, (1, LANES), 1)
    rows = q_ref.shape[0] // MLA_CHAINS if with_x else q_ref.shape[0]
    for r0 in range(0, q_ref.shape[0], rows):
        rs = slice(r0, r0 + rows)
        outs = []
        for hh in range(2):
            sl = slice(hh * HEAD_SLOT, (hh + 1) * HEAD_SLOT)
            q = q_ref[rs, sl]
            sc = _dot_nt(q, kc_ref[:, sl])
            m = jnp.max(sc, axis=-1, keepdims=True)
            if with_x:
                sx = _dot_nt(q, kx_ref[:, sl])
                m = jnp.maximum(m, jnp.max(sx, axis=-1, keepdims=True))
                px = jnp.exp(sx - m)
            pc = jnp.exp(sc - m)
            o = _dot(pc.astype(BF16), vc_ref[:, sl])
            if with_x:
                o = o + _dot(px.astype(BF16), vx_ref[:, sl])
            outs.append(o / o[:, MLA_V:MLA_V + 1])
        o_ref[rs, :] = jnp.where(lane < MLA_V, outs[0], pltpu.roll(outs[1], MLA_V, 1)).astype(BF16)


def _ctx_rows_kernel(kernel_fn, *refs, **kw):
    kernel_fn(*refs[:-2], refs[-1], **kw)


def _mla_attend(qm, km, vm, B, S, C, prev=None):
    T = qm.shape[0]
    npair = MLA_HEADS // 2
    with_x = prev is None
    if with_x:
        tq = 2048
        nq = S // tq
        qmap = lambda b, p, i: (b * nq + i, p)
        in_specs = [pl.BlockSpec((tq, 2 * HEAD_SLOT), qmap),
                    pl.BlockSpec((S, 2 * HEAD_SLOT), lambda b, p, i: (b, p)),
                    pl.BlockSpec((C, 2 * HEAD_SLOT), lambda b, p, i: (B * S // C + b, p)),
                    pl.BlockSpec((S, 2 * HEAD_SLOT), lambda b, p, i: (b, p)),
                    pl.BlockSpec((C, 2 * HEAD_SLOT), lambda b, p, i: (B * S // C + b, p))]
        args = (qm, km, km, vm, vm)
        body = functools.partial(_mla_kernel, with_x=True)
        aliases = {}
    else:
        tq = C
        nq = 1
        qmap = lambda b, p, i: (B * S // C + b, p)
        in_specs = [pl.BlockSpec((tq, 2 * HEAD_SLOT), qmap),
                    pl.BlockSpec((C, 2 * HEAD_SLOT), qmap),
                    pl.BlockSpec((C, 2 * HEAD_SLOT), qmap),
                    pl.BlockSpec(memory_space=pl.ANY)]
        args = (qm, km, vm, prev)
        body = functools.partial(_ctx_rows_kernel, _mla_kernel, with_x=False)
        aliases = {3: 0}
    return pl.pallas_call(
        body,
        grid=(B, npair, nq),
        in_specs=in_specs,
        out_specs=pl.BlockSpec((tq, LANES), qmap),
        out_shape=jax.ShapeDtypeStruct((T, MLA_HEADS * MLA_V), BF16),
        input_output_aliases=aliases,
        compiler_params=_cparams(("arbitrary",) * 3),
        name="mla_x" if with_x else "mla_c",
    )(*args)


def _swa_kernel(*refs, with_x, nblk, qb):
    if with_x:
        sink_ref, q_ref, kp_ref, ko_ref, kn_ref, kc_ref, vp_ref, vo_ref, vn_ref, vc_ref, o_ref = refs
    else:
        sink_ref, q_ref, kc_ref, vc_ref, o_ref = refs
    n0 = pl.program_id(1) * qb
    lane = lax.broadcasted_iota(jnp.int32, (1, LANES), 1)
    lo = lane < SWA_HEAD_DIM
    row2 = lax.broadcasted_iota(jnp.int32, (2 * BLOCK, 1), 0)
    kc, vc = kc_ref[...], vc_ref[...]
    if with_x:
        kloc = jnp.concatenate([kp_ref[...], ko_ref[...], kn_ref[...]], axis=0)
        vloc = jnp.concatenate([vp_ref[...], vo_ref[...], vn_ref[...]], axis=0)
        nk = 3 * BLOCK + kc.shape[0]
        qi = lax.broadcasted_iota(jnp.int32, (2 * BLOCK, nk), 0) % BLOCK
        kj = lax.broadcasted_iota(jnp.int32, (2 * BLOCK, nk), 1)
        out_prev = jnp.logical_and(kj < BLOCK, kj - qi < BLOCK - WINDOW)
        out_next = jnp.logical_and(jnp.logical_and(kj >= 2 * BLOCK, kj < 3 * BLOCK), kj - qi > BLOCK + WINDOW)
        in_prev = kj < BLOCK
        in_next = jnp.logical_and(kj >= 2 * BLOCK, kj < 3 * BLOCK)
    for i in range(qb):
        rs = slice(i * BLOCK, (i + 1) * BLOCK)
        if with_x:
            kall = jnp.concatenate([kloc[i * BLOCK:(i + 3) * BLOCK], kc], axis=0)
            vall = jnp.concatenate([vloc[i * BLOCK:(i + 3) * BLOCK], vc], axis=0)
            n = n0 + i
            bad = jnp.logical_or(
                jnp.logical_or(out_prev, jnp.logical_and(in_prev, n == 0)),
                jnp.logical_or(out_next, jnp.logical_and(in_next, n == nblk - 1)))
        else:
            kall, vall = kc, vc
        for s in range(SWA_HEADS // 2):
            q = q_ref[rs, s * LANES:(s + 1) * LANES]
            zero = jnp.zeros_like(q)
            q2 = jnp.concatenate([jnp.where(lo, q, zero), jnp.where(lo, zero, q)], axis=0)
            sink = jnp.where(row2 < BLOCK, sink_ref[0, s], sink_ref[0, SWA_HEADS // 2 + s])
            sc = _dot_nt(q2, kall)
            if with_x:
                sc = jnp.where(bad, -jnp.inf, sc)
            m = jnp.maximum(jnp.max(sc, axis=-1, keepdims=True), sink)
            p = jnp.exp(sc - m)
            l = jnp.sum(p, axis=-1, keepdims=True) + jnp.exp(sink - m)
            o = _dot(p.astype(BF16), vall) / l
            o_ref[rs, s * LANES:(s + 1) * LANES] = jnp.where(lo, o[:BLOCK], o[BLOCK:]).astype(BF16)


def _swa_attend(sink, qs, ks, vs, B, S, C, prev=None):
    T = qs.shape[0]
    cb = B * S // C
    with_x = prev is None
    if with_x:
        nblk = S // BLOCK
        qb = SWA_QB
        nstep = nblk // qb
        qmap = lambda b, j: (b * nstep + j, 0)
        pmap = lambda b, j: (b * nblk + jnp.maximum(j * qb - 1, 0), 0)
        nmap = lambda b, j: (b * nblk + jnp.minimum(j * qb + qb, nblk - 1), 0)
        cmap = lambda b, j: (cb + b, 0)
        kv1 = lambda mp: pl.BlockSpec((BLOCK, LANES), mp)
        kvq = pl.BlockSpec((qb * BLOCK, LANES), qmap)
        cspec = pl.BlockSpec((C, LANES), cmap)
        in_specs = [pl.BlockSpec(memory_space=pltpu.SMEM),
                    pl.BlockSpec((qb * BLOCK, SWA_HEADS * SWA_HEAD_DIM), qmap),
                    kv1(pmap), kvq, kv1(nmap), cspec, kv1(pmap), kvq, kv1(nmap), cspec]
        args = (sink, qs, ks, ks, ks, ks, vs, vs, vs, vs)
        body = functools.partial(_swa_kernel, with_x=True, nblk=nblk, qb=qb)
        aliases = {}
    else:
        nblk = C // BLOCK
        qb = nblk
        nstep = 1
        qmap = lambda b, j: (cb + b, 0)
        cspec = pl.BlockSpec((C, LANES), qmap)
        in_specs = [pl.BlockSpec(memory_space=pltpu.SMEM),
                    pl.BlockSpec((C, SWA_HEADS * SWA_HEAD_DIM), qmap), cspec, cspec,
                    pl.BlockSpec(memory_space=pl.ANY)]
        args = (sink, qs, ks, vs, prev)
        body = functools.partial(_ctx_rows_kernel, _swa_kernel, with_x=False, nblk=nblk, qb=qb)
        aliases = {4: 0}
    return pl.pallas_call(
        body,
        grid=(B, nstep),
        in_specs=in_specs,
        out_specs=pl.BlockSpec((qb * BLOCK, SWA_HEADS * SWA_HEAD_DIM), qmap),
        out_shape=jax.ShapeDtypeStruct((T, SWA_HEADS * SWA_HEAD_DIM), BF16),
        input_output_aliases=aliases,
        compiler_params=_cparams(("arbitrary",) * 2),
        name="swa_x" if with_x else "swa_c",
    )(*args)


def _fourier_kernel(c_ref, s_ref, pp_ref, w_ref, o_ref):
    f = _dot(c_ref[...], pp_ref[:, 0:D_FNET]) - _dot(s_ref[...], pp_ref[:, D_FNET:2 * D_FNET])
    o_ref[...] = _dot(f.astype(BF16), w_ref[...]).astype(BF16)


def _fourier(cmat, smat, pp, wblk, B, N, row0, prev=None):
    T = pp.shape[0]
    tq = min(512, N)
    nr = N // tq
    b0 = row0 // N
    o0 = row0 // tq
    in_specs = [pl.BlockSpec((tq, N), lambda r, b: (r, 0)),
                pl.BlockSpec((tq, N), lambda r, b: (r, 0)),
                pl.BlockSpec((N, 2 * D_FNET), lambda r, b: (b0 + b, 0)),
                pl.BlockSpec((D_FNET, D_FNET), lambda r, b: (0, 0))]
    args = (cmat, smat, pp, wblk)
    if prev is None:
        body = _fourier_kernel
        aliases = {}
    else:
        in_specs.append(pl.BlockSpec(memory_space=pl.ANY))
        args = args + (prev,)
        body = functools.partial(_ctx_rows_kernel, _fourier_kernel)
        aliases = {4: 0}
    return pl.pallas_call(
        body,
        grid=(nr, B),
        in_specs=in_specs,
        out_specs=pl.BlockSpec((tq, D_FNET), lambda r, b: (o0 + b * nr + r, 0)),
        out_shape=jax.ShapeDtypeStruct((T, D_FNET), BF16),
        input_output_aliases=aliases,
        compiler_params=_cparams(("arbitrary",) * 2),
        name="fourier_%d" % N,
    )(*args)


def _route_rows(sel, aff):
    G, K = N_EXPERT_GROUPS, EXPERTS_PER_GROUP
    gscore = []
    for g in range(G):
        a = sel[g * K:(g + 1) * K]
        best = None
        for i in range(K):
            for j in range(i + 1, K):
                v = a[i] + a[j]
                best = v if best is None else jnp.maximum(best, v)
        gscore.append(best)
    gb = jnp.zeros_like(gscore[0])
    gbest = gscore[0]
    for g in range(1, G):
        upd = gscore[g] > gbest
        gb = jnp.where(upd, float(g), gb)
        gbest = jnp.where(upd, gscore[g], gbest)
    cs, ca = [], []
    for i in range(K):
        c, a = sel[i], aff[i]
        for g in range(1, G):
            pick = gb == float(g)
            c = jnp.where(pick, sel[g * K + i], c)
            a = jnp.where(pick, aff[g * K + i], a)
        cs.append(c)
        ca.append(a)

    def first_max(vals):
        bi = jnp.zeros_like(vals[0])
        bv = vals[0]
        for i in range(1, K):
            upd = vals[i] > bv
            bi = jnp.where(upd, float(i), bi)
            bv = jnp.where(upd, vals[i], bv)
        return bi

    i1 = first_max(cs)
    cs2 = [jnp.where(i1 == float(i), -jnp.inf, cs[i]) for i in range(K)]
    i2 = first_max(cs2)
    a1 = sum(jnp.where(i1 == float(i), ca[i], 0.0) for i in range(K))
    a2 = sum(jnp.where(i2 == float(i), ca[i], 0.0) for i in range(K))
    den = a1 + a2
    return gb * K + i1, gb * K + i2, a1 / den, a2 / den


def _post_kernel(x_ref, mod_ref, fo_ref, ml_ref, sw_ref, wof_ref, wom_ref, wos_ref, n2g_ref,
                 rwh_ref, rwl_ref, rb_ref, tri_ref, ones_ref, xn_ref, h2_ref, rc_ref, rr_ref, cnt_ref):
    m = mod_ref[...]
    parts = []
    for r0 in range(0, TM, POST_CHAIN_ROWS):
        rs = slice(r0, r0 + POST_CHAIN_ROWS)
        mix = (_dot(fo_ref[rs, :], wof_ref[...]) + _dot(ml_ref[rs, :], wom_ref[...])
               + _dot(sw_ref[rs, :], wos_ref[...]))
        xn = x_ref[rs, :] + m[2:3] * mix
        xn_ref[rs, :] = xn
        h2 = _rms(xn, D_MODEL) * n2g_ref[...] * (1.0 + m[4:5]) + m[3:4]
        h2_ref[rs, :] = h2.astype(BF16)
        hh = h2.astype(BF16)
        hl = (h2 - hh.astype(F32)).astype(BF16)
        parts.append(_dot(hh, rwh_ref[...]) + (_dot(hl, rwh_ref[...]) + _dot(hh, rwl_ref[...])))
    logits = jnp.concatenate(parts, axis=0)
    lt = logits.T[0:N_EXPERTS, :]
    aff_t = jax.nn.sigmoid(lt)
    sel_t = aff_t + rb_ref[...]
    sel = [sel_t[e:e + 1, :] for e in range(N_EXPERTS)]
    aff = [aff_t[e:e + 1, :] for e in range(N_EXPERTS)]
    e1, e2, w1, w2 = _route_rows(sel, aff)

    eio = lax.broadcasted_iota(jnp.int32, (N_EXPERTS, TM), 0).astype(F32)
    oh = jnp.concatenate([jnp.where(eio == e1, 1.0, 0.0), jnp.where(eio == e2, 1.0, 0.0)], axis=1)
    ohb = oh.astype(BF16)
    rank = _dot(ohb, tri_ref[...])
    cnt = _dot(ohb, ones_ref[...]).astype(jnp.int32)
    cnt8 = jnp.left_shift(jnp.right_shift(cnt + (CHUNK - 1), CHUNK_LOG2), CHUNK_LOG2)
    cnt_ref[...] = cnt8
    cnt8f = cnt8.astype(F32)
    off = jnp.zeros((1, 1), F32)
    slot = jnp.zeros((1, 2 * TM), F32)
    for e in range(N_EXPERTS):
        slot = slot + oh[e:e + 1, :] * (off + rank[e:e + 1, :])
        off = off + cnt8f[e:e + 1, 0:1]
    s0, s1 = slot[:, :TM], slot[:, TM:]

    sub = lax.broadcasted_iota(jnp.int32, (8, TM), 0)
    rr_ref[...] = jnp.where(sub == 0, s0, jnp.where(sub == 1, s1, 0.0))
    blk = jnp.where(sub == 0, e1, jnp.where(sub == 1, e2, jnp.where(sub == 2, w1, jnp.where(
        sub == 3, w2, jnp.where(sub == 4, s0, jnp.where(sub == 5, s1, 0.0))))))
    rows = jnp.concatenate([blk, jnp.zeros((LANES - 8, TM), F32)], axis=0)
    rc_ref[...] = rows.T


def _post(xc, mod, fo, ml, sw, wl, n2g, rw_hi, rw_lo, rb_col, nt, nx_tiles, tiles_per_batch):
    rows = nt * TM

    def bidx(i):
        return jnp.where(i < nx_tiles, i // tiles_per_batch, mod.shape[0] - 1)

    def full(a):
        return pl.BlockSpec(a.shape, lambda i: (0,) * a.ndim)

    row = lambda w: pl.BlockSpec((TM, w), lambda i: (i, 0))
    pair = np.arange(2 * TM)
    tri = jnp.asarray(pair[:, None] < pair[None, :], BF16)
    ones = jnp.ones((2 * TM, LANES), BF16)
    consts = [wl["wo_f"], wl["wo_m"], wl["wo_s"], n2g, rw_hi, rw_lo, rb_col, tri, ones]
    return pl.pallas_call(
        _post_kernel,
        grid=(nt,),
        in_specs=[row(D_MODEL), pl.BlockSpec((None, 8, D_MODEL), lambda i: (bidx(i), 0, 0)),
                  row(D_FNET), row(MLA_HEADS * MLA_V), row(SWA_HEADS * SWA_HEAD_DIM)]
                 + [full(a) for a in consts],
        out_specs=[row(D_MODEL), row(D_MODEL), row(LANES),
                   pl.BlockSpec((None, 8, TM), lambda i: (i, 0, 0)),
                   pl.BlockSpec((None, N_EXPERTS, LANES), lambda i: (i, 0, 0))],
        out_shape=[jax.ShapeDtypeStruct((rows, D_MODEL), F32),
                   jax.ShapeDtypeStruct((rows, D_MODEL), BF16),
                   jax.ShapeDtypeStruct((rows, LANES), F32),
                   jax.ShapeDtypeStruct((nt, 8, TM), F32),
                   jax.ShapeDtypeStruct((nt, N_EXPERTS, LANES), jnp.int32)],
        compiler_params=_cparams(("arbitrary",)),
        name="post",
    )(xc, mod, fo, ml, sw, *consts)


def _moe_tables(cnt8, te):
    nt = cnt8.shape[0]
    tile_prefix = jnp.cumsum(cnt8, axis=0) - cnt8
    tot = jnp.sum(cnt8, axis=0)
    tot_e = ((tot + te - 1) // te) * te
    goff = jnp.cumsum(tot_e) - tot_e
    dbase = goff[None, :] + tile_prefix
    nch = cnt8 // CHUNK
    cum = jnp.cumsum(nch, axis=1)
    k = jnp.arange(MAX_CHUNKS, dtype=jnp.int32)[None, :, None]
    owns = jnp.logical_and(k >= (cum - nch)[:, None, :], k < cum[:, None, :])
    dst = jnp.sum(jnp.where(owns, dbase[:, None, :] + CHUNK * (k - (cum - nch)[:, None, :]), 0), axis=-1)
    nchunks = cum[:, -1]
    padch = (tot_e - tot) // CHUNK
    cump = jnp.cumsum(padch)
    kp = jnp.arange(N_EXPERTS * (te // CHUNK), dtype=jnp.int32)[:, None]
    pown = jnp.logical_and(kp >= (cump - padch)[None, :], kp < cump[None, :])
    pdst = jnp.sum(jnp.where(pown, (goff + tot)[None, :] + CHUNK * (kp - (cump - padch)[None, :]), 0), axis=-1)
    npad = cump[-1]
    ntile_cum = jnp.cumsum(tot_e // te)
    nact = ntile_cum[-1]
    return dict(dst=dst.reshape(-1).astype(jnp.int32), nchunks=nchunks.astype(jnp.int32),
                pdst=pdst.astype(jnp.int32), npad=npad.reshape(1).astype(jnp.int32),
                ntile_cum=ntile_cum.astype(jnp.int32), nact=nact.reshape(1).astype(jnp.int32))


def _dispatch_kernel(dst_ref, nch_ref, pdst_ref, npad_ref, h_ref, rr_ref, xs_ref, sbuf, zbuf, sem, zsem):
    i = pl.program_id(0)
    nt = pl.num_programs(0)
    slot = i % 2

    def chunk_copy(sl, k, d):
        return pltpu.make_async_copy(
            sbuf.at[sl, pl.ds(pl.multiple_of(k * CHUNK, CHUNK), CHUNK), :],
            xs_ref.at[pl.ds(pl.multiple_of(d, CHUNK), CHUNK), :], sem.at[sl])

    def wait_tile(t, sl):
        n = nch_ref[t] * CHUNK
        pltpu.make_async_copy(sbuf.at[sl, pl.ds(0, n), :], xs_ref.at[pl.ds(0, n), :], sem.at[sl]).wait()

    def pad_copy(d):
        return pltpu.make_async_copy(zbuf, xs_ref.at[pl.ds(pl.multiple_of(d, CHUNK), CHUNK), :], zsem)

    @pl.when(i == 0)
    def _():
        zbuf[...] = jnp.zeros_like(zbuf)

        def start(k, c):
            pad_copy(pdst_ref[k]).start()
            return c
        lax.fori_loop(0, npad_ref[0], start, 0)

        def wait(k, c):
            pad_copy(0).wait()
            return c
        lax.fori_loop(0, npad_ref[0], wait, 0)

    @pl.when(i >= 2)
    def _():
        wait_tile(i - 2, slot)

    rr = rr_ref[...]
    sio = lax.broadcasted_iota(jnp.int32, (NSLOT, TM), 0).astype(F32)
    psel = jnp.where(jnp.logical_or(sio == rr[0:1, :], sio == rr[1:2, :]), 1.0, 0.0).astype(BF16)
    sbuf[slot] = _dot(psel, h_ref[...])

    for k in range(MIN_CHUNKS):
        chunk_copy(slot, k, dst_ref[i * MAX_CHUNKS + k]).start(priority=k % 2)

    def issue(k, c):
        chunk_copy(slot, k, dst_ref[i * MAX_CHUNKS + k]).start()
        return c
    lax.fori_loop(MIN_CHUNKS, nch_ref[i], issue, 0)

    @pl.when(i == nt - 1)
    def _():
        wait_tile(i, slot)

        @pl.when(i >= 1)
        def _():
            wait_tile(i - 1, 1 - slot)


def _dispatch(h2, rr, tb, rows_sorted):
    nt = rr.shape[0]
    return pl.pallas_call(
        _dispatch_kernel,
        grid_spec=pltpu.PrefetchScalarGridSpec(
            num_scalar_prefetch=4,
            grid=(nt,),
            in_specs=[pl.BlockSpec((TM, D_MODEL), lambda i, *_: (i, 0)),
                      pl.BlockSpec((None, 8, TM), lambda i, *_: (i, 0, 0))],
            out_specs=pl.BlockSpec(memory_space=pl.ANY),
            scratch_shapes=[pltpu.VMEM((2, NSLOT, D_MODEL), F32), pltpu.VMEM((CHUNK, D_MODEL), F32),
                            pltpu.SemaphoreType.DMA((2,)), pltpu.SemaphoreType.DMA(())]),
        out_shape=jax.ShapeDtypeStruct((rows_sorted, D_MODEL), F32),
        compiler_params=_cparams(("arbitrary",)),
        name="moe_dispatch",
    )(tb["dst"], tb["nchunks"], tb["pdst"], tb["npad"], h2, rr)


def _expert_kernel(te_ref, na_ref, x_ref, wg_ref, wu_ref, wd_ref, o_ref, wgb, wub, wdb):
    j = pl.program_id(0)
    active = j < na_ref[0]
    fresh = jnp.logical_or(j == 0, te_ref[j] != te_ref[jnp.maximum(j - 1, 0)])

    @pl.when(jnp.logical_and(active, fresh))
    def _():
        wgb[...] = wg_ref[...].astype(BF16)
        wub[...] = wu_ref[...].astype(BF16)
        wdb[...] = wd_ref[...].astype(BF16)

    @pl.when(active)
    def _():
        x = x_ref[...].astype(BF16)
        a = _dot(x, wgb[...])
        a = a * jax.nn.sigmoid(a) * _dot(x, wub[...])
        o_ref[...] = _dot(a.astype(BF16), wdb[...])


def _experts(xs, wg, wu, wd, layer, tb, te):
    nte = xs.shape[0] // te
    jj = jnp.minimum(jnp.arange(nte, dtype=jnp.int32), tb["nact"][0] - 1)
    tile_e = jnp.sum(jj[:, None] >= tb["ntile_cum"][None, :], axis=-1).astype(jnp.int32)

    def tmap(j, te_ref, na):
        return (jnp.minimum(j, na[0] - 1), 0)

    def wmap(j, te_ref, na):
        return (layer, te_ref[j], 0, 0)

    return pl.pallas_call(
        _expert_kernel,
        grid_spec=pltpu.PrefetchScalarGridSpec(
            num_scalar_prefetch=2,
            grid=(nte,),
            in_specs=[pl.BlockSpec((te, D_MODEL), tmap),
                      pl.BlockSpec((None, None, D_MODEL, D_EXPERT), wmap),
                      pl.BlockSpec((None, None, D_MODEL, D_EXPERT), wmap),
                      pl.BlockSpec((None, None, D_EXPERT, D_MODEL), wmap)],
            out_specs=pl.BlockSpec((te, D_MODEL), tmap),
            scratch_shapes=[pltpu.VMEM((D_MODEL, D_EXPERT), BF16), pltpu.VMEM((D_MODEL, D_EXPERT), BF16),
                            pltpu.VMEM((D_EXPERT, D_MODEL), BF16)]),
        out_shape=jax.ShapeDtypeStruct(xs.shape, F32),
        compiler_params=_cparams(("arbitrary",)),
        name="moe_experts",
    )(tile_e, tb["nact"], xs, wg, wu, wd)


def _combine_kernel(dst_ref, nch_ref, xn_ref, mod_ref, rc_ref, ys_ref, o_ref, gbuf, sem):
    i = pl.program_id(0)
    nt = pl.num_programs(0)
    slot = i % 2

    def chunk_copy(sl, k, d):
        return pltpu.make_async_copy(
            ys_ref.at[pl.ds(pl.multiple_of(d, CHUNK), CHUNK), :],
            gbuf.at[sl, pl.ds(pl.multiple_of(k * CHUNK, CHUNK), CHUNK), :], sem.at[sl])

    def issue_tile(t, sl):
        for k in range(MIN_CHUNKS):
            chunk_copy(sl, k, dst_ref[t * MAX_CHUNKS + k]).start(priority=k % 2)

        def body(k, c):
            chunk_copy(sl, k, dst_ref[t * MAX_CHUNKS + k]).start()
            return c
        lax.fori_loop(MIN_CHUNKS, nch_ref[t], body, 0)

    @pl.when(i == 0)
    def _():
        gbuf[...] = jnp.zeros_like(gbuf)
        issue_tile(0, 0)

    @pl.when(i + 1 < nt)
    def _():
        issue_tile(i + 1, 1 - slot)

    n = nch_ref[i] * CHUNK
    pltpu.make_async_copy(ys_ref.at[pl.ds(0, n), :], gbuf.at[slot, pl.ds(0, n), :], sem.at[slot]).wait()

    g = gbuf[slot].astype(BF16)
    rc = rc_ref[...]
    lio = lax.broadcasted_iota(jnp.int32, (TM, NSLOT), 1).astype(F32)
    p0 = jnp.where(lio == rc[:, 4:5], 1.0, 0.0).astype(BF16)
    p1 = jnp.where(lio == rc[:, 5:6], 1.0, 0.0).astype(BF16)
    y = rc[:, 2:3] * _dot(p0, g) + rc[:, 3:4] * _dot(p1, g)
    o_ref[...] = xn_ref[...] + mod_ref[5:6, :] * y


def _combine(xn, mod, rc, ys, tb, nx_tiles, tiles_per_batch):
    rows = xn.shape[0]
    nt = rows // TM

    def bidx(i, *_):
        return (jnp.where(i < nx_tiles, i // tiles_per_batch, mod.shape[0] - 1), 0, 0)

    return pl.pallas_call(
        _combine_kernel,
        grid_spec=pltpu.PrefetchScalarGridSpec(
            num_scalar_prefetch=2,
            grid=(nt,),
            in_specs=[pl.BlockSpec((TM, D_MODEL), lambda i, *_: (i, 0)),
                      pl.BlockSpec((None, 8, D_MODEL), bidx),
                      pl.BlockSpec((TM, LANES), lambda i, *_: (i, 0)),
                      pl.BlockSpec(memory_space=pl.ANY)],
            out_specs=pl.BlockSpec((TM, D_MODEL), lambda i, *_: (i, 0)),
            scratch_shapes=[pltpu.VMEM((2, NSLOT, D_MODEL), F32), pltpu.SemaphoreType.DMA((2,))]),
        out_shape=jax.ShapeDtypeStruct((rows, D_MODEL), F32),
        compiler_params=_cparams(("arbitrary",)),
        name="moe_combine",
    )(tb["dst"], tb["nchunks"], xn, mod, rc, ys)


def _moe(h2, rc, rr, cnt, wg, wu, wd, layer, xn, mod, nx_tiles, tiles_per_batch):
    nt = rr.shape[0]
    te = TE
    max_rows = 2 * nt * TM + (CHUNK - 1) * N_EXPERTS * nt + N_EXPERTS * (te - CHUNK)
    rows_sorted = ((max_rows + te - 1) // te) * te
    tb = _moe_tables(cnt[:, :, 0], te)
    xs = _dispatch(h2, rr, tb, rows_sorted)
    ys = _experts(xs, wg, wu, wd, layer, tb, te)
    return _combine(xn, mod, rc, ys, tb, nx_tiles, tiles_per_batch)


def _rope_tables(S, C):
    t = jnp.arange(S)
    rows, cols = (t // GRID_W).astype(F32), (t % GRID_W).astype(F32)

    def axis_tabs(d_rot, lane0, width):
        d_axis = d_rot // 2
        inv = ROPE_THETA ** (-jnp.arange(0, d_axis, 2, dtype=F32) / d_axis)
        ar, ac = rows[:, None] * inv, cols[:, None] * inv
        ang = jnp.concatenate([ar, ar, ac, ac], axis=-1)
        q = d_rot // 4
        first = np.concatenate([np.ones(q), np.zeros(q), np.ones(q), np.zeros(q)]).astype(np.float32)
        pad = ((0, C), (lane0, width - lane0 - d_rot))
        cos = jnp.pad(jnp.cos(ang) - 1.0, pad) + 1.0
        sa = jnp.pad(-jnp.sin(ang) * first, pad)
        sb = jnp.pad(jnp.sin(ang) * (1.0 - first), pad)
        return cos, sa, sb

    cm, sam, sbm = axis_tabs(MLA_ROPE, MLA_NOPE, LANES)
    cs, sas, sbs = axis_tabs(SWA_HEAD_DIM, 0, SWA_HEAD_DIM)
    tile2 = lambda a: jnp.concatenate([a, a], axis=1)
    return dict(cm=cm, sam=sam, sbm=sbm, cs=tile2(cs), sas=tile2(sas), sbs=tile2(sbs))


_TWO_PI_HI = float(np.float32(2.0 * np.pi))
_TWO_PI_LO = float(np.float32(2.0 * np.pi - np.float64(np.float32(2.0 * np.pi))))


def _dft_mats(N):
    k = jnp.arange(N, dtype=jnp.int32)

    def cos_sin(rows, period):
        frac = ((rows[:, None] * k[None, :]) % period).astype(F32) / period
        ang = _TWO_PI_HI * frac + _TWO_PI_LO * frac
        return jnp.cos(ang), jnp.sin(ang)

    if N <= 4 * FNET_CH:
        return cos_sin(k, N)
    A = N // FNET_CH
    c1, s1 = cos_sin(jnp.arange(A, dtype=jnp.int32), A)
    c2, s2 = cos_sin(jnp.arange(FNET_CH, dtype=jnp.int32), N)
    c = c1[:, None, :] * c2[None, :, :] - s1[:, None, :] * s2[None, :, :]
    s = s1[:, None, :] * c2[None, :, :] + c1[:, None, :] * s2[None, :, :]
    return c.reshape(N, N), s.reshape(N, N)


def _dft64_blocks():
    c, s = _dft_mats(FNET_CH)
    eye = jnp.eye(FNET_GROUPS, dtype=F32)
    return jnp.concatenate([jnp.kron(eye, c), jnp.kron(eye, s)], axis=1)


def _layer_weights(l, w_in, fnet_w, mla_cq_g, mla_ckv_g, mla_w_uq, mla_w_uk, mla_w_uv, mla_q_g, mla_k_g,
                   swa_q_g, swa_k_g, swa_sink, w_out):
    D = D_MODEL
    wi = w_in[l]
    o_kr = D_FNET + MLA_Q_RANK + MLA_KV_RANK
    o_qs = o_kr + MLA_ROPE
    o_ks = o_qs + SWA_HEADS * SWA_HEAD_DIM
    o_vs = o_ks + SWA_KV_HEADS * SWA_HEAD_DIM
    order = np.array(SWA_HEAD_ORDER)
    w_qs = wi[:, o_qs:o_ks].reshape(D, SWA_HEADS, SWA_HEAD_DIM)[:, order].reshape(D, -1)
    z = lambda n: jnp.zeros((D, n), F32)
    win = jnp.concatenate([wi[:, :o_kr], w_qs, wi[:, o_ks:o_vs], wi[:, o_vs:],
                           z(MLA_NOPE), wi[:, o_kr:o_qs], z(LANES - MLA_QK)], axis=1)
    pad_slot = lambda w, d: jnp.pad(w.reshape(w.shape[0], MLA_HEADS, d),
                                    ((0, 0), (0, 0), (0, HEAD_SLOT - d))).reshape(w.shape[0], -1)
    wo = w_out[l]
    o_m = D_FNET
    o_s = D_FNET + MLA_HEADS * MLA_V
    wo_s = wo[o_s:].reshape(SWA_HEADS, SWA_HEAD_DIM, D)[order].reshape(-1, D)
    fw = fnet_w[l]
    wblk = jnp.zeros((D_FNET, D_FNET), F32)
    for g in range(FNET_GROUPS):
        wblk = wblk.at[g * FNET_CH:(g + 1) * FNET_CH, g * FNET_CH:(g + 1) * FNET_CH].set(fw[g])
    pad_g = lambda g: jnp.pad(g, (0, HEAD_SLOT - MLA_QK)).reshape(1, HEAD_SLOT)
    return dict(
        w_in=win.astype(BF16),
        cq_g=mla_cq_g[l].reshape(1, -1), ckv_g=mla_ckv_g[l].reshape(1, -1),
        w_uq=pad_slot(mla_w_uq[l], MLA_QK).astype(BF16),
        w_uk=pad_slot(mla_w_uk[l], MLA_NOPE).astype(BF16),
        w_uv=pad_slot(mla_w_uv[l], MLA_V).astype(BF16),
        mq_g=pad_g(mla_q_g[l]), mk_g=pad_g(mla_k_g[l]),
        sq_g=jnp.tile(swa_q_g[l], 2).reshape(1, LANES), sk_g=jnp.tile(swa_k_g[l], 2).reshape(1, LANES),
        sink=swa_sink[l].reshape(1, SWA_HEADS),
        wo_f=wo[:o_m].astype(BF16), wo_m=wo[o_m:o_s].astype(BF16), wo_s=wo_s.astype(BF16),
        fnet=wblk,
    )


def kernel(x, c, ctx, c_ctx, ada_w, ada_b, norm1_g, norm2_g, w_in, fnet_w, mla_cq_g, mla_ckv_g, mla_w_uq,
           mla_w_uk, mla_w_uv, mla_q_g, mla_k_g, swa_q_g, swa_k_g, swa_sink, w_out, router_w, router_b,
           exp_w_gate, exp_w_up, exp_w_down):
    B, S, D = x.shape
    C = ctx.shape[1]
    L = ada_w.shape[0]
    assert D == D_MODEL and S % 512 == 0 and C == TM and S % TM == 0
    nx_tiles = B * S // TM
    nt_all = nx_tiles + B * C // TM
    tiles_per_batch = S // TM

    prep_tm = PREP_TM if (B * C) % PREP_TM == 0 and S % PREP_TM == 0 else TM
    tabs = _rope_tables(S, prep_tm)
    tabs["dft64"] = _dft64_blocks().astype(BF16)
    dft = {n: tuple(m.astype(BF16) for m in _dft_mats(n)) for n in (S, C)}
    fscale = {n: lax.rsqrt(jnp.full((), n * FNET_CH, F32)) for n in (S, C)}

    nmod = 16
    cvec = jnp.concatenate([c, c_ctx[None, :], jnp.zeros((nmod - B - 1, D), F32)], axis=0)
    mod_all = _adaln(cvec, ada_w, ada_b)
    mod_all = mod_all[:, :B + 1].reshape(L, B + 1, 6, D)
    mod_all = jnp.pad(mod_all, ((0, 0), (0, 0), (0, 2), (0, 0)))

    rw = jnp.pad(router_w, ((0, 0), (0, LANES - N_EXPERTS)))
    rw_hi = rw.astype(BF16)
    rw_lo = (rw - rw_hi.astype(F32)).astype(BF16)
    rb_col = router_b.reshape(N_EXPERTS, 1)

    xc = jnp.concatenate([x.reshape(B * S, D), ctx.reshape(B * C, D)], axis=0)
    for l in range(L):
        last = l == L - 1
        wl = _layer_weights(l, w_in, fnet_w, mla_cq_g, mla_ckv_g, mla_w_uq, mla_w_uk, mla_w_uv, mla_q_g,
                            mla_k_g, swa_q_g, swa_k_g, swa_sink, w_out)
        mod = mod_all[l]
        pr = _prep(xc, mod, norm1_g[l].reshape(1, D), wl, tabs, B * S, S, prep_tm)
        fo = _fourier(dft[S][0], dft[S][1], pr["pp"], (wl["fnet"] * fscale[S]).astype(BF16), B, S, 0)
        ml = _mla_attend(pr["qm"], pr["km"], pr["vm"], B, S, C)
        sw = _swa_attend(wl["sink"], pr["qs"], pr["ks"], pr["vs"], B, S, C)
        if not last:
            fo = _fourier(dft[C][0], dft[C][1], pr["pp"], (wl["fnet"] * fscale[C]).astype(BF16), B, C, B * S,
                          prev=fo)
            ml = _mla_attend(pr["qm"], pr["km"], pr["vm"], B, S, C, prev=ml)
            sw = _swa_attend(wl["sink"], pr["qs"], pr["ks"], pr["vs"], B, S, C, prev=sw)
        nt = nx_tiles if last else nt_all
        xn, h2, rc, rr, cnt = _post(xc, mod, fo, ml, sw, wl, norm2_g[l].reshape(1, D), rw_hi, rw_lo, rb_col,
                                    nt, nx_tiles, tiles_per_batch)
        xc = _moe(h2, rc, rr, cnt, exp_w_gate, exp_w_up, exp_w_down, l, xn, mod, nx_tiles, tiles_per_batch)
    return xc[:B * S].reshape(B, S, D)
```

```python
import functools

import numpy as np
import jax
import jax.numpy as jnp
from jax import lax
from jax.experimental import pallas as pl
from jax.experimental.pallas import tpu as pltpu

F32 = jnp.float32
BF16 = jnp.bfloat16

D_MODEL = 1024
GRID_W = 64
FNET_GROUPS = 4
FNET_CH = 64
D_FNET = FNET_GROUPS * FNET_CH
MLA_HEADS = 6
MLA_Q_RANK = 256
MLA_KV_RANK = 128
MLA_NOPE = 64
MLA_ROPE = 32
MLA_QK = MLA_NOPE + MLA_ROPE
MLA_V = 64
SWA_HEADS = 6
SWA_KV_HEADS = 2
SWA_HEAD_DIM = 64
WINDOW = 128
BLOCK = 128
N_EXPERTS = 16
N_EXPERT_GROUPS = 4
EXPERTS_PER_GROUP = 4
D_EXPERT = 512
ROPE_THETA = 10000.0
EPS = 1e-6

LANES = 128
TM = 256
HEAD_SLOT = LANES
CHUNK = 16
CHUNK_LOG2 = 4
NSLOT = 768
MAX_CHUNKS = NSLOT // CHUNK
TE = 512
PREP_CHAIN_ROWS = 128
POST_CHAIN_ROWS = TM
PREP_TM = 256
MLA_CHAINS = 8
SWA_QB = 8
MIN_CHUNKS = 2 * TM // CHUNK
VMEM_LIMIT = 48 * 1024 * 1024

P_UF = 0
P_CQ = 256
P_CKV = 512
P_QS = 640
P_KS = 1024
P_VS = 1152
P_KR = 1280
P_TOT = 1408
SWA_HEAD_ORDER = (0, 3, 1, 4, 2, 5)


def _dot(a, b):
    return jnp.dot(a, b, preferred_element_type=F32)


def _dot_nt(a, b):
    return lax.dot_general(a, b, (((1,), (1,)), ((), ())), preferred_element_type=F32)


def _rms(x, n):
    return x * lax.rsqrt(jnp.sum(x * x, axis=-1, keepdims=True) / n + EPS)


def _rope(x, c, sa, sb, half):
    n = x.shape[-1]
    return x * c + pltpu.roll(x, n - half, 1) * sa + pltpu.roll(x, half, 1) * sb


def _cparams(sem):
    return pltpu.CompilerParams(dimension_semantics=sem, vmem_limit_bytes=VMEM_LIMIT)


def _adaln_kernel(c_ref, w_ref, b_ref, o_ref):
    c = c_ref[...]
    sc = c * jax.nn.sigmoid(c)
    w = w_ref[...]
    s_hi = sc.astype(BF16)
    s_lo = (sc - s_hi.astype(F32)).astype(BF16)
    w_hi = w.astype(BF16)
    w_lo = (w - w_hi.astype(F32)).astype(BF16)
    o_ref[...] = _dot(s_hi, w_hi) + (_dot(s_lo, w_hi) + _dot(s_hi, w_lo)) + b_ref[...]


def _adaln(cvec, ada_w, ada_b):
    L, D, N6 = ada_w.shape
    R = cvec.shape[0]
    bn = 512
    return pl.pallas_call(
        _adaln_kernel,
        grid=(L, N6 // bn),
        in_specs=[pl.BlockSpec((R, D), lambda l, j: (0, 0)),
                  pl.BlockSpec((None, D, bn), lambda l, j: (l, 0, j)),
                  pl.BlockSpec((None, 1, bn), lambda l, j: (l, 0, j))],
        out_specs=pl.BlockSpec((None, R, bn), lambda l, j: (l, 0, j)),
        out_shape=jax.ShapeDtypeStruct((L, R, N6), F32),
        compiler_params=_cparams(("arbitrary", "arbitrary")),
        name="adaln",
    )(cvec, ada_w, ada_b.reshape(L, 1, N6))


def _prep_kernel(x_ref, mod_ref, n1g_ref, win_ref, cqg_ref, ckvg_ref, wuq_ref, wuk_ref, wuv_ref,
                 mqg_ref, mkg_ref, sqg_ref, skg_ref, cm_ref, sam_ref, sbm_ref, cs_ref, sas_ref, sbs_ref,
                 dft_ref, pp_ref, qm_ref, km_ref, vm_ref, qs_ref, ks_ref, vs_ref):
    m = mod_ref[...]
    lo = lax.broadcasted_iota(jnp.int32, (1, LANES), 1) < SWA_HEAD_DIM
    slot_lane = jnp.bitwise_and(lax.broadcasted_iota(jnp.int32, (1, MLA_HEADS * HEAD_SLOT), 1), HEAD_SLOT - 1)
    vone = jnp.where(slot_lane == MLA_V, 1.0, 0.0)

    def head_norm(slab, g):
        sq = slab * slab
        s_lo = jnp.sum(jnp.where(lo, sq, 0.0), axis=-1, keepdims=True)
        s_hi = jnp.sum(jnp.where(lo, 0.0, sq), axis=-1, keepdims=True)
        r = jnp.where(lo, lax.rsqrt(s_lo * (1.0 / SWA_HEAD_DIM) + EPS),
                      lax.rsqrt(s_hi * (1.0 / SWA_HEAD_DIM) + EPS))
        return slab * r * g

    rows = PREP_CHAIN_ROWS
    for ch in range(x_ref.shape[0] // rows):
        rs = slice(ch * rows, (ch + 1) * rows)
        x = x_ref[rs, :]
        h = _rms(x, D_MODEL) * n1g_ref[...] * (1.0 + m[1:2]) + m[0:1]
        p = _dot(h.astype(BF16), win_ref[...])

        u = p[:, P_UF:P_UF + D_FNET].astype(BF16)
        pp_ref[rs, :] = _dot(u, dft_ref[...]).astype(BF16)

        cm, sam, sbm = cm_ref[rs, :], sam_ref[rs, :], sbm_ref[rs, :]
        cs, sas, sbs = cs_ref[rs, :], sas_ref[rs, :], sbs_ref[rs, :]

        cq = _rms(p[:, P_CQ:P_CQ + MLA_Q_RANK], MLA_Q_RANK) * cqg_ref[...]
        qraw = _dot(cq.astype(BF16), wuq_ref[...])
        mqg = mqg_ref[...]
        for hh in range(MLA_HEADS):
            sl = slice(hh * HEAD_SLOT, (hh + 1) * HEAD_SLOT)
            qn = _rms(qraw[:, sl], MLA_QK) * mqg
            qm_ref[rs, sl] = (_rope(qn, cm, sam, sbm, MLA_ROPE // 4) * (MLA_QK ** -0.5)).astype(BF16)

        ckv = (_rms(p[:, P_CKV:P_CKV + MLA_KV_RANK], MLA_KV_RANK) * ckvg_ref[...]).astype(BF16)
        knope = _dot(ckv, wuk_ref[...])
        vm_ref[rs, :] = (_dot(ckv, wuv_ref[...]) + vone).astype(BF16)
        kr = p[:, P_KR:P_KR + LANES]
        mkg = mkg_ref[...]
        kr_ss = jnp.sum(kr * kr, axis=-1, keepdims=True)
        kr_rot = _rope(kr * mkg, cm, sam, sbm, MLA_ROPE // 4)
        for hh in range(MLA_HEADS):
            sl = slice(hh * HEAD_SLOT, (hh + 1) * HEAD_SLOT)
            kn_h = knope[:, sl]
            r = lax.rsqrt((jnp.sum(kn_h * kn_h, axis=-1, keepdims=True) + kr_ss) / MLA_QK + EPS)
            km_ref[rs, sl] = ((kn_h * mkg + kr_rot) * r).astype(BF16)

        sqg = sqg_ref[...]
        for s in range(SWA_HEADS // 2):
            sl = slice(P_QS + s * LANES, P_QS + (s + 1) * LANES)
            qn = head_norm(p[:, sl], sqg)
            qs_ref[rs, s * LANES:(s + 1) * LANES] = (
                _rope(qn, cs, sas, sbs, SWA_HEAD_DIM // 4) * (SWA_HEAD_DIM ** -0.5)).astype(BF16)
        kn = head_norm(p[:, P_KS:P_KS + LANES], skg_ref[...])
        ks_ref[rs, :] = _rope(kn, cs, sas, sbs, SWA_HEAD_DIM // 4).astype(BF16)
        vs_ref[rs, :] = p[:, P_VS:P_VS + LANES].astype(BF16)


def _prep(xc, mod, n1g, wl, tabs, n_latent, seq, tm):
    T = xc.shape[0]
    nt = T // tm
    nx_tiles = n_latent // tm
    tiles_per_batch = seq // tm

    def bidx(i):
        return jnp.where(i < nx_tiles, i // tiles_per_batch, mod.shape[0] - 1)

    def ridx(i):
        return jnp.where(i < nx_tiles, i % tiles_per_batch, tiles_per_batch)

    def full(a):
        return pl.BlockSpec(a.shape, lambda i: (0,) * a.ndim)

    tab_spec = pl.BlockSpec((tm, LANES), lambda i: (ridx(i), 0))
    row = lambda w: pl.BlockSpec((tm, w), lambda i: (i, 0))
    consts = [n1g, wl["w_in"], wl["cq_g"], wl["ckv_g"], wl["w_uq"], wl["w_uk"], wl["w_uv"],
              wl["mq_g"], wl["mk_g"], wl["sq_g"], wl["sk_g"]]
    outs = [("pp", 2 * D_FNET), ("qm", MLA_HEADS * HEAD_SLOT), ("km", MLA_HEADS * HEAD_SLOT),
            ("vm", MLA_HEADS * HEAD_SLOT), ("qs", SWA_HEADS * SWA_HEAD_DIM), ("ks", LANES), ("vs", LANES)]
    res = pl.pallas_call(
        _prep_kernel,
        grid=(nt,),
        in_specs=[row(D_MODEL), pl.BlockSpec((None, 8, D_MODEL), lambda i: (bidx(i), 0, 0))]
                 + [full(a) for a in consts] + [tab_spec] * 6 + [full(tabs["dft64"])],
        out_specs=[row(w) for _, w in outs],
        out_shape=[jax.ShapeDtypeStruct((T, w), BF16) for _, w in outs],
        compiler_params=_cparams(("arbitrary",)),
        name="prep",
    )(xc, mod, *consts, tabs["cm"], tabs["sam"], tabs["sbm"], tabs["cs"], tabs["sas"], tabs["sbs"],
      tabs["dft64"])
    return dict(zip([n for n, _ in outs], res))


def _mla_kernel(*refs, with_x):
    if with_x:
        q_ref, kx_ref, kc_ref, vx_ref, vc_ref, o_ref = refs
    else:
        q_ref, kc_ref, vc_ref, o_ref = refs
    lane = lax.broadcasted_iota(jnp.int32, (1, LANES), 1)
    rows = q_ref.shape[0] // MLA_CHAINS if with_x else q_ref.shape[0]
    for r0 in range(0, q_ref.shape[0], rows):
        rs = slice(r0, r0 + rows)
        outs = []
        for hh in range(2):
            sl = slice(hh * HEAD_SLOT, (hh + 1) * HEAD_SLOT)
            q = q_ref[rs, sl]
            sc = _dot_nt(q, kc_ref[:, sl])
            m = jnp.max(sc, axis=-1, keepdims=True)
            if with_x:
                sx = _dot_nt(q, kx_ref[:, sl])
                m = jnp.maximum(m, jnp.max(sx, axis=-1, keepdims=True))
                px = jnp.exp(sx - m)
            pc = jnp.exp(sc - m)
            o = _dot(pc.astype(BF16), vc_ref[:, sl])
            if with_x:
                o = o + _dot(px.astype(BF16), vx_ref[:, sl])
            outs.append(o / o[:, MLA_V:MLA_V + 1])
        o_ref[rs, :] = jnp.where(lane < MLA_V, outs[0], pltpu.roll(outs[1], MLA_V, 1)).astype(BF16)


def _ctx_rows_kernel(kernel_fn, *refs, **kw):
    kernel_fn(*refs[:-2], refs[-1], **kw)


def _mla_attend(qm, km, vm, B, S, C, prev=None):
    T = qm.shape[0]
    npair = MLA_HEADS // 2
    with_x = prev is None
    if with_x:
        tq = 2048
        nq = S // tq
        qmap = lambda b, p, i: (b * nq + i, p)
        in_specs = [pl.BlockSpec((tq, 2 * HEAD_SLOT), qmap),
                    pl.BlockSpec((S, 2 * HEAD_SLOT), lambda b, p, i: (b, p)),
                    pl.BlockSpec((C, 2 * HEAD_SLOT), lambda b, p, i: (B * S // C + b, p)),
                    pl.BlockSpec((S, 2 * HEAD_SLOT), lambda b, p, i: (b, p)),
                    pl.BlockSpec((C, 2 * HEAD_SLOT), lambda b, p, i: (B * S // C + b, p))]
        args = (qm, km, km, vm, vm)
        body = functools.partial(_mla_kernel, with_x=True)
        aliases = {}
    else:
        tq = C
        nq = 1
        qmap = lambda b, p, i: (B * S // C + b, p)
        in_specs = [pl.BlockSpec((tq, 2 * HEAD_SLOT), qmap),
                    pl.BlockSpec((C, 2 * HEAD_SLOT), qmap),
                    pl.BlockSpec((C, 2 * HEAD_SLOT), qmap),
                    pl.BlockSpec(memory_space=pl.ANY)]
        args = (qm, km, vm, prev)
        body = functools.partial(_ctx_rows_kernel, _mla_kernel, with_x=False)
        aliases = {3: 0}
    return pl.pallas_call(
        body,
        grid=(B, npair, nq),
        in_specs=in_specs,
        out_specs=pl.BlockSpec((tq, LANES), qmap),
        out_shape=jax.ShapeDtypeStruct((T, MLA_HEADS * MLA_V), BF16),
        input_output_aliases=aliases,
        compiler_params=_cparams(("arbitrary",) * 3),
        name="mla_x" if with_x else "mla_c",
    )(*args)


def _swa_kernel(*refs, with_x, nblk, qb):
    if with_x:
        sink_ref, q_ref, kp_ref, ko_ref, kn_ref, kc_ref, vp_ref, vo_ref, vn_ref, vc_ref, o_ref = refs
    else:
        sink_ref, q_ref, kc_ref, vc_ref, o_ref = refs
    n0 = pl.program_id(1) * qb
    lane = lax.broadcasted_iota(jnp.int32, (1, LANES), 1)
    lo = lane < SWA_HEAD_DIM
    row2 = lax.broadcasted_iota(jnp.int32, (2 * BLOCK, 1), 0)
    kc, vc = kc_ref[...], vc_ref[...]
    if with_x:
        kloc = jnp.concatenate([kp_ref[...], ko_ref[...], kn_ref[...]], axis=0)
        vloc = jnp.concatenate([vp_ref[...], vo_ref[...], vn_ref[...]], axis=0)
        nk = 3 * BLOCK + kc.shape[0]
        qi = lax.broadcasted_iota(jnp.int32, (2 * BLOCK, nk), 0) % BLOCK
        kj = lax.broadcasted_iota(jnp.int32, (2 * BLOCK, nk), 1)
        out_prev = jnp.logical_and(kj < BLOCK, kj - qi < BLOCK - WINDOW)
        out_next = jnp.logical_and(jnp.logical_and(kj >= 2 * BLOCK, kj < 3 * BLOCK), kj - qi > BLOCK + WINDOW)
        in_prev = kj < BLOCK
        in_next = jnp.logical_and(kj >= 2 * BLOCK, kj < 3 * BLOCK)
    for i in range(qb):
        rs = slice(i * BLOCK, (i + 1) * BLOCK)
        if with_x:
            kall = jnp.concatenate([kloc[i * BLOCK:(i + 3) * BLOCK], kc], axis=0)
            vall = jnp.concatenate([vloc[i * BLOCK:(i + 3) * BLOCK], vc], axis=0)
            n = n0 + i
            bad = jnp.logical_or(
                jnp.logical_or(out_prev, jnp.logical_and(in_prev, n == 0)),
                jnp.logical_or(out_next, jnp.logical_and(in_next, n == nblk - 1)))
        else:
            kall, vall = kc, vc
        for s in range(SWA_HEADS // 2):
            q = q_ref[rs, s * LANES:(s + 1) * LANES]
            zero = jnp.zeros_like(q)
            q2 = jnp.concatenate([jnp.where(lo, q, zero), jnp.where(lo, zero, q)], axis=0)
            sink = jnp.where(row2 < BLOCK, sink_ref[0, s], sink_ref[0, SWA_HEADS // 2 + s])
            sc = _dot_nt(q2, kall)
            if with_x:
                sc = jnp.where(bad, -jnp.inf, sc)
            m = jnp.maximum(jnp.max(sc, axis=-1, keepdims=True), sink)
            p = jnp.exp(sc - m)
            l = jnp.sum(p, axis=-1, keepdims=True) + jnp.exp(sink - m)
            o = _dot(p.astype(BF16), vall) / l
            o_ref[rs, s * LANES:(s + 1) * LANES] = jnp.where(lo, o[:BLOCK], o[BLOCK:]).astype(BF16)


def _swa_attend(sink, qs, ks, vs, B, S, C, prev=None):
    T = qs.shape[0]
    cb = B * S // C
    with_x = prev is None
    if with_x:
        nblk = S // BLOCK
        qb = SWA_QB
        nstep = nblk // qb
        qmap = lambda b, j: (b * nstep + j, 0)
        pmap = lambda b, j: (b * nblk + jnp.maximum(j * qb - 1, 0), 0)
        nmap = lambda b, j: (b * nblk + jnp.minimum(j * qb + qb, nblk - 1), 0)
        cmap = lambda b, j: (cb + b, 0)
        kv1 = lambda mp: pl.BlockSpec((BLOCK, LANES), mp)
        kvq = pl.BlockSpec((qb * BLOCK, LANES), qmap)
        cspec = pl.BlockSpec((C, LANES), cmap)
        in_specs = [pl.BlockSpec(memory_space=pltpu.SMEM),
                    pl.BlockSpec((qb * BLOCK, SWA_HEADS * SWA_HEAD_DIM), qmap),
                    kv1(pmap), kvq, kv1(nmap), cspec, kv1(pmap), kvq, kv1(nmap), cspec]
        args = (sink, qs, ks, ks, ks, ks, vs, vs, vs, vs)
        body = functools.partial(_swa_kernel, with_x=True, nblk=nblk, qb=qb)
        aliases = {}
    else:
        nblk = C // BLOCK
        qb = nblk
        nstep = 1
        qmap = lambda b, j: (cb + b, 0)
        cspec = pl.BlockSpec((C, LANES), qmap)
        in_specs = [pl.BlockSpec(memory_space=pltpu.SMEM),
                    pl.BlockSpec((C, SWA_HEADS * SWA_HEAD_DIM), qmap), cspec, cspec,
                    pl.BlockSpec(memory_space=pl.ANY)]
        args = (sink, qs, ks, vs, prev)
        body = functools.partial(_ctx_rows_kernel, _swa_kernel, with_x=False, nblk=nblk, qb=qb)
        aliases = {4: 0}
    return pl.pallas_call(
        body,
        grid=(B, nstep),
        in_specs=in_specs,
        out_specs=pl.BlockSpec((qb * BLOCK, SWA_HEADS * SWA_HEAD_DIM), qmap),
        out_shape=jax.ShapeDtypeStruct((T, SWA_HEADS * SWA_HEAD_DIM), BF16),
        input_output_aliases=aliases,
        compiler_params=_cparams(("arbitrary",) * 2),
        name="swa_x" if with_x else "swa_c",
    )(*args)


def _fourier_kernel(c_ref, s_ref, pp_ref, w_ref, o_ref):
    f = _dot(c_ref[...], pp_ref[:, 0:D_FNET]) - _dot(s_ref[...], pp_ref[:, D_FNET:2 * D_FNET])
    o_ref[...] = _dot(f.astype(BF16), w_ref[...]).astype(BF16)


def _fourier(cmat, smat, pp, wblk, B, N, row0, prev=None):
    T = pp.shape[0]
    tq = min(512, N)
    nr = N // tq
    b0 = row0 // N
    o0 = row0 // tq
    in_specs = [pl.BlockSpec((tq, N), lambda r, b: (r, 0)),
                pl.BlockSpec((tq, N), lambda r, b: (r, 0)),
                pl.BlockSpec((N, 2 * D_FNET), lambda r, b: (b0 + b, 0)),
                pl.BlockSpec((D_FNET, D_FNET), lambda r, b: (0, 0))]
    args = (cmat, smat, pp, wblk)
    if prev is None:
        body = _fourier_kernel
        aliases = {}
    else:
        in_specs.append(pl.BlockSpec(memory_space=pl.ANY))
        args = args + (prev,)
        body = functools.partial(_ctx_rows_kernel, _fourier_kernel)
        aliases = {4: 0}
    return pl.pallas_call(
        body,
        grid=(nr, B),
        in_specs=in_specs,
        out_specs=pl.BlockSpec((tq, D_FNET), lambda r, b: (o0 + b * nr + r, 0)),
        out_shape=jax.ShapeDtypeStruct((T, D_FNET), BF16),
        input_output_aliases=aliases,
        compiler_params=_cparams(("arbitrary",) * 2),
        name="fourier_%d" % N,
    )(*args)


def _route_rows(sel, aff):
    G, K = N_EXPERT_GROUPS, EXPERTS_PER_GROUP
    gscore = []
    for g in range(G):
        a = sel[g * K:(g + 1) * K]
        best = None
        for i in range(K):
            for j in range(i + 1, K):
                v = a[i] + a[j]
                best = v if best is None else jnp.maximum(best, v)
        gscore.append(best)
    gb = jnp.zeros_like(gscore[0])
    gbest = gscore[0]
    for g in range(1, G):
        upd = gscore[g] > gbest
        gb = jnp.where(upd, float(g), gb)
        gbest = jnp.where(upd, gscore[g], gbest)
    cs, ca = [], []
    for i in range(K):
        c, a = sel[i], aff[i]
        for g in range(1, G):
            pick = gb == float(g)
            c = jnp.where(pick, sel[g * K + i], c)
            a = jnp.where(pick, aff[g * K + i], a)
        cs.append(c)
        ca.append(a)

    def first_max(vals):
        bi = jnp.zeros_like(vals[0])
        bv = vals[0]
        for i in range(1, K):
            upd = vals[i] > bv
            bi = jnp.where(upd, float(i), bi)
            bv = jnp.where(upd, vals[i], bv)
        return bi

    i1 = first_max(cs)
    cs2 = [jnp.where(i1 == float(i), -jnp.inf, cs[i]) for i in range(K)]
    i2 = first_max(cs2)
    a1 = sum(jnp.where(i1 == float(i), ca[i], 0.0) for i in range(K))
    a2 = sum(jnp.where(i2 == float(i), ca[i], 0.0) for i in range(K))
    den = a1 + a2
    return gb * K + i1, gb * K + i2, a1 / den, a2 / den


def _post_kernel(x_ref, mod_ref, fo_ref, ml_ref, sw_ref, wof_ref, wom_ref, wos_ref, n2g_ref,
                 rwh_ref, rwl_ref, rb_ref, tri_ref, ones_ref, xn_ref, h2_ref, rc_ref, rr_ref, cnt_ref):
    m = mod_ref[...]
    parts = []
    for r0 in range(0, TM, POST_CHAIN_ROWS):
        rs = slice(r0, r0 + POST_CHAIN_ROWS)
        mix = (_dot(fo_ref[rs, :], wof_ref[...]) + _dot(ml_ref[rs, :], wom_ref[...])
               + _dot(sw_ref[rs, :], wos_ref[...]))
        xn = x_ref[rs, :] + m[2:3] * mix
        xn_ref[rs, :] = xn
        h2 = _rms(xn, D_MODEL) * n2g_ref[...] * (1.0 + m[4:5]) + m[3:4]
        h2_ref[rs, :] = h2.astype(BF16)
        hh = h2.astype(BF16)
        hl = (h2 - hh.astype(F32)).astype(BF16)
        parts.append(_dot(hh, rwh_ref[...]) + (_dot(hl, rwh_ref[...]) + _dot(hh, rwl_ref[...])))
    logits = jnp.concatenate(parts, axis=0)
    lt = logits.T[0:N_EXPERTS, :]
    aff_t = jax.nn.sigmoid(lt)
    sel_t = aff_t + rb_ref[...]
    sel = [sel_t[e:e + 1, :] for e in range(N_EXPERTS)]
    aff = [aff_t[e:e + 1, :] for e in range(N_EXPERTS)]
    e1, e2, w1, w2 = _route_rows(sel, aff)

    eio = lax.broadcasted_iota(jnp.int32, (N_EXPERTS, TM), 0).astype(F32)
    oh = jnp.concatenate([jnp.where(eio == e1, 1.0, 0.0), jnp.where(eio == e2, 1.0, 0.0)], axis=1)
    ohb = oh.astype(BF16)
    rank = _dot(ohb, tri_ref[...])
    cnt = _dot(ohb, ones_ref[...]).astype(jnp.int32)
    cnt8 = jnp.left_shift(jnp.right_shift(cnt + (CHUNK - 1), CHUNK_LOG2), CHUNK_LOG2)
    cnt_ref[...] = cnt8
    cnt8f = cnt8.astype(F32)
    off = jnp.zeros((1, 1), F32)
    slot = jnp.zeros((1, 2 * TM), F32)
    for e in range(N_EXPERTS):
        slot = slot + oh[e:e + 1, :] * (off + rank[e:e + 1, :])
        off = off + cnt8f[e:e + 1, 0:1]
    s0, s1 = slot[:, :TM], slot[:, TM:]

    sub = lax.broadcasted_iota(jnp.int32, (8, TM), 0)
    rr_ref[...] = jnp.where(sub == 0, s0, jnp.where(sub == 1, s1, 0.0))
    blk = jnp.where(sub == 0, e1, jnp.where(sub == 1, e2, jnp.where(sub == 2, w1, jnp.where(
        sub == 3, w2, jnp.where(sub == 4, s0, jnp.where(sub == 5, s1, 0.0))))))
    rows = jnp.concatenate([blk, jnp.zeros((LANES - 8, TM), F32)], axis=0)
    rc_ref[...] = rows.T


def _post(xc, mod, fo, ml, sw, wl, n2g, rw_hi, rw_lo, rb_col, nt, nx_tiles, tiles_per_batch):
    rows = nt * TM

    def bidx(i):
        return jnp.where(i < nx_tiles, i // tiles_per_batch, mod.shape[0] - 1)

    def full(a):
        return pl.BlockSpec(a.shape, lambda i: (0,) * a.ndim)

    row = lambda w: pl.BlockSpec((TM, w), lambda i: (i, 0))
    pair = np.arange(2 * TM)
    tri = jnp.asarray(pair[:, None] < pair[None, :], BF16)
    ones = jnp.ones((2 * TM, LANES), BF16)
    consts = [wl["wo_f"], wl["wo_m"], wl["wo_s"], n2g, rw_hi, rw_lo, rb_col, tri, ones]
    return pl.pallas_call(
        _post_kernel,
        grid=(nt,),
        in_specs=[row(D_MODEL), pl.BlockSpec((None, 8, D_MODEL), lambda i: (bidx(i), 0, 0)),
                  row(D_FNET), row(MLA_HEADS * MLA_V), row(SWA_HEADS * SWA_HEAD_DIM)]
                 + [full(a) for a in consts],
        out_specs=[row(D_MODEL), row(D_MODEL), row(LANES),
                   pl.BlockSpec((None, 8, TM), lambda i: (i, 0, 0)),
                   pl.BlockSpec((None, N_EXPERTS, LANES), lambda i: (i, 0, 0))],
        out_shape=[jax.ShapeDtypeStruct((rows, D_MODEL), F32),
                   jax.ShapeDtypeStruct((rows, D_MODEL), BF16),
                   jax.ShapeDtypeStruct((rows, LANES), F32),
                   jax.ShapeDtypeStruct((nt, 8, TM), F32),
                   jax.ShapeDtypeStruct((nt, N_EXPERTS, LANES), jnp.int32)],
        compiler_params=_cparams(("arbitrary",)),
        name="post",
    )(xc, mod, fo, ml, sw, *consts)


def _moe_tables(cnt8, te):
    nt = cnt8.shape[0]
    tile_prefix = jnp.cumsum(cnt8, axis=0) - cnt8
    tot = jnp.sum(cnt8, axis=0)
    tot_e = ((tot + te - 1) // te) * te
    goff = jnp.cumsum(tot_e) - tot_e
    dbase = goff[None, :] + tile_prefix
    nch = cnt8 // CHUNK
    cum = jnp.cumsum(nch, axis=1)
    k = jnp.arange(MAX_CHUNKS, dtype=jnp.int32)[None, :, None]
    owns = jnp.logical_and(k >= (cum - nch)[:, None, :], k < cum[:, None, :])
    dst = jnp.sum(jnp.where(owns, dbase[:, None, :] + CHUNK * (k - (cum - nch)[:, None, :]), 0), axis=-1)
    nchunks = cum[:, -1]
    padch = (tot_e - tot) // CHUNK
    cump = jnp.cumsum(padch)
    kp = jnp.arange(N_EXPERTS * (te // CHUNK), dtype=jnp.int32)[:, None]
    pown = jnp.logical_and(kp >= (cump - padch)[None, :], kp < cump[None, :])
    pdst = jnp.sum(jnp.where(pown, (goff + tot)[None, :] + CHUNK * (kp - (cump - padch)[None, :]), 0), axis=-1)
    npad = cump[-1]
    ntile_cum = jnp.cumsum(tot_e // te)
    nact = ntile_cum[-1]
    return dict(dst=dst.reshape(-1).astype(jnp.int32), nchunks=nchunks.astype(jnp.int32),
                pdst=pdst.astype(jnp.int32), npad=npad.reshape(1).astype(jnp.int32),
                ntile_cum=ntile_cum.astype(jnp.int32), nact=nact.reshape(1).astype(jnp.int32))


def _dispatch_kernel(dst_ref, nch_ref, pdst_ref, npad_ref, h_ref, rr_ref, xs_ref, sbuf, zbuf, sem, zsem):
    i = pl.program_id(0)
    nt = pl.num_programs(0)
    slot = i % 2

    def chunk_copy(sl, k, d):
        return pltpu.make_async_copy(
            sbuf.at[sl, pl.ds(pl.multiple_of(k * CHUNK, CHUNK), CHUNK), :],
            xs_ref.at[pl.ds(pl.multiple_of(d, CHUNK), CHUNK), :], sem.at[sl])

    def wait_tile(t, sl):
        n = nch_ref[t] * CHUNK
        pltpu.make_async_copy(sbuf.at[sl, pl.ds(0, n), :], xs_ref.at[pl.ds(0, n), :], sem.at[sl]).wait()

    def pad_copy(d):
        return pltpu.make_async_copy(zbuf, xs_ref.at[pl.ds(pl.multiple_of(d, CHUNK), CHUNK), :], zsem)

    @pl.when(i == 0)
    def _():
        zbuf[...] = jnp.zeros_like(zbuf)

        def start(k, c):
            pad_copy(pdst_ref[k]).start()
            return c
        lax.fori_loop(0, npad_ref[0], start, 0)

        def wait(k, c):
            pad_copy(0).wait()
            return c
        lax.fori_loop(0, npad_ref[0], wait, 0)

    @pl.when(i >= 2)
    def _():
        wait_tile(i - 2, slot)

    rr = rr_ref[...]
    sio = lax.broadcasted_iota(jnp.int32, (NSLOT, TM), 0).astype(F32)
    psel = jnp.where(jnp.logical_or(sio == rr[0:1, :], sio == rr[1:2, :]), 1.0, 0.0).astype(BF16)
    sbuf[slot] = _dot(psel, h_ref[...])

    for k in range(MIN_CHUNKS):
        chunk_copy(slot, k, dst_ref[i * MAX_CHUNKS + k]).start(priority=k % 2)

    def issue(k, c):
        chunk_copy(slot, k, dst_ref[i * MAX_CHUNKS + k]).start()
        return c
    lax.fori_loop(MIN_CHUNKS, nch_ref[i], issue, 0)

    @pl.when(i == nt - 1)
    def _():
        wait_tile(i, slot)

        @pl.when(i >= 1)
        def _():
            wait_tile(i - 1, 1 - slot)


def _dispatch(h2, rr, tb, rows_sorted):
    nt = rr.shape[0]
    return pl.pallas_call(
        _dispatch_kernel,
        grid_spec=pltpu.PrefetchScalarGridSpec(
            num_scalar_prefetch=4,
            grid=(nt,),
            in_specs=[pl.BlockSpec((TM, D_MODEL), lambda i, *_: (i, 0)),
                      pl.BlockSpec((None, 8, TM), lambda i, *_: (i, 0, 0))],
            out_specs=pl.BlockSpec(memory_space=pl.ANY),
            scratch_shapes=[pltpu.VMEM((2, NSLOT, D_MODEL), F32), pltpu.VMEM((CHUNK, D_MODEL), F32),
                            pltpu.SemaphoreType.DMA((2,)), pltpu.SemaphoreType.DMA(())]),
        out_shape=jax.ShapeDtypeStruct((rows_sorted, D_MODEL), F32),
        compiler_params=_cparams(("arbitrary",)),
        name="moe_dispatch",
    )(tb["dst"], tb["nchunks"], tb["pdst"], tb["npad"], h2, rr)


def _expert_kernel(te_ref, na_ref, x_ref, wg_ref, wu_ref, wd_ref, o_ref, wgb, wub, wdb):
    j = pl.program_id(0)
    active = j < na_ref[0]
    fresh = jnp.logical_or(j == 0, te_ref[j] != te_ref[jnp.maximum(j - 1, 0)])

    @pl.when(jnp.logical_and(active, fresh))
    def _():
        wgb[...] = wg_ref[...].astype(BF16)
        wub[...] = wu_ref[...].astype(BF16)
        wdb[...] = wd_ref[...].astype(BF16)

    @pl.when(active)
    def _():
        x = x_ref[...].astype(BF16)
        a = _dot(x, wgb[...])
        a = a * jax.nn.sigmoid(a) * _dot(x, wub[...])
        o_ref[...] = _dot(a.astype(BF16), wdb[...])


def _experts(xs, wg, wu, wd, layer, tb, te):
    nte = xs.shape[0] // te
    jj = jnp.minimum(jnp.arange(nte, dtype=jnp.int32), tb["nact"][0] - 1)
    tile_e = jnp.sum(jj[:, None] >= tb["ntile_cum"][None, :], axis=-1).astype(jnp.int32)

    def tmap(j, te_ref, na):
        return (jnp.minimum(j, na[0] - 1), 0)

    def wmap(j, te_ref, na):
        return (layer, te_ref[j], 0, 0)

    return pl.pallas_call(
        _expert_kernel,
        grid_spec=pltpu.PrefetchScalarGridSpec(
            num_scalar_prefetch=2,
            grid=(nte,),
            in_specs=[pl.BlockSpec((te, D_MODEL), tmap),
                      pl.BlockSpec((None, None, D_MODEL, D_EXPERT), wmap),
                      pl.BlockSpec((None, None, D_MODEL, D_EXPERT), wmap),
                      pl.BlockSpec((None, None, D_EXPERT, D_MODEL), wmap)],
            out_specs=pl.BlockSpec((te, D_MODEL), tmap),
            scratch_shapes=[pltpu.VMEM((D_MODEL, D_EXPERT), BF16), pltpu.VMEM((D_MODEL, D_EXPERT), BF16),
                            pltpu.VMEM((D_EXPERT, D_MODEL), BF16)]),
        out_shape=jax.ShapeDtypeStruct(xs.shape, F32),
        compiler_params=_cparams(("arbitrary",)),
        name="moe_experts",
    )(tile_e, tb["nact"], xs, wg, wu, wd)


def _combine_kernel(dst_ref, nch_ref, xn_ref, mod_ref, rc_ref, ys_ref, o_ref, gbuf, sem):
    i = pl.program_id(0)
    nt = pl.num_programs(0)
    slot = i % 2

    def chunk_copy(sl, k, d):
        return pltpu.make_async_copy(
            ys_ref.at[pl.ds(pl.multiple_of(d, CHUNK), CHUNK), :],
            gbuf.at[sl, pl.ds(pl.multiple_of(k * CHUNK, CHUNK), CHUNK), :], sem.at[sl])

    def issue_tile(t, sl):
        for k in range(MIN_CHUNKS):
            chunk_copy(sl, k, dst_ref[t * MAX_CHUNKS + k]).start(priority=k % 2)

        def body(k, c):
            chunk_copy(sl, k, dst_ref[t * MAX_CHUNKS + k]).start()
            return c
        lax.fori_loop(MIN_CHUNKS, nch_ref[t], body, 0)

    @pl.when(i == 0)
    def _():
        gbuf[...] = jnp.zeros_like(gbuf)
        issue_tile(0, 0)

    @pl.when(i + 1 < nt)
    def _():
        issue_tile(i + 1, 1 - slot)

    n = nch_ref[i] * CHUNK
    pltpu.make_async_copy(ys_ref.at[pl.ds(0, n), :], gbuf.at[slot, pl.ds(0, n), :], sem.at[slot]).wait()

    g = gbuf[slot].astype(BF16)
    rc = rc_ref[...]
    lio = lax.broadcasted_iota(jnp.int32, (TM, NSLOT), 1).astype(F32)
    p0 = jnp.where(lio == rc[:, 4:5], 1.0, 0.0).astype(BF16)
    p1 = jnp.where(lio == rc[:, 5:6], 1.0, 0.0).astype(BF16)
    y = rc[:, 2:3] * _dot(p0, g) + rc[:, 3:4] * _dot(p1, g)
    o_ref[...] = xn_ref[...] + mod_ref[5:6, :] * y


def _combine(xn, mod, rc, ys, tb, nx_tiles, tiles_per_batch):
    rows = xn.shape[0]
    nt = rows // TM

    def bidx(i, *_):
        return (jnp.where(i < nx_tiles, i // tiles_per_batch, mod.shape[0] - 1), 0, 0)

    return pl.pallas_call(
        _combine_kernel,
        grid_spec=pltpu.PrefetchScalarGridSpec(
            num_scalar_prefetch=2,
            grid=(nt,),
            in_specs=[pl.BlockSpec((TM, D_MODEL), lambda i, *_: (i, 0)),
                      pl.BlockSpec((None, 8, D_MODEL), bidx),
                      pl.BlockSpec((TM, LANES), lambda i, *_: (i, 0)),
                      pl.BlockSpec(memory_space=pl.ANY)],
            out_specs=pl.BlockSpec((TM, D_MODEL), lambda i, *_: (i, 0)),
            scratch_shapes=[pltpu.VMEM((2, NSLOT, D_MODEL), F32), pltpu.SemaphoreType.DMA((2,))]),
        out_shape=jax.ShapeDtypeStruct((rows, D_MODEL), F32),
        compiler_params=_cparams(("arbitrary",)),
        name="moe_combine",
    )(tb["dst"], tb["nchunks"], xn, mod, rc, ys)


def _moe(h2, rc, rr, cnt, wg, wu, wd, layer, xn, mod, nx_tiles, tiles_per_batch):
    nt = rr.shape[0]
    te = TE
    max_rows = 2 * nt * TM + (CHUNK - 1) * N_EXPERTS * nt + N_EXPERTS * (te - CHUNK)
    rows_sorted = ((max_rows + te - 1) // te) * te
    tb = _moe_tables(cnt[:, :, 0], te)
    xs = _dispatch(h2, rr, tb, rows_sorted)
    ys = _experts(xs, wg, wu, wd, layer, tb, te)
    return _combine(xn, mod, rc, ys, tb, nx_tiles, tiles_per_batch)


def _rope_tables(S, C):
    t = jnp.arange(S)
    rows, cols = (t // GRID_W).astype(F32), (t % GRID_W).astype(F32)

    def axis_tabs(d_rot, lane0, width):
        d_axis = d_rot // 2
        inv = ROPE_THETA ** (-jnp.arange(0, d_axis, 2, dtype=F32) / d_axis)
        ar, ac = rows[:, None] * inv, cols[:, None] * inv
        ang = jnp.concatenate([ar, ar, ac, ac], axis=-1)
        q = d_rot // 4
        first = np.concatenate([np.ones(q), np.zeros(q), np.ones(q), np.zeros(q)]).astype(np.float32)
        pad = ((0, C), (lane0, width - lane0 - d_rot))
        cos = jnp.pad(jnp.cos(ang) - 1.0, pad) + 1.0
        sa = jnp.pad(-jnp.sin(ang) * first, pad)
        sb = jnp.pad(jnp.sin(ang) * (1.0 - first), pad)
        return cos, sa, sb

    cm, sam, sbm = axis_tabs(MLA_ROPE, MLA_NOPE, LANES)
    cs, sas, sbs = axis_tabs(SWA_HEAD_DIM, 0, SWA_HEAD_DIM)
    tile2 = lambda a: jnp.concatenate([a, a], axis=1)
    return dict(cm=cm, sam=sam, sbm=sbm, cs=tile2(cs), sas=tile2(sas), sbs=tile2(sbs))


_TWO_PI_HI = float(np.float32(2.0 * np.pi))
_TWO_PI_LO = float(np.float32(2.0 * np.pi - np.float64(np.float32(2.0 * np.pi))))


def _dft_mats(N):
    k = jnp.arange(N, dtype=jnp.int32)

    def cos_sin(rows, period):
        frac = ((rows[:, None] * k[None, :]) % period).astype(F32) / period
        ang = _TWO_PI_HI * frac + _TWO_PI_LO * frac
        return jnp.cos(ang), jnp.sin(ang)

    if N <= 4 * FNET_CH:
        return cos_sin(k, N)
    A = N // FNET_CH
    c1, s1 = cos_sin(jnp.arange(A, dtype=jnp.int32), A)
    c2, s2 = cos_sin(jnp.arange(FNET_CH, dtype=jnp.int32), N)
    c = c1[:, None, :] * c2[None, :, :] - s1[:, None, :] * s2[None, :, :]
    s = s1[:, None, :] * c2[None, :, :] + c1[:, None, :] * s2[None, :, :]
    return c.reshape(N, N), s.reshape(N, N)


def _dft64_blocks():
    c, s = _dft_mats(FNET_CH)
    eye = jnp.eye(FNET_GROUPS, dtype=F32)
    return jnp.concatenate([jnp.kron(eye, c), jnp.kron(eye, s)], axis=1)


def _layer_weights(l, w_in, fnet_w, mla_cq_g, mla_ckv_g, mla_w_uq, mla_w_uk, mla_w_uv, mla_q_g, mla_k_g,
                   swa_q_g, swa_k_g, swa_sink, w_out):
    D = D_MODEL
    wi = w_in[l]
    o_kr = D_FNET + MLA_Q_RANK + MLA_KV_RANK
    o_qs = o_kr + MLA_ROPE
    o_ks = o_qs + SWA_HEADS * SWA_HEAD_DIM
    o_vs = o_ks + SWA_KV_HEADS * SWA_HEAD_DIM
    order = np.array(SWA_HEAD_ORDER)
    w_qs = wi[:, o_qs:o_ks].reshape(D, SWA_HEADS, SWA_HEAD_DIM)[:, order].reshape(D, -1)
    z = lambda n: jnp.zeros((D, n), F32)
    win = jnp.concatenate([wi[:, :o_kr], w_qs, wi[:, o_ks:o_vs], wi[:, o_vs:],
                           z(MLA_NOPE), wi[:, o_kr:o_qs], z(LANES - MLA_QK)], axis=1)
    pad_slot = lambda w, d: jnp.pad(w.reshape(w.shape[0], MLA_HEADS, d),
                                    ((0, 0), (0, 0), (0, HEAD_SLOT - d))).reshape(w.shape[0], -1)
    wo = w_out[l]
    o_m = D_FNET
    o_s = D_FNET + MLA_HEADS * MLA_V
    wo_s = wo[o_s:].reshape(SWA_HEADS, SWA_HEAD_DIM, D)[order].reshape(-1, D)
    fw = fnet_w[l]
    wblk = jnp.zeros((D_FNET, D_FNET), F32)
    for g in range(FNET_GROUPS):
        wblk = wblk.at[g * FNET_CH:(g + 1) * FNET_CH, g * FNET_CH:(g + 1) * FNET_CH].set(fw[g])
    pad_g = lambda g: jnp.pad(g, (0, HEAD_SLOT - MLA_QK)).reshape(1, HEAD_SLOT)
    return dict(
        w_in=win.astype(BF16),
        cq_g=mla_cq_g[l].reshape(1, -1), ckv_g=mla_ckv_g[l].reshape(1, -1),
        w_uq=pad_slot(mla_w_uq[l], MLA_QK).astype(BF16),
        w_uk=pad_slot(mla_w_uk[l], MLA_NOPE).astype(BF16),
        w_uv=pad_slot(mla_w_uv[l], MLA_V).astype(BF16),
        mq_g=pad_g(mla_q_g[l]), mk_g=pad_g(mla_k_g[l]),
        sq_g=jnp.tile(swa_q_g[l], 2).reshape(1, LANES), sk_g=jnp.tile(swa_k_g[l], 2).reshape(1, LANES),
        sink=swa_sink[l].reshape(1, SWA_HEADS),
        wo_f=wo[:o_m].astype(BF16), wo_m=wo[o_m:o_s].astype(BF16), wo_s=wo_s.astype(BF16),
        fnet=wblk,
    )


def kernel(x, c, ctx, c_ctx, ada_w, ada_b, norm1_g, norm2_g, w_in, fnet_w, mla_cq_g, mla_ckv_g, mla_w_uq,
           mla_w_uk, mla_w_uv, mla_q_g, mla_k_g, swa_q_g, swa_k_g, swa_sink, w_out, router_w, router_b,
           exp_w_gate, exp_w_up, exp_w_down):
    B, S, D = x.shape
    C = ctx.shape[1]
    L = ada_w.shape[0]
    assert D == D_MODEL and S % 512 == 0 and C == TM and S % TM == 0
    nx_tiles = B * S // TM
    nt_all = nx_tiles + B * C // TM
    tiles_per_batch = S // TM

    prep_tm = PREP_TM if (B * C) % PREP_TM == 0 and S % PREP_TM == 0 else TM
    tabs = _rope_tables(S, prep_tm)
    tabs["dft64"] = _dft64_blocks().astype(BF16)
    dft = {n: tuple(m.astype(BF16) for m in _dft_mats(n)) for n in (S, C)}
    fscale = {n: lax.rsqrt(jnp.full((), n * FNET_CH, F32)) for n in (S, C)}

    nmod = 16
    cvec = jnp.concatenate([c, c_ctx[None, :], jnp.zeros((nmod - B - 1, D), F32)], axis=0)
    mod_all = _adaln(cvec, ada_w, ada_b)
    mod_all = mod_all[:, :B + 1].reshape(L, B + 1, 6, D)
    mod_all = jnp.pad(mod_all, ((0, 0), (0, 0), (0, 2), (0, 0)))

    rw = jnp.pad(router_w, ((0, 0), (0, LANES - N_EXPERTS)))
    rw_hi = rw.astype(BF16)
    rw_lo = (rw - rw_hi.astype(F32)).astype(BF16)
    rb_col = router_b.reshape(N_EXPERTS, 1)

    xc = jnp.concatenate([x.reshape(B * S, D), ctx.reshape(B * C, D)], axis=0)
    for l in range(L):
        last = l == L - 1
        wl = _layer_weights(l, w_in, fnet_w, mla_cq_g, mla_ckv_g, mla_w_uq, mla_w_uk, mla_w_uv, mla_q_g,
                            mla_k_g, swa_q_g, swa_k_g, swa_sink, w_out)
        mod = mod_all[l]
        pr = _prep(xc, mod, norm1_g[l].reshape(1, D), wl, tabs, B * S, S, prep_tm)
        fo = _fourier(dft[S][0], dft[S][1], pr["pp"], (wl["fnet"] * fscale[S]).astype(BF16), B, S, 0)
        ml = _mla_attend(pr["qm"], pr["km"], pr["vm"], B, S, C)
        sw = _swa_attend(wl["sink"], pr["qs"], pr["ks"], pr["vs"], B, S, C)
        if not last:
            fo = _fourier(dft[C][0], dft[C][1], pr["pp"], (wl["fnet"] * fscale[C]).astype(BF16), B, C, B * S,
                          prev=fo)
            ml = _mla_attend(pr["qm"], pr["km"], pr["vm"], B, S, C, prev=ml)
            sw = _swa_attend(wl["sink"], pr["qs"], pr["ks"], pr["vs"], B, S, C, prev=sw)
        nt = nx_tiles if last else nt_all
        xn, h2, rc, rr, cnt = _post(xc, mod, fo, ml, sw, wl, norm2_g[l].reshape(1, D), rw_hi, rw_lo, rb_col,
                                    nt, nx_tiles, tiles_per_batch)
        xc = _moe(h2, rc, rr, cnt, exp_w_gate, exp_w_up, exp_w_down, l, xn, mod, nx_tiles, tiles_per_batch)
    return xc[:B * S].reshape(B, S, D)
```

```python
import functools

import numpy as np
import jax
import jax.numpy as jnp
from jax import lax
from jax.experimental import pallas as pl
from jax.experimental.pallas import tpu as pltpu

F32 = jnp.float32
BF16 = jnp.bfloat16

D_MODEL = 1024
GRID_W = 64
FNET_GROUPS = 4
FNET_CH = 64
D_FNET = FNET_GROUPS * FNET_CH
MLA_HEADS = 6
MLA_Q_RANK = 256
MLA_KV_RANK = 128
MLA_NOPE = 64
MLA_ROPE = 32
MLA_QK = MLA_NOPE + MLA_ROPE
MLA_V = 64
SWA_HEADS = 6
SWA_KV_HEADS = 2
SWA_HEAD_DIM = 64
WINDOW = 128
BLOCK = 128
N_EXPERTS = 16
N_EXPERT_GROUPS = 4
EXPERTS_PER_GROUP = 4
D_EXPERT = 512
ROPE_THETA = 10000.0
EPS = 1e-6

LANES = 128
TM = 256
HEAD_SLOT = LANES
CHUNK = 16
CHUNK_LOG2 = 4
NSLOT = 768
MAX_CHUNKS = NSLOT // CHUNK
TE = 512
PREP_CHAIN_ROWS = 128
POST_CHAIN_ROWS = TM
PREP_TM = 256
MLA_CHAINS = 8
SWA_QB = 8
MIN_CHUNKS = 2 * TM // CHUNK
VMEM_LIMIT = 48 * 1024 * 1024

P_UF = 0
P_CQ = 256
P_CKV = 512
P_QS = 640
P_KS = 1024
P_VS = 1152
P_KR = 1280
P_TOT = 1408
SWA_HEAD_ORDER = (0, 3, 1, 4, 2, 5)


def _dot(a, b):
    return jnp.dot(a, b, preferred_element_type=F32)


def _dot_nt(a, b):
    return lax.dot_general(a, b, (((1,), (1,)), ((), ())), preferred_element_type=F32)


def _rms(x, n):
    return x * lax.rsqrt(jnp.sum(x * x, axis=-1, keepdims=True) / n + EPS)


def _rope(x, c, sa, sb, half):
    n = x.shape[-1]
    return x * c + pltpu.roll(x, n - half, 1) * sa + pltpu.roll(x, half, 1) * sb


def _cparams(sem):
    return pltpu.CompilerParams(dimension_semantics=sem, vmem_limit_bytes=VMEM_LIMIT)


def _adaln_kernel(c_ref, w_ref, b_ref, o_ref):
    c = c_ref[...]
    sc = c * jax.nn.sigmoid(c)
    w = w_ref[...]
    s_hi = sc.astype(BF16)
    s_lo = (sc - s_hi.astype(F32)).astype(BF16)
    w_hi = w.astype(BF16)
    w_lo = (w - w_hi.astype(F32)).astype(BF16)
    o_ref[...] = _dot(s_hi, w_hi) + (_dot(s_lo, w_hi) + _dot(s_hi, w_lo)) + b_ref[...]


def _adaln(cvec, ada_w, ada_b):
    L, D, N6 = ada_w.shape
    R = cvec.shape[0]
    bn = 512
    return pl.pallas_call(
        _adaln_kernel,
        grid=(L, N6 // bn),
        in_specs=[pl.BlockSpec((R, D), lambda l, j: (0, 0)),
                  pl.BlockSpec((None, D, bn), lambda l, j: (l, 0, j)),
                  pl.BlockSpec((None, 1, bn), lambda l, j: (l, 0, j))],
        out_specs=pl.BlockSpec((None, R, bn), lambda l, j: (l, 0, j)),
        out_shape=jax.ShapeDtypeStruct((L, R, N6), F32),
        compiler_params=_cparams(("arbitrary", "arbitrary")),
        name="adaln",
    )(cvec, ada_w, ada_b.reshape(L, 1, N6))


def _prep_kernel(x_ref, mod_ref, n1g_ref, win_ref, cqg_ref, ckvg_ref, wuq_ref, wuk_ref, wuv_ref,
                 mqg_ref, mkg_ref, sqg_ref, skg_ref, cm_ref, sam_ref, sbm_ref, cs_ref, sas_ref, sbs_ref,
                 dft_ref, pp_ref, qm_ref, km_ref, vm_ref, qs_ref, ks_ref, vs_ref):
    m = mod_ref[...]
    lo = lax.broadcasted_iota(jnp.int32, (1, LANES), 1) < SWA_HEAD_DIM
    slot_lane = jnp.bitwise_and(lax.broadcasted_iota(jnp.int32, (1, MLA_HEADS * HEAD_SLOT), 1), HEAD_SLOT - 1)
    vone = jnp.where(slot_lane == MLA_V, 1.0, 0.0)

    def head_norm(slab, g):
        sq = slab * slab
        s_lo = jnp.sum(jnp.where(lo, sq, 0.0), axis=-1, keepdims=True)
        s_hi = jnp.sum(jnp.where(lo, 0.0, sq), axis=-1, keepdims=True)
        r = jnp.where(lo, lax.rsqrt(s_lo * (1.0 / SWA_HEAD_DIM) + EPS),
                      lax.rsqrt(s_hi * (1.0 / SWA_HEAD_DIM) + EPS))
        return slab * r * g

    rows = PREP_CHAIN_ROWS
    for ch in range(x_ref.shape[0] // rows):
        rs = slice(ch * rows, (ch + 1) * rows)
        x = x_ref[rs, :]
        h = _rms(x, D_MODEL) * n1g_ref[...] * (1.0 + m[1:2]) + m[0:1]
        p = _dot(h.astype(BF16), win_ref[...])

        u = p[:, P_UF:P_UF + D_FNET].astype(BF16)
        pp_ref[rs, :] = _dot(u, dft_ref[...]).astype(BF16)

        cm, sam, sbm = cm_ref[rs, :], sam_ref[rs, :], sbm_ref[rs, :]
        cs, sas, sbs = cs_ref[rs, :], sas_ref[rs, :], sbs_ref[rs, :]

        cq = _rms(p[:, P_CQ:P_CQ + MLA_Q_RANK], MLA_Q_RANK) * cqg_ref[...]
        qraw = _dot(cq.astype(BF16), wuq_ref[...])
        mqg = mqg_ref[...]
        for hh in range(MLA_HEADS):
            sl = slice(hh * HEAD_SLOT, (hh + 1) * HEAD_SLOT)
            qn = _rms(qraw[:, sl], MLA_QK) * mqg
            qm_ref[rs, sl] = (_rope(qn, cm, sam, sbm, MLA_ROPE // 4) * (MLA_QK ** -0.5)).astype(BF16)

        ckv = (_rms(p[:, P_CKV:P_CKV + MLA_KV_RANK], MLA_KV_RANK) * ckvg_ref[...]).astype(BF16)
        knope = _dot(ckv, wuk_ref[...])
        vm_ref[rs, :] = (_dot(ckv, wuv_ref[...]) + vone).astype(BF16)
        kr = p[:, P_KR:P_KR + LANES]
        mkg = mkg_ref[...]
        kr_ss = jnp.sum(kr * kr, axis=-1, keepdims=True)
        kr_rot = _rope(kr * mkg, cm, sam, sbm, MLA_ROPE // 4)
        for hh in range(MLA_HEADS):
            sl = slice(hh * HEAD_SLOT, (hh + 1) * HEAD_SLOT)
            kn_h = knope[:, sl]
            r = lax.rsqrt((jnp.sum(kn_h * kn_h, axis=-1, keepdims=True) + kr_ss) / MLA_QK + EPS)
            km_ref[rs, sl] = ((kn_h * mkg + kr_rot) * r).astype(BF16)

        sqg = sqg_ref[...]
        for s in range(SWA_HEADS // 2):
            sl = slice(P_QS + s * LANES, P_QS + (s + 1) * LANES)
            qn = head_norm(p[:, sl], sqg)
            qs_ref[rs, s * LANES:(s + 1) * LANES] = (
                _rope(qn, cs, sas, sbs, SWA_HEAD_DIM // 4) * (SWA_HEAD_DIM ** -0.5)).astype(BF16)
        kn = head_norm(p[:, P_KS:P_KS + LANES], skg_ref[...])
        ks_ref[rs, :] = _rope(kn, cs, sas, sbs, SWA_HEAD_DIM // 4).astype(BF16)
        vs_ref[rs, :] = p[:, P_VS:P_VS + LANES].astype(BF16)


def _prep(xc, mod, n1g, wl, tabs, n_latent, seq, tm):
    T = xc.shape[0]
    nt = T // tm
    nx_tiles = n_latent // tm
    tiles_per_batch = seq // tm

    def bidx(i):
        return jnp.where(i < nx_tiles, i // tiles_per_batch, mod.shape[0] - 1)

    def ridx(i):
        return jnp.where(i < nx_tiles, i % tiles_per_batch, tiles_per_batch)

    def full(a):
        return pl.BlockSpec(a.shape, lambda i: (0,) * a.ndim)

    tab_spec = pl.BlockSpec((tm, LANES), lambda i: (ridx(i), 0))
    row = lambda w: pl.BlockSpec((tm, w), lambda i: (i, 0))
    consts = [n1g, wl["w_in"], wl["cq_g"], wl["ckv_g"], wl["w_uq"], wl["w_uk"], wl["w_uv"],
              wl["mq_g"], wl["mk_g"], wl["sq_g"], wl["sk_g"]]
    outs = [("pp", 2 * D_FNET), ("qm", MLA_HEADS * HEAD_SLOT), ("km", MLA_HEADS * HEAD_SLOT),
            ("vm", MLA_HEADS * HEAD_SLOT), ("qs", SWA_HEADS * SWA_HEAD_DIM), ("ks", LANES), ("vs", LANES)]
    res = pl.pallas_call(
        _prep_kernel,
        grid=(nt,),
        in_specs=[row(D_MODEL), pl.BlockSpec((None, 8, D_MODEL), lambda i: (bidx(i), 0, 0))]
                 + [full(a) for a in consts] + [tab_spec] * 6 + [full(tabs["dft64"])],
        out_specs=[row(w) for _, w in outs],
        out_shape=[jax.ShapeDtypeStruct((T, w), BF16) for _, w in outs],
        compiler_params=_cparams(("arbitrary",)),
        name="prep",
    )(xc, mod, *consts, tabs["cm"], tabs["sam"], tabs["sbm"], tabs["cs"], tabs["sas"], tabs["sbs"],
      tabs["dft64"])
    return dict(zip([n for n, _ in outs], res))


def _mla_kernel(*refs, with_x):
    if with_x:
        q_ref, kx_ref, kc_ref, vx_ref, vc_ref, o_ref = refs
    else:
        q_ref, kc_ref, vc_ref, o_ref = refs
    lane = lax.broadcasted_iota(jnp.int32, (1, LANES), 1)
    rows = q_ref.shape[0] // MLA_CHAINS if with_x else q_ref.shape[0]
    for r0 in range(0, q_ref.shape[0], rows):
        rs = slice(r0, r0 + rows)
        outs = []
        for hh in range(2):
            sl = slice(hh * HEAD_SLOT, (hh + 1) * HEAD_SLOT)
            q = q_ref[rs, sl]
            sc = _dot_nt(q, kc_ref[:, sl])
            m = jnp.max(sc, axis=-1, keepdims=True)
            if with_x:
                sx = _dot_nt(q, kx_ref[:, sl])
                m = jnp.maximum(m, jnp.max(sx, axis=-1, keepdims=True))
                px = jnp.exp(sx - m)
            pc = jnp.exp(sc - m)
            o = _dot(pc.astype(BF16), vc_ref[:, sl])
            if with_x:
                o = o + _dot(px.astype(BF16), vx_ref[:, sl])
            outs.append(o / o[:, MLA_V:MLA_V + 1])
        o_ref[rs, :] = jnp.where(lane < MLA_V, outs[0], pltpu.roll(outs[1], MLA_V, 1)).astype(BF16)


def _ctx_rows_kernel(kernel_fn, *refs, **kw):
    kernel_fn(*refs[:-2], refs[-1], **kw)


def _mla_attend(qm, km, vm, B, S, C, prev=None):
    T = qm.shape[0]
    npair = MLA_HEADS // 2
    with_x = prev is None
    if with_x:
        tq = 2048
        nq = S // tq
        qmap = lambda b, p, i: (b * nq + i, p)
        in_specs = [pl.BlockSpec((tq, 2 * HEAD_SLOT), qmap),
                    pl.BlockSpec((S, 2 * HEAD_SLOT), lambda b, p, i: (b, p)),
                    pl.BlockSpec((C, 2 * HEAD_SLOT), lambda b, p, i: (B * S // C + b, p)),
                    pl.BlockSpec((S, 2 * HEAD_SLOT), lambda b, p, i: (b, p)),
                    pl.BlockSpec((C, 2 * HEAD_SLOT), lambda b, p, i: (B * S // C + b, p))]
        args = (qm, km, km, vm, vm)
        body = functools.partial(_mla_kernel, with_x=True)
        aliases = {}
    else:
        tq = C
        nq = 1
        qmap = lambda b, p, i: (B * S // C + b, p)
        in_specs = [pl.BlockSpec((tq, 2 * HEAD_SLOT), qmap),
                    pl.BlockSpec((C, 2 * HEAD_SLOT), qmap),
                    pl.BlockSpec((C, 2 * HEAD_SLOT), qmap),
                    pl.BlockSpec(memory_space=pl.ANY)]
        args = (qm, km, vm, prev)
        body = functools.partial(_ctx_rows_kernel, _mla_kernel, with_x=False)
        aliases = {3: 0}
    return pl.pallas_call(
        body,
        grid=(B, npair, nq),
        in_specs=in_specs,
        out_specs=pl.BlockSpec((tq, LANES), qmap),
        out_shape=jax.ShapeDtypeStruct((T, MLA_HEADS * MLA_V), BF16),
        input_output_aliases=aliases,
        compiler_params=_cparams(("arbitrary",) * 3),
        name="mla_x" if with_x else "mla_c",
    )(*args)


def _swa_kernel(*refs, with_x, nblk, qb):
    if with_x:
        sink_ref, q_ref, kp_ref, ko_ref, kn_ref, kc_ref, vp_ref, vo_ref, vn_ref, vc_ref, o_ref = refs
    else:
        sink_ref, q_ref, kc_ref, vc_ref, o_ref = refs
    n0 = pl.program_id(1) * qb
    lane = lax.broadcasted_iota(jnp.int32, (1, LANES), 1)
    lo = lane < SWA_HEAD_DIM
    row2 = lax.broadcasted_iota(jnp.int32, (2 * BLOCK, 1), 0)
    kc, vc = kc_ref[...], vc_ref[...]
    if with_x:
        kloc = jnp.concatenate([kp_ref[...], ko_ref[...], kn_ref[...]], axis=0)
        vloc = jnp.concatenate([vp_ref[...], vo_ref[...], vn_ref[...]], axis=0)
        nk = 3 * BLOCK + kc.shape[0]
        qi = lax.broadcasted_iota(jnp.int32, (2 * BLOCK, nk), 0) % BLOCK
        kj = lax.broadcasted_iota(jnp.int32, (2 * BLOCK, nk), 1)
        out_prev = jnp.logical_and(kj < BLOCK, kj - qi < BLOCK - WINDOW)
        out_next = jnp.logical_and(jnp.logical_and(kj >= 2 * BLOCK, kj < 3 * BLOCK), kj - qi > BLOCK + WINDOW)
        in_prev = kj < BLOCK
        in_next = jnp.logical_and(kj >= 2 * BLOCK, kj < 3 * BLOCK)
    for i in range(qb):
        rs = slice(i * BLOCK, (i + 1) * BLOCK)
        if with_x:
            kall = jnp.concatenate([kloc[i * BLOCK:(i + 3) * BLOCK], kc], axis=0)
            vall = jnp.concatenate([vloc[i * BLOCK:(i + 3) * BLOCK], vc], axis=0)
            n = n0 + i
            bad = jnp.logical_or(
                jnp.logical_or(out_prev, jnp.logical_and(in_prev, n == 0)),
                jnp.logical_or(out_next, jnp.logical_and(in_next, n == nblk - 1)))
        else:
            kall, vall = kc, vc
        for s in range(SWA_HEADS // 2):
            q = q_ref[rs, s * LANES:(s + 1) * LANES]
            zero = jnp.zeros_like(q)
            q2 = jnp.concatenate([jnp.where(lo, q, zero), jnp.where(lo, zero, q)], axis=0)
            sink = jnp.where(row2 < BLOCK, sink_ref[0, s], sink_ref[0, SWA_HEADS // 2 + s])
            sc = _dot_nt(q2, kall)
            if with_x:
                sc = jnp.where(bad, -jnp.inf, sc)
            m = jnp.maximum(jnp.max(sc, axis=-1, keepdims=True), sink)
            p = jnp.exp(sc - m)
            l = jnp.sum(p, axis=-1, keepdims=True) + jnp.exp(sink - m)
            o = _dot(p.astype(BF16), vall) / l
            o_ref[rs, s * LANES:(s + 1) * LANES] = jnp.where(lo, o[:BLOCK], o[BLOCK:]).astype(BF16)


def _swa_attend(sink, qs, ks, vs, B, S, C, prev=None):
    T = qs.shape[0]
    cb = B * S // C
    with_x = prev is None
    if with_x:
        nblk = S // BLOCK
        qb = SWA_QB
        nstep = nblk // qb
        qmap = lambda b, j: (b * nstep + j, 0)
        pmap = lambda b, j: (b * nblk + jnp.maximum(j * qb - 1, 0), 0)
        nmap = lambda b, j: (b * nblk + jnp.minimum(j * qb + qb, nblk - 1), 0)
        cmap = lambda b, j: (cb + b, 0)
        kv1 = lambda mp: pl.BlockSpec((BLOCK, LANES), mp)
        kvq = pl.BlockSpec((qb * BLOCK, LANES), qmap)
        cspec = pl.BlockSpec((C, LANES), cmap)
        in_specs = [pl.BlockSpec(memory_space=pltpu.SMEM),
                    pl.BlockSpec((qb * BLOCK, SWA_HEADS * SWA_HEAD_DIM), qmap),
                    kv1(pmap), kvq, kv1(nmap), cspec, kv1(pmap), kvq, kv1(nmap), cspec]
        args = (sink, qs, ks, ks, ks, ks, vs, vs, vs, vs)
        body = functools.partial(_swa_kernel, with_x=True, nblk=nblk, qb=qb)
        aliases = {}
    else:
        nblk = C // BLOCK
        qb = nblk
        nstep = 1
        qmap = lambda b, j: (cb + b, 0)
        cspec = pl.BlockSpec((C, LANES), qmap)
        in_specs = [pl.BlockSpec(memory_space=pltpu.SMEM),
                    pl.BlockSpec((C, SWA_HEADS * SWA_HEAD_DIM), qmap), cspec, cspec,
                    pl.BlockSpec(memory_space=pl.ANY)]
        args = (sink, qs, ks, vs, prev)
        body = functools.partial(_ctx_rows_kernel, _swa_kernel, with_x=False, nblk=nblk, qb=qb)
        aliases = {4: 0}
    return pl.pallas_call(
        body,
        grid=(B, nstep),
        in_specs=in_specs,
        out_specs=pl.BlockSpec((qb * BLOCK, SWA_HEADS * SWA_HEAD_DIM), qmap),
        out_shape=jax.ShapeDtypeStruct((T, SWA_HEADS * SWA_HEAD_DIM), BF16),
        input_output_aliases=aliases,
        compiler_params=_cparams(("arbitrary",) * 2),
        name="swa_x" if with_x else "swa_c",
    )(*args)


def _fourier_kernel(c_ref, s_ref, pp_ref, w_ref, o_ref):
    f = _dot(c_ref[...], pp_ref[:, 0:D_FNET]) - _dot(s_ref[...], pp_ref[:, D_FNET:2 * D_FNET])
    o_ref[...] = _dot(f.astype(BF16), w_ref[...]).astype(BF16)


def _fourier(cmat, smat, pp, wblk, B, N, row0, prev=None):
    T = pp.shape[0]
    tq = min(512, N)
    nr = N // tq
    b0 = row0 // N
    o0 = row0 // tq
    in_specs = [pl.BlockSpec((tq, N), lambda r, b: (r, 0)),
                pl.BlockSpec((tq, N), lambda r, b: (r, 0)),
                pl.BlockSpec((N, 2 * D_FNET), lambda r, b: (b0 + b, 0)),
                pl.BlockSpec((D_FNET, D_FNET), lambda r, b: (0, 0))]
    args = (cmat, smat, pp, wblk)
    if prev is None:
        body = _fourier_kernel
        aliases = {}
    else:
        in_specs.append(pl.BlockSpec(memory_space=pl.ANY))
        args = args + (prev,)
        body = functools.partial(_ctx_rows_kernel, _fourier_kernel)
        aliases = {4: 0}
    return pl.pallas_call(
        body,
        grid=(nr, B),
        in_specs=in_specs,
        out_specs=pl.BlockSpec((tq, D_FNET), lambda r, b: (o0 + b * nr + r, 0)),
        out_shape=jax.ShapeDtypeStruct((T, D_FNET), BF16),
        input_output_aliases=aliases,
        compiler_params=_cparams(("arbitrary",) * 2),
        name="fourier_%d" % N,
    )(*args)


def _route_rows(sel, aff):
    G, K = N_EXPERT_GROUPS, EXPERTS_PER_GROUP
    gscore = []
    for g in range(G):
        a = sel[g * K:(g + 1) * K]
        best = None
        for i in range(K):
            for j in range(i + 1, K):
                v = a[i] + a[j]
                best = v if best is None else jnp.maximum(best, v)
        gscore.append(best)
    gb = jnp.zeros_like(gscore[0])
    gbest = gscore[0]
    for g in range(1, G):
        upd = gscore[g] > gbest
        gb = jnp.where(upd, float(g), gb)
        gbest = jnp.where(upd, gscore[g], gbest)
    cs, ca = [], []
    for i in range(K):
        c, a = sel[i], aff[i]
        for g in range(1, G):
            pick = gb == float(g)
            c = jnp.where(pick, sel[g * K + i], c)
            a = jnp.where(pick, aff[g * K + i], a)
        cs.append(c)
        ca.append(a)

    def first_max(vals):
        bi = jnp.zeros_like(vals[0])
        bv = vals[0]
        for i in range(1, K):
            upd = vals[i] > bv
            bi = jnp.where(upd, float(i), bi)
            bv = jnp.where(upd, vals[i], bv)
        return bi

    i1 = first_max(cs)
    cs2 = [jnp.where(i1 == float(i), -jnp.inf, cs[i]) for i in range(K)]
    i2 = first_max(cs2)
    a1 = sum(jnp.where(i1 == float(i), ca[i], 0.0) for i in range(K))
    a2 = sum(jnp.where(i2 == float(i), ca[i], 0.0) for i in range(K))
    den = a1 + a2
    return gb * K + i1, gb * K + i2, a1 / den, a2 / den


def _post_kernel(x_ref, mod_ref, fo_ref, ml_ref, sw_ref, wof_ref, wom_ref, wos_ref, n2g_ref,
                 rwh_ref, rwl_ref, rb_ref, tri_ref, ones_ref, xn_ref, h2_ref, rc_ref, rr_ref, cnt_ref):
    m = mod_ref[...]
    parts = []
    for r0 in range(0, TM, POST_CHAIN_ROWS):
        rs = slice(r0, r0 + POST_CHAIN_ROWS)
        mix = (_dot(fo_ref[rs, :], wof_ref[...]) + _dot(ml_ref[rs, :], wom_ref[...])
               + _dot(sw_ref[rs, :], wos_ref[...]))
        xn = x_ref[rs, :] + m[2:3] * mix
        xn_ref[rs, :] = xn
        h2 = _rms(xn, D_MODEL) * n2g_ref[...] * (1.0 + m[4:5]) + m[3:4]
        h2_ref[rs, :] = h2.astype(BF16)
        hh = h2.astype(BF16)
        hl = (h2 - hh.astype(F32)).astype(BF16)
        both = _dot(hh, rwl_ref[...])
        parts.append(both[:, :LANES] + (_dot(hl, rwh_ref[...]) + both[:, LANES:]))
    logits = jnp.concatenate(parts, axis=0)
    lt = logits.T[0:N_EXPERTS, :]
    aff_t = jax.nn.sigmoid(lt)
    sel_t = aff_t + rb_ref[...]
    sel = [sel_t[e:e + 1, :] for e in range(N_EXPERTS)]
    aff = [aff_t[e:e + 1, :] for e in range(N_EXPERTS)]
    e1, e2, w1, w2 = _route_rows(sel, aff)

    eio = lax.broadcasted_iota(jnp.int32, (N_EXPERTS, TM), 0).astype(F32)
    oh = jnp.concatenate([jnp.where(eio == e1, 1.0, 0.0), jnp.where(eio == e2, 1.0, 0.0)], axis=1)
    ohb = oh.astype(BF16)
    rank = _dot(ohb, tri_ref[...])
    cnt = _dot(ohb, ones_ref[...]).astype(jnp.int32)
    cnt8 = jnp.left_shift(jnp.right_shift(cnt + (CHUNK - 1), CHUNK_LOG2), CHUNK_LOG2)
    cnt_ref[...] = cnt8
    cnt8f = cnt8.astype(F32)
    off = jnp.zeros((1, 1), F32)
    slot = jnp.zeros((1, 2 * TM), F32)
    for e in range(N_EXPERTS):
        slot = slot + oh[e:e + 1, :] * (off + rank[e:e + 1, :])
        off = off + cnt8f[e:e + 1, 0:1]
    s0, s1 = slot[:, :TM], slot[:, TM:]

    sub = lax.broadcasted_iota(jnp.int32, (8, TM), 0)
    rr_ref[...] = jnp.where(sub == 0, s0, jnp.where(sub == 1, s1, 0.0))
    blk = jnp.where(sub == 0, e1, jnp.where(sub == 1, e2, jnp.where(sub == 2, w1, jnp.where(
        sub == 3, w2, jnp.where(sub == 4, s0, jnp.where(sub == 5, s1, 0.0))))))
    rows = jnp.concatenate([blk, jnp.zeros((LANES - 8, TM), F32)], axis=0)
    rc_ref[...] = rows.T


def _post(xc, mod, fo, ml, sw, wl, n2g, rw_hi, rw_lo, rb_col, nt, nx_tiles, tiles_per_batch):
    rows = nt * TM

    def bidx(i):
        return jnp.where(i < nx_tiles, i // tiles_per_batch, mod.shape[0] - 1)

    def full(a):
        return pl.BlockSpec(a.shape, lambda i: (0,) * a.ndim)

    row = lambda w: pl.BlockSpec((TM, w), lambda i: (i, 0))
    pair = np.arange(2 * TM)
    tri = jnp.asarray(pair[:, None] < pair[None, :], BF16)
    ones = jnp.ones((2 * TM, LANES), BF16)
    consts = [wl["wo_f"], wl["wo_m"], wl["wo_s"], n2g, rw_hi, rw_lo, rb_col, tri, ones]
    return pl.pallas_call(
        _post_kernel,
        grid=(nt,),
        in_specs=[row(D_MODEL), pl.BlockSpec((None, 8, D_MODEL), lambda i: (bidx(i), 0, 0)),
                  row(D_FNET), row(MLA_HEADS * MLA_V), row(SWA_HEADS * SWA_HEAD_DIM)]
                 + [full(a) for a in consts],
        out_specs=[row(D_MODEL), row(D_MODEL), row(LANES),
                   pl.BlockSpec((None, 8, TM), lambda i: (i, 0, 0)),
                   pl.BlockSpec((None, N_EXPERTS, LANES), lambda i: (i, 0, 0))],
        out_shape=[jax.ShapeDtypeStruct((rows, D_MODEL), F32),
                   jax.ShapeDtypeStruct((rows, D_MODEL), BF16),
                   jax.ShapeDtypeStruct((rows, LANES), F32),
                   jax.ShapeDtypeStruct((nt, 8, TM), F32),
                   jax.ShapeDtypeStruct((nt, N_EXPERTS, LANES), jnp.int32)],
        compiler_params=_cparams(("arbitrary",)),
        name="post",
    )(xc, mod, fo, ml, sw, *consts)


def _moe_tables(cnt8, te):
    nt = cnt8.shape[0]
    tile_prefix = jnp.cumsum(cnt8, axis=0) - cnt8
    tot = jnp.sum(cnt8, axis=0)
    tot_e = ((tot + te - 1) // te) * te
    goff = jnp.cumsum(tot_e) - tot_e
    dbase = goff[None, :] + tile_prefix
    nch = cnt8 // CHUNK
    cum = jnp.cumsum(nch, axis=1)
    k = jnp.arange(MAX_CHUNKS, dtype=jnp.int32)[None, :, None]
    owns = jnp.logical_and(k >= (cum - nch)[:, None, :], k < cum[:, None, :])
    dst = jnp.sum(jnp.where(owns, dbase[:, None, :] + CHUNK * (k - (cum - nch)[:, None, :]), 0), axis=-1)
    nchunks = cum[:, -1]
    padch = (tot_e - tot) // CHUNK
    cump = jnp.cumsum(padch)
    kp = jnp.arange(N_EXPERTS * (te // CHUNK), dtype=jnp.int32)[:, None]
    pown = jnp.logical_and(kp >= (cump - padch)[None, :], kp < cump[None, :])
    pdst = jnp.sum(jnp.where(pown, (goff + tot)[None, :] + CHUNK * (kp - (cump - padch)[None, :]), 0), axis=-1)
    npad = cump[-1]
    ntile_cum = jnp.cumsum(tot_e // te)
    nact = ntile_cum[-1]
    return dict(dst=dst.reshape(-1).astype(jnp.int32), nchunks=nchunks.astype(jnp.int32),
                pdst=pdst.astype(jnp.int32), npad=npad.reshape(1).astype(jnp.int32),
                ntile_cum=ntile_cum.astype(jnp.int32), nact=nact.reshape(1).astype(jnp.int32))


def _dispatch_kernel(dst_ref, nch_ref, pdst_ref, npad_ref, h_ref, rr_ref, xs_ref, sbuf, zbuf, sem, zsem):
    i = pl.program_id(0)
    nt = pl.num_programs(0)
    slot = i % 2

    def chunk_copy(sl, k, d):
        return pltpu.make_async_copy(
            sbuf.at[sl, pl.ds(pl.multiple_of(k * CHUNK, CHUNK), CHUNK), :],
            xs_ref.at[pl.ds(pl.multiple_of(d, CHUNK), CHUNK), :], sem.at[sl])

    def wait_tile(t, sl):
        n = nch_ref[t] * CHUNK
        pltpu.make_async_copy(sbuf.at[sl, pl.ds(0, n), :], xs_ref.at[pl.ds(0, n), :], sem.at[sl]).wait()

    def pad_copy(d):
        return pltpu.make_async_copy(zbuf, xs_ref.at[pl.ds(pl.multiple_of(d, CHUNK), CHUNK), :], zsem)

    @pl.when(i == 0)
    def _():
        zbuf[...] = jnp.zeros_like(zbuf)

        def start(k, c):
            pad_copy(pdst_ref[k]).start()
            return c
        lax.fori_loop(0, npad_ref[0], start, 0)

        def wait(k, c):
            pad_copy(0).wait()
            return c
        lax.fori_loop(0, npad_ref[0], wait, 0)

    @pl.when(i >= 2)
    def _():
        wait_tile(i - 2, slot)

    rr = rr_ref[...]
    sio = lax.broadcasted_iota(jnp.int32, (NSLOT, TM), 0).astype(F32)
    psel = jnp.where(jnp.logical_or(sio == rr[0:1, :], sio == rr[1:2, :]), 1.0, 0.0).astype(BF16)
    sbuf[slot] = _dot(psel, h_ref[...]).astype(BF16)

    for k in range(MIN_CHUNKS):
        chunk_copy(slot, k, dst_ref[i * MAX_CHUNKS + k]).start(priority=k % 2)

    def issue(k, c):
        chunk_copy(slot, k, dst_ref[i * MAX_CHUNKS + k]).start()
        return c
    lax.fori_loop(MIN_CHUNKS, nch_ref[i], issue, 0)

    @pl.when(i == nt - 1)
    def _():
        wait_tile(i, slot)

        @pl.when(i >= 1)
        def _():
            wait_tile(i - 1, 1 - slot)


def _dispatch(h2, rr, tb, rows_sorted):
    nt = rr.shape[0]
    return pl.pallas_call(
        _dispatch_kernel,
        grid_spec=pltpu.PrefetchScalarGridSpec(
            num_scalar_prefetch=4,
            grid=(nt,),
            in_specs=[pl.BlockSpec((TM, D_MODEL), lambda i, *_: (i, 0)),
                      pl.BlockSpec((None, 8, TM), lambda i, *_: (i, 0, 0))],
            out_specs=pl.BlockSpec(memory_space=pl.ANY),
            scratch_shapes=[pltpu.VMEM((2, NSLOT, D_MODEL), BF16), pltpu.VMEM((CHUNK, D_MODEL), BF16),
                            pltpu.SemaphoreType.DMA((2,)), pltpu.SemaphoreType.DMA(())]),
        out_shape=jax.ShapeDtypeStruct((rows_sorted, D_MODEL), BF16),
        compiler_params=_cparams(("arbitrary",)),
        name="moe_dispatch",
    )(tb["dst"], tb["nchunks"], tb["pdst"], tb["npad"], h2, rr)


def _expert_kernel(te_ref, na_ref, x_ref, wg_ref, wu_ref, wd_ref, o_ref, wgb, wub, wdb):
    j = pl.program_id(0)
    active = j < na_ref[0]
    fresh = jnp.logical_or(j == 0, te_ref[j] != te_ref[jnp.maximum(j - 1, 0)])

    @pl.when(jnp.logical_and(active, fresh))
    def _():
        wgb[...] = wg_ref[...].astype(BF16)
        wub[...] = wu_ref[...].astype(BF16)
        wdb[...] = wd_ref[...].astype(BF16)

    @pl.when(active)
    def _():
        x = x_ref[...]
        a = _dot(x, wgb[...])
        a = a * jax.nn.sigmoid(a) * _dot(x, wub[...])
        o_ref[...] = _dot(a.astype(BF16), wdb[...]).astype(BF16)


def _experts(xs, wg, wu, wd, layer, tb, te):
    nte = xs.shape[0] // te
    jj = jnp.minimum(jnp.arange(nte, dtype=jnp.int32), tb["nact"][0] - 1)
    tile_e = jnp.sum(jj[:, None] >= tb["ntile_cum"][None, :], axis=-1).astype(jnp.int32)

    def tmap(j, te_ref, na):
        return (jnp.minimum(j, na[0] - 1), 0)

    def wmap(j, te_ref, na):
        return (layer, te_ref[j], 0, 0)

    return pl.pallas_call(
        _expert_kernel,
        grid_spec=pltpu.PrefetchScalarGridSpec(
            num_scalar_prefetch=2,
            grid=(nte,),
            in_specs=[pl.BlockSpec((te, D_MODEL), tmap),
                      pl.BlockSpec((None, None, D_MODEL, D_EXPERT), wmap),
                      pl.BlockSpec((None, None, D_MODEL, D_EXPERT), wmap),
                      pl.BlockSpec((None, None, D_EXPERT, D_MODEL), wmap)],
            out_specs=pl.BlockSpec((te, D_MODEL), tmap),
            scratch_shapes=[pltpu.VMEM((D_MODEL, D_EXPERT), BF16), pltpu.VMEM((D_MODEL, D_EXPERT), BF16),
                            pltpu.VMEM((D_EXPERT, D_MODEL), BF16)]),
        out_shape=jax.ShapeDtypeStruct(xs.shape, BF16),
        compiler_params=_cparams(("arbitrary",)),
        name="moe_experts",
    )(tile_e, tb["nact"], xs, wg, wu, wd)


def _combine_kernel(dst_ref, nch_ref, xn_ref, mod_ref, rc_ref, ys_ref, o_ref, gbuf, sem):
    i = pl.program_id(0)
    nt = pl.num_programs(0)
    slot = i % 2

    def chunk_copy(sl, k, d):
        return pltpu.make_async_copy(
            ys_ref.at[pl.ds(pl.multiple_of(d, CHUNK), CHUNK), :],
            gbuf.at[sl, pl.ds(pl.multiple_of(k * CHUNK, CHUNK), CHUNK), :], sem.at[sl])

    def issue_tile(t, sl):
        for k in range(MIN_CHUNKS):
            chunk_copy(sl, k, dst_ref[t * MAX_CHUNKS + k]).start(priority=k % 2)

        def body(k, c):
            chunk_copy(sl, k, dst_ref[t * MAX_CHUNKS + k]).start()
            return c
        lax.fori_loop(MIN_CHUNKS, nch_ref[t], body, 0)

    @pl.when(i == 0)
    def _():
        gbuf[...] = jnp.zeros_like(gbuf)
        issue_tile(0, 0)

    @pl.when(i + 1 < nt)
    def _():
        issue_tile(i + 1, 1 - slot)

    n = nch_ref[i] * CHUNK
    pltpu.make_async_copy(ys_ref.at[pl.ds(0, n), :], gbuf.at[slot, pl.ds(0, n), :], sem.at[slot]).wait()

    g = gbuf[slot]
    rc = rc_ref[...]
    lio = lax.broadcasted_iota(jnp.int32, (TM, NSLOT), 1).astype(F32)
    p0 = jnp.where(lio == rc[:, 4:5], 1.0, 0.0).astype(BF16)
    p1 = jnp.where(lio == rc[:, 5:6], 1.0, 0.0).astype(BF16)
    y = rc[:, 2:3] * _dot(p0, g) + rc[:, 3:4] * _dot(p1, g)
    o_ref[...] = xn_ref[...] + mod_ref[5:6, :] * y


def _combine(xn, mod, rc, ys, tb, nx_tiles, tiles_per_batch):
    rows = xn.shape[0]
    nt = rows // TM

    def bidx(i, *_):
        return (jnp.where(i < nx_tiles, i // tiles_per_batch, mod.shape[0] - 1), 0, 0)

    return pl.pallas_call(
        _combine_kernel,
        grid_spec=pltpu.PrefetchScalarGridSpec(
            num_scalar_prefetch=2,
            grid=(nt,),
            in_specs=[pl.BlockSpec((TM, D_MODEL), lambda i, *_: (i, 0)),
                      pl.BlockSpec((None, 8, D_MODEL), bidx),
                      pl.BlockSpec((TM, LANES), lambda i, *_: (i, 0)),
                      pl.BlockSpec(memory_space=pl.ANY)],
            out_specs=pl.BlockSpec((TM, D_MODEL), lambda i, *_: (i, 0)),
            scratch_shapes=[pltpu.VMEM((2, NSLOT, D_MODEL), BF16), pltpu.SemaphoreType.DMA((2,))]),
        out_shape=jax.ShapeDtypeStruct((rows, D_MODEL), F32),
        compiler_params=_cparams(("arbitrary",)),
        name="moe_combine",
    )(tb["dst"], tb["nchunks"], xn, mod, rc, ys)


def _moe(h2, rc, rr, cnt, wg, wu, wd, layer, xn, mod, nx_tiles, tiles_per_batch):
    nt = rr.shape[0]
    te = TE
    max_rows = 2 * nt * TM + (CHUNK - 1) * N_EXPERTS * nt + N_EXPERTS * (te - CHUNK)
    rows_sorted = ((max_rows + te - 1) // te) * te
    tb = _moe_tables(cnt[:, :, 0], te)
    xs = _dispatch(h2, rr, tb, rows_sorted)
    ys = _experts(xs, wg, wu, wd, layer, tb, te)
    return _combine(xn, mod, rc, ys, tb, nx_tiles, tiles_per_batch)


def _rope_tables(S, C):
    t = jnp.arange(S)
    rows, cols = (t // GRID_W).astype(F32), (t % GRID_W).astype(F32)

    def axis_tabs(d_rot, lane0, width):
        d_axis = d_rot // 2
        inv = ROPE_THETA ** (-jnp.arange(0, d_axis, 2, dtype=F32) / d_axis)
        ar, ac = rows[:, None] * inv, cols[:, None] * inv
        ang = jnp.concatenate([ar, ar, ac, ac], axis=-1)
        q = d_rot // 4
        first = np.concatenate([np.ones(q), np.zeros(q), np.ones(q), np.zeros(q)]).astype(np.float32)
        pad = ((0, C), (lane0, width - lane0 - d_rot))
        cos = jnp.pad(jnp.cos(ang) - 1.0, pad) + 1.0
        sa = jnp.pad(-jnp.sin(ang) * first, pad)
        sb = jnp.pad(jnp.sin(ang) * (1.0 - first), pad)
        return cos, sa, sb

    cm, sam, sbm = axis_tabs(MLA_ROPE, MLA_NOPE, LANES)
    cs, sas, sbs = axis_tabs(SWA_HEAD_DIM, 0, SWA_HEAD_DIM)
    tile2 = lambda a: jnp.concatenate([a, a], axis=1)
    return dict(cm=cm, sam=sam, sbm=sbm, cs=tile2(cs), sas=tile2(sas), sbs=tile2(sbs))


_TWO_PI_HI = float(np.float32(2.0 * np.pi))
_TWO_PI_LO = float(np.float32(2.0 * np.pi - np.float64(np.float32(2.0 * np.pi))))


def _dft_mats(N):
    k = jnp.arange(N, dtype=jnp.int32)

    def cos_sin(rows, period):
        frac = ((rows[:, None] * k[None, :]) % period).astype(F32) / period
        ang = _TWO_PI_HI * frac + _TWO_PI_LO * frac
        return jnp.cos(ang), jnp.sin(ang)

    if N <= 4 * FNET_CH:
        return cos_sin(k, N)
    A = N // FNET_CH
    c1, s1 = cos_sin(jnp.arange(A, dtype=jnp.int32), A)
    c2, s2 = cos_sin(jnp.arange(FNET_CH, dtype=jnp.int32), N)
    c = c1[:, None, :] * c2[None, :, :] - s1[:, None, :] * s2[None, :, :]
    s = s1[:, None, :] * c2[None, :, :] + c1[:, None, :] * s2[None, :, :]
    return c.reshape(N, N), s.reshape(N, N)


def _dft64_blocks():
    c, s = _dft_mats(FNET_CH)
    eye = jnp.eye(FNET_GROUPS, dtype=F32)
    return jnp.concatenate([jnp.kron(eye, c), jnp.kron(eye, s)], axis=1)


def _layer_weights(l, w_in, fnet_w, mla_cq_g, mla_ckv_g, mla_w_uq, mla_w_uk, mla_w_uv, mla_q_g, mla_k_g,
                   swa_q_g, swa_k_g, swa_sink, w_out):
    D = D_MODEL
    wi = w_in[l]
    o_kr = D_FNET + MLA_Q_RANK + MLA_KV_RANK
    o_qs = o_kr + MLA_ROPE
    o_ks = o_qs + SWA_HEADS * SWA_HEAD_DIM
    o_vs = o_ks + SWA_KV_HEADS * SWA_HEAD_DIM
    order = np.array(SWA_HEAD_ORDER)
    w_qs = wi[:, o_qs:o_ks].reshape(D, SWA_HEADS, SWA_HEAD_DIM)[:, order].reshape(D, -1)
    z = lambda n: jnp.zeros((D, n), F32)
    win = jnp.concatenate([wi[:, :o_kr], w_qs, wi[:, o_ks:o_vs], wi[:, o_vs:],
                           z(MLA_NOPE), wi[:, o_kr:o_qs], z(LANES - MLA_QK)], axis=1)
    pad_slot = lambda w, d: jnp.pad(w.reshape(w.shape[0], MLA_HEADS, d),
                                    ((0, 0), (0, 0), (0, HEAD_SLOT - d))).reshape(w.shape[0], -1)
    wo = w_out[l]
    o_m = D_FNET
    o_s = D_FNET + MLA_HEADS * MLA_V
    wo_s = wo[o_s:].reshape(SWA_HEADS, SWA_HEAD_DIM, D)[order].reshape(-1, D)
    fw = fnet_w[l]
    wblk = jnp.zeros((D_FNET, D_FNET), F32)
    for g in range(FNET_GROUPS):
        wblk = wblk.at[g * FNET_CH:(g + 1) * FNET_CH, g * FNET_CH:(g + 1) * FNET_CH].set(fw[g])
    pad_g = lambda g: jnp.pad(g, (0, HEAD_SLOT - MLA_QK)).reshape(1, HEAD_SLOT)
    return dict(
        w_in=win.astype(BF16),
        cq_g=mla_cq_g[l].reshape(1, -1), ckv_g=mla_ckv_g[l].reshape(1, -1),
        w_uq=pad_slot(mla_w_uq[l], MLA_QK).astype(BF16),
        w_uk=pad_slot(mla_w_uk[l], MLA_NOPE).astype(BF16),
        w_uv=pad_slot(mla_w_uv[l], MLA_V).astype(BF16),
        mq_g=pad_g(mla_q_g[l]), mk_g=pad_g(mla_k_g[l]),
        sq_g=jnp.tile(swa_q_g[l], 2).reshape(1, LANES), sk_g=jnp.tile(swa_k_g[l], 2).reshape(1, LANES),
        sink=swa_sink[l].reshape(1, SWA_HEADS),
        wo_f=wo[:o_m].astype(BF16), wo_m=wo[o_m:o_s].astype(BF16), wo_s=wo_s.astype(BF16),
        fnet=wblk,
    )


def kernel(x, c, ctx, c_ctx, ada_w, ada_b, norm1_g, norm2_g, w_in, fnet_w, mla_cq_g, mla_ckv_g, mla_w_uq,
           mla_w_uk, mla_w_uv, mla_q_g, mla_k_g, swa_q_g, swa_k_g, swa_sink, w_out, router_w, router_b,
           exp_w_gate, exp_w_up, exp_w_down):
    B, S, D = x.shape
    C = ctx.shape[1]
    L = ada_w.shape[0]
    assert D == D_MODEL and S % 512 == 0 and C == TM and S % TM == 0
    nx_tiles = B * S // TM
    nt_all = nx_tiles + B * C // TM
    tiles_per_batch = S // TM

    prep_tm = PREP_TM if (B * C) % PREP_TM == 0 and S % PREP_TM == 0 else TM
    tabs = _rope_tables(S, prep_tm)
    tabs["dft64"] = _dft64_blocks().astype(BF16)
    dft = {n: tuple(m.astype(BF16) for m in _dft_mats(n)) for n in (S, C)}
    fscale = {n: lax.rsqrt(jnp.full((), n * FNET_CH, F32)) for n in (S, C)}

    nmod = 16
    cvec = jnp.concatenate([c, c_ctx[None, :], jnp.zeros((nmod - B - 1, D), F32)], axis=0)
    mod_all = _adaln(cvec, ada_w, ada_b)
    mod_all = mod_all[:, :B + 1].reshape(L, B + 1, 6, D)
    mod_all = jnp.pad(mod_all, ((0, 0), (0, 0), (0, 2), (0, 0)))

    rw = jnp.pad(router_w, ((0, 0), (0, LANES - N_EXPERTS)))
    rw_hi = rw.astype(BF16)
    rw_lo = jnp.concatenate([rw_hi, (rw - rw_hi.astype(F32)).astype(BF16)], axis=1)
    rb_col = router_b.reshape(N_EXPERTS, 1)

    xc = jnp.concatenate([x.reshape(B * S, D), ctx.reshape(B * C, D)], axis=0)
    for l in range(L):
        last = l == L - 1
        wl = _layer_weights(l, w_in, fnet_w, mla_cq_g, mla_ckv_g, mla_w_uq, mla_w_uk, mla_w_uv, mla_q_g,
                            mla_k_g, swa_q_g, swa_k_g, swa_sink, w_out)
        mod = mod_all[l]
        pr = _prep(xc, mod, norm1_g[l].reshape(1, D), wl, tabs, B * S, S, prep_tm)
        fo = _fourier(dft[S][0], dft[S][1], pr["pp"], (wl["fnet"] * fscale[S]).astype(BF16), B, S, 0)
        ml = _mla_attend(pr["qm"], pr["km"], pr["vm"], B, S, C)
        sw = _swa_attend(wl["sink"], pr["qs"], pr["ks"], pr["vs"], B, S, C)
        if not last:
            fo = _fourier(dft[C][0], dft[C][1], pr["pp"], (wl["fnet"] * fscale[C]).astype(BF16), B, C, B * S,
                          prev=fo)
            ml = _mla_attend(pr["qm"], pr["km"], pr["vm"], B, S, C, prev=ml)
            sw = _swa_attend(wl["sink"], pr["qs"], pr["ks"], pr["vs"], B, S, C, prev=sw)
        nt = nx_tiles if last else nt_all
        xn, h2, rc, rr, cnt = _post(xc, mod, fo, ml, sw, wl, norm2_g[l].reshape(1, D), rw_hi, rw_lo, rb_col,
                                    nt, nx_tiles, tiles_per_batch)
        xc = _moe(h2, rc, rr, cnt, exp_w_gate, exp_w_up, exp_w_down, l, xn, mod, nx_tiles, tiles_per_batch)
    return xc[:B * S].reshape(B, S, D)
```

```python
import functools

import numpy as np
import jax
import jax.numpy as jnp
from jax import lax
from jax.experimental import pallas as pl
from jax.experimental.pallas import tpu as pltpu

F32 = jnp.float32
BF16 = jnp.bfloat16

D_MODEL = 1024
GRID_W = 64
FNET_GROUPS = 4
FNET_CH = 64
D_FNET = FNET_GROUPS * FNET_CH
MLA_HEADS = 6
MLA_Q_RANK = 256
MLA_KV_RANK = 128
MLA_NOPE = 64
MLA_ROPE = 32
MLA_QK = MLA_NOPE + MLA_ROPE
MLA_V = 64
SWA_HEADS = 6
SWA_KV_HEADS = 2
SWA_HEAD_DIM = 64
WINDOW = 128
BLOCK = 128
N_EXPERTS = 16
N_EXPERT_GROUPS = 4
EXPERTS_PER_GROUP = 4
D_EXPERT = 512
ROPE_THETA = 10000.0
EPS = 1e-6

LANES = 128
TM = 256
HEAD_SLOT = LANES
CHUNK = 16
CHUNK_LOG2 = 4
NSLOT = 768
MAX_CHUNKS = NSLOT // CHUNK
TE = 512
PREP_CHAIN_ROWS = 128
POST_CHAIN_ROWS = TM
PREP_TM = 256
MLA_CHAINS = 8
SWA_QB = 8
MIN_CHUNKS = 2 * TM // CHUNK
VMEM_LIMIT = 48 * 1024 * 1024

P_UF = 0
P_CQ = 256
P_CKV = 512
P_QS = 640
P_KS = 1024
P_VS = 1152
P_KR = 1280
P_TOT = 1408
SWA_HEAD_ORDER = (0, 3, 1, 4, 2, 5)


def _dot(a, b):
    return jnp.dot(a, b, preferred_element_type=F32)


def _dot_nt(a, b):
    return lax.dot_general(a, b, (((1,), (1,)), ((), ())), preferred_element_type=F32)


def _rms(x, n):
    return x * lax.rsqrt(jnp.sum(x * x, axis=-1, keepdims=True) / n + EPS)


def _rope(x, c, sa, sb, half):
    n = x.shape[-1]
    return x * c + pltpu.roll(x, n - half, 1) * sa + pltpu.roll(x, half, 1) * sb


def _cparams(sem):
    return pltpu.CompilerParams(dimension_semantics=sem, vmem_limit_bytes=VMEM_LIMIT)


def _adaln_kernel(c_ref, w_ref, b_ref, o_ref):
    c = c_ref[...]
    sc = c * jax.nn.sigmoid(c)
    w = w_ref[...]
    s_hi = sc.astype(BF16)
    s_lo = (sc - s_hi.astype(F32)).astype(BF16)
    w_hi = w.astype(BF16)
    w_lo = (w - w_hi.astype(F32)).astype(BF16)
    o_ref[...] = _dot(s_hi, w_hi) + (_dot(s_lo, w_hi) + _dot(s_hi, w_lo)) + b_ref[...]


def _adaln(cvec, ada_w, ada_b):
    L, D, N6 = ada_w.shape
    R = cvec.shape[0]
    bn = 512
    return pl.pallas_call(
        _adaln_kernel,
        grid=(L, N6 // bn),
        in_specs=[pl.BlockSpec((R, D), lambda l, j: (0, 0)),
                  pl.BlockSpec((None, D, bn), lambda l, j: (l, 0, j)),
                  pl.BlockSpec((None, 1, bn), lambda l, j: (l, 0, j))],
        out_specs=pl.BlockSpec((None, R, bn), lambda l, j: (l, 0, j)),
        out_shape=jax.ShapeDtypeStruct((L, R, N6), F32),
        compiler_params=_cparams(("arbitrary", "arbitrary")),
        name="adaln",
    )(cvec, ada_w, ada_b.reshape(L, 1, N6))


def _prep_kernel(x_ref, mod_ref, n1g_ref, win_ref, cqg_ref, ckvg_ref, wuq_ref, wuk_ref, wuv_ref,
                 mqg_ref, mkg_ref, sqg_ref, skg_ref, cm_ref, sam_ref, sbm_ref, cs_ref, sas_ref, sbs_ref,
                 dft_ref, pp_ref, qm_ref, km_ref, vm_ref, qs_ref, ks_ref, vs_ref):
    m = mod_ref[...]
    slot_lane = jnp.bitwise_and(lax.broadcasted_iota(jnp.int32, (1, MLA_HEADS * HEAD_SLOT), 1), HEAD_SLOT - 1)
    vone = jnp.where(slot_lane == MLA_V, 1.0, 0.0)

    li = lax.broadcasted_iota(jnp.int32, (LANES, LANES), 0) < SWA_HEAD_DIM
    lj = lax.broadcasted_iota(jnp.int32, (LANES, LANES), 1) < SWA_HEAD_DIM
    head_ones = jnp.where(li == lj, 1.0, 0.0).astype(BF16)

    def head_norm(slab, g):
        sq = slab * slab
        sq_hi = sq.astype(BF16)
        sq_lo = (sq - sq_hi.astype(F32)).astype(BF16)
        ss = _dot(sq_hi, head_ones) + _dot(sq_lo, head_ones)
        return slab * lax.rsqrt(ss * (1.0 / SWA_HEAD_DIM) + EPS) * g

    rows = PREP_CHAIN_ROWS
    for ch in range(x_ref.shape[0] // rows):
        rs = slice(ch * rows, (ch + 1) * rows)
        x = x_ref[rs, :]
        h = _rms(x, D_MODEL) * n1g_ref[...] * (1.0 + m[1:2]) + m[0:1]
        p = _dot(h.astype(BF16), win_ref[...])

        u = p[:, P_UF:P_UF + D_FNET].astype(BF16)
        pp_ref[rs, :] = _dot(u, dft_ref[...]).astype(BF16)

        cm, sam, sbm = cm_ref[rs, :], sam_ref[rs, :], sbm_ref[rs, :]
        cs, sas, sbs = cs_ref[rs, :], sas_ref[rs, :], sbs_ref[rs, :]

        cq = _rms(p[:, P_CQ:P_CQ + MLA_Q_RANK], MLA_Q_RANK) * cqg_ref[...]
        qraw = _dot(cq.astype(BF16), wuq_ref[...])
        mqg = mqg_ref[...]
        for hh in range(MLA_HEADS):
            sl = slice(hh * HEAD_SLOT, (hh + 1) * HEAD_SLOT)
            qn = _rms(qraw[:, sl], MLA_QK) * mqg
            qm_ref[rs, sl] = (_rope(qn, cm, sam, sbm, MLA_ROPE // 4) * (MLA_QK ** -0.5)).astype(BF16)

        ckv = (_rms(p[:, P_CKV:P_CKV + MLA_KV_RANK], MLA_KV_RANK) * ckvg_ref[...]).astype(BF16)
        knope = _dot(ckv, wuk_ref[...])
        vm_ref[rs, :] = (_dot(ckv, wuv_ref[...]) + vone).astype(BF16)
        kr = p[:, P_KR:P_KR + LANES]
        mkg = mkg_ref[...]
        kr_ss = jnp.sum(kr * kr, axis=-1, keepdims=True)
        kr_rot = _rope(kr * mkg, cm, sam, sbm, MLA_ROPE // 4)
        for hh in range(MLA_HEADS):
            sl = slice(hh * HEAD_SLOT, (hh + 1) * HEAD_SLOT)
            kn_h = knope[:, sl]
            r = lax.rsqrt((jnp.sum(kn_h * kn_h, axis=-1, keepdims=True) + kr_ss) / MLA_QK + EPS)
            km_ref[rs, sl] = ((kn_h * mkg + kr_rot) * r).astype(BF16)

        sqg = sqg_ref[...]
        for s in range(SWA_HEADS // 2):
            sl = slice(P_QS + s * LANES, P_QS + (s + 1) * LANES)
            qn = head_norm(p[:, sl], sqg)
            qs_ref[rs, s * LANES:(s + 1) * LANES] = (
                _rope(qn, cs, sas, sbs, SWA_HEAD_DIM // 4) * (SWA_HEAD_DIM ** -0.5)).astype(BF16)
        kn = head_norm(p[:, P_KS:P_KS + LANES], skg_ref[...])
        ks_ref[rs, :] = _rope(kn, cs, sas, sbs, SWA_HEAD_DIM // 4).astype(BF16)
        vs_ref[rs, :] = p[:, P_VS:P_VS + LANES].astype(BF16)


def _prep(xc, mod, n1g, wl, tabs, n_latent, seq, tm):
    T = xc.shape[0]
    nt = T // tm
    nx_tiles = n_latent // tm
    tiles_per_batch = seq // tm

    def bidx(i):
        return jnp.where(i < nx_tiles, i // tiles_per_batch, mod.shape[0] - 1)

    def ridx(i):
        return jnp.where(i < nx_tiles, i % tiles_per_batch, tiles_per_batch)

    def full(a):
        return pl.BlockSpec(a.shape, lambda i: (0,) * a.ndim)

    tab_spec = pl.BlockSpec((tm, LANES), lambda i: (ridx(i), 0))
    row = lambda w: pl.BlockSpec((tm, w), lambda i: (i, 0))
    consts = [n1g, wl["w_in"], wl["cq_g"], wl["ckv_g"], wl["w_uq"], wl["w_uk"], wl["w_uv"],
              wl["mq_g"], wl["mk_g"], wl["sq_g"], wl["sk_g"]]
    outs = [("pp", 2 * D_FNET), ("qm", MLA_HEADS * HEAD_SLOT), ("km", MLA_HEADS * HEAD_SLOT),
            ("vm", MLA_HEADS * HEAD_SLOT), ("qs", SWA_HEADS * SWA_HEAD_DIM), ("ks", LANES), ("vs", LANES)]
    res = pl.pallas_call(
        _prep_kernel,
        grid=(nt,),
        in_specs=[row(D_MODEL), pl.BlockSpec((None, 8, D_MODEL), lambda i: (bidx(i), 0, 0))]
                 + [full(a) for a in consts] + [tab_spec] * 6 + [full(tabs["dft64"])],
        out_specs=[row(w) for _, w in outs],
        out_shape=[jax.ShapeDtypeStruct((T, w), BF16) for _, w in outs],
        compiler_params=_cparams(("arbitrary",)),
        name="prep",
    )(xc, mod, *consts, tabs["cm"], tabs["sam"], tabs["sbm"], tabs["cs"], tabs["sas"], tabs["sbs"],
      tabs["dft64"])
    return dict(zip([n for n, _ in outs], res))


def _mla_kernel(*refs, with_x):
    if with_x:
        q_ref, kx_ref, kc_ref, vx_ref, vc_ref, o_ref = refs
    else:
        q_ref, kc_ref, vc_ref, o_ref = refs
    lane = lax.broadcasted_iota(jnp.int32, (1, LANES), 1)
    rows = q_ref.shape[0] // MLA_CHAINS if with_x else q_ref.shape[0]
    for r0 in range(0, q_ref.shape[0], rows):
        rs = slice(r0, r0 + rows)
        outs = []
        for hh in range(2):
            sl = slice(hh * HEAD_SLOT, (hh + 1) * HEAD_SLOT)
            q = q_ref[rs, sl]
            sc = _dot_nt(q, kc_ref[:, sl])
            m = jnp.max(sc, axis=-1, keepdims=True)
            if with_x:
                sx = _dot_nt(q, kx_ref[:, sl])
                m = jnp.maximum(m, jnp.max(sx, axis=-1, keepdims=True))
                px = jnp.exp(sx - m)
            pc = jnp.exp(sc - m)
            o = _dot(pc.astype(BF16), vc_ref[:, sl])
            if with_x:
                o = o + _dot(px.astype(BF16), vx_ref[:, sl])
            outs.append(o / o[:, MLA_V:MLA_V + 1])
        o_ref[rs, :] = jnp.where(lane < MLA_V, outs[0], pltpu.roll(outs[1], MLA_V, 1)).astype(BF16)


def _ctx_rows_kernel(kernel_fn, *refs, **kw):
    kernel_fn(*refs[:-2], refs[-1], **kw)


def _mla_attend(qm, km, vm, B, S, C, prev=None):
    T = qm.shape[0]
    npair = MLA_HEADS // 2
    with_x = prev is None
    if with_x:
        tq = 2048
        nq = S // tq
        qmap = lambda b, p, i: (b * nq + i, p)
        in_specs = [pl.BlockSpec((tq, 2 * HEAD_SLOT), qmap),
                    pl.BlockSpec((S, 2 * HEAD_SLOT), lambda b, p, i: (b, p)),
                    pl.BlockSpec((C, 2 * HEAD_SLOT), lambda b, p, i: (B * S // C + b, p)),
                    pl.BlockSpec((S, 2 * HEAD_SLOT), lambda b, p, i: (b, p)),
                    pl.BlockSpec((C, 2 * HEAD_SLOT), lambda b, p, i: (B * S // C + b, p))]
        args = (qm, km, km, vm, vm)
        body = functools.partial(_mla_kernel, with_x=True)
        aliases = {}
    else:
        tq = C
        nq = 1
        qmap = lambda b, p, i: (B * S // C + b, p)
        in_specs = [pl.BlockSpec((tq, 2 * HEAD_SLOT), qmap),
                    pl.BlockSpec((C, 2 * HEAD_SLOT), qmap),
                    pl.BlockSpec((C, 2 * HEAD_SLOT), qmap),
                    pl.BlockSpec(memory_space=pl.ANY)]
        args = (qm, km, vm, prev)
        body = functools.partial(_ctx_rows_kernel, _mla_kernel, with_x=False)
        aliases = {3: 0}
    return pl.pallas_call(
        body,
        grid=(B, npair, nq),
        in_specs=in_specs,
        out_specs=pl.BlockSpec((tq, LANES), qmap),
        out_shape=jax.ShapeDtypeStruct((T, MLA_HEADS * MLA_V), BF16),
        input_output_aliases=aliases,
        compiler_params=_cparams(("arbitrary",) * 3),
        name="mla_x" if with_x else "mla_c",
    )(*args)


def _swa_kernel(*refs, with_x, nblk, qb):
    if with_x:
        sink_ref, q_ref, kp_ref, ko_ref, kn_ref, kc_ref, vp_ref, vo_ref, vn_ref, vc_ref, o_ref = refs
    else:
        sink_ref, q_ref, kc_ref, vc_ref, o_ref = refs
    n0 = pl.program_id(1) * qb
    lane = lax.broadcasted_iota(jnp.int32, (1, LANES), 1)
    lo = lane < SWA_HEAD_DIM
    row2 = lax.broadcasted_iota(jnp.int32, (2 * BLOCK, 1), 0)
    kc, vc = kc_ref[...], vc_ref[...]
    if with_x:
        kloc = jnp.concatenate([kp_ref[...], ko_ref[...], kn_ref[...]], axis=0)
        vloc = jnp.concatenate([vp_ref[...], vo_ref[...], vn_ref[...]], axis=0)
        nk = 3 * BLOCK + kc.shape[0]
        qi = lax.broadcasted_iota(jnp.int32, (2 * BLOCK, nk), 0) % BLOCK
        kj = lax.broadcasted_iota(jnp.int32, (2 * BLOCK, nk), 1)
        out_prev = jnp.logical_and(kj < BLOCK, kj - qi < BLOCK - WINDOW)
        out_next = jnp.logical_and(jnp.logical_and(kj >= 2 * BLOCK, kj < 3 * BLOCK), kj - qi > BLOCK + WINDOW)
        in_prev = kj < BLOCK
        in_next = jnp.logical_and(kj >= 2 * BLOCK, kj < 3 * BLOCK)
    for i in range(qb):
        rs = slice(i * BLOCK, (i + 1) * BLOCK)
        if with_x:
            kall = jnp.concatenate([kloc[i * BLOCK:(i + 3) * BLOCK], kc], axis=0)
            vall = jnp.concatenate([vloc[i * BLOCK:(i + 3) * BLOCK], vc], axis=0)
            n = n0 + i
            bad = jnp.logical_or(
                jnp.logical_or(out_prev, jnp.logical_and(in_prev, n == 0)),
                jnp.logical_or(out_next, jnp.logical_and(in_next, n == nblk - 1)))
        else:
            kall, vall = kc, vc
        for s in range(SWA_HEADS // 2):
            q = q_ref[rs, s * LANES:(s + 1) * LANES]
            zero = jnp.zeros_like(q)
            q2 = jnp.concatenate([jnp.where(lo, q, zero), jnp.where(lo, zero, q)], axis=0)
            sink = jnp.where(row2 < BLOCK, sink_ref[0, s], sink_ref[0, SWA_HEADS // 2 + s])
            sc = _dot_nt(q2, kall)
            if with_x:
                sc = jnp.where(bad, -jnp.inf, sc)
            m = jnp.maximum(jnp.max(sc, axis=-1, keepdims=True), sink)
            p = jnp.exp(sc - m)
            l = jnp.sum(p, axis=-1, keepdims=True) + jnp.exp(sink - m)
            o = _dot(p.astype(BF16), vall) / l
            o_ref[rs, s * LANES:(s + 1) * LANES] = jnp.where(lo, o[:BLOCK], o[BLOCK:]).astype(BF16)


def _swa_attend(sink, qs, ks, vs, B, S, C, prev=None):
    T = qs.shape[0]
    cb = B * S // C
    with_x = prev is None
    if with_x:
        nblk = S // BLOCK
        qb = SWA_QB
        nstep = nblk // qb
        qmap = lambda b, j: (b * nstep + j, 0)
        pmap = lambda b, j: (b * nblk + jnp.maximum(j * qb - 1, 0), 0)
        nmap = lambda b, j: (b * nblk + jnp.minimum(j * qb + qb, nblk - 1), 0)
        cmap = lambda b, j: (cb + b, 0)
        kv1 = lambda mp: pl.BlockSpec((BLOCK, LANES), mp)
        kvq = pl.BlockSpec((qb * BLOCK, LANES), qmap)
        cspec = pl.BlockSpec((C, LANES), cmap)
        in_specs = [pl.BlockSpec(memory_space=pltpu.SMEM),
                    pl.BlockSpec((qb * BLOCK, SWA_HEADS * SWA_HEAD_DIM), qmap),
                    kv1(pmap), kvq, kv1(nmap), cspec, kv1(pmap), kvq, kv1(nmap), cspec]
        args = (sink, qs, ks, ks, ks, ks, vs, vs, vs, vs)
        body = functools.partial(_swa_kernel, with_x=True, nblk=nblk, qb=qb)
        aliases = {}
    else:
        nblk = C // BLOCK
        qb = nblk
        nstep = 1
        qmap = lambda b, j: (cb + b, 0)
        cspec = pl.BlockSpec((C, LANES), qmap)
        in_specs = [pl.BlockSpec(memory_space=pltpu.SMEM),
                    pl.BlockSpec((C, SWA_HEADS * SWA_HEAD_DIM), qmap), cspec, cspec,
                    pl.BlockSpec(memory_space=pl.ANY)]
        args = (sink, qs, ks, vs, prev)
        body = functools.partial(_ctx_rows_kernel, _swa_kernel, with_x=False, nblk=nblk, qb=qb)
        aliases = {4: 0}
    return pl.pallas_call(
        body,
        grid=(B, nstep),
        in_specs=in_specs,
        out_specs=pl.BlockSpec((qb * BLOCK, SWA_HEADS * SWA_HEAD_DIM), qmap),
        out_shape=jax.ShapeDtypeStruct((T, SWA_HEADS * SWA_HEAD_DIM), BF16),
        input_output_aliases=aliases,
        compiler_params=_cparams(("arbitrary",) * 2),
        name="swa_x" if with_x else "swa_c",
    )(*args)


def _fourier_kernel(c_ref, s_ref, pp_ref, w_ref, o_ref):
    f = _dot(c_ref[...], pp_ref[:, 0:D_FNET]) - _dot(s_ref[...], pp_ref[:, D_FNET:2 * D_FNET])
    o_ref[...] = _dot(f.astype(BF16), w_ref[...]).astype(BF16)


def _fourier(cmat, smat, pp, wblk, B, N, row0, prev=None):
    T = pp.shape[0]
    tq = min(512, N)
    nr = N // tq
    b0 = row0 // N
    o0 = row0 // tq
    in_specs = [pl.BlockSpec((tq, N), lambda r, b: (r, 0)),
                pl.BlockSpec((tq, N), lambda r, b: (r, 0)),
                pl.BlockSpec((N, 2 * D_FNET), lambda r, b: (b0 + b, 0)),
                pl.BlockSpec((D_FNET, D_FNET), lambda r, b: (0, 0))]
    args = (cmat, smat, pp, wblk)
    if prev is None:
        body = _fourier_kernel
        aliases = {}
    else:
        in_specs.append(pl.BlockSpec(memory_space=pl.ANY))
        args = args + (prev,)
        body = functools.partial(_ctx_rows_kernel, _fourier_kernel)
        aliases = {4: 0}
    return pl.pallas_call(
        body,
        grid=(nr, B),
        in_specs=in_specs,
        out_specs=pl.BlockSpec((tq, D_FNET), lambda r, b: (o0 + b * nr + r, 0)),
        out_shape=jax.ShapeDtypeStruct((T, D_FNET), BF16),
        input_output_aliases=aliases,
        compiler_params=_cparams(("arbitrary",) * 2),
        name="fourier_%d" % N,
    )(*args)


def _route_rows(sel, aff):
    G, K = N_EXPERT_GROUPS, EXPERTS_PER_GROUP
    gscore = []
    for g in range(G):
        a = sel[g * K:(g + 1) * K]
        best = None
        for i in range(K):
            for j in range(i + 1, K):
                v = a[i] + a[j]
                best = v if best is None else jnp.maximum(best, v)
        gscore.append(best)
    gb = jnp.zeros_like(gscore[0])
    gbest = gscore[0]
    for g in range(1, G):
        upd = gscore[g] > gbest
        gb = jnp.where(upd, float(g), gb)
        gbest = jnp.where(upd, gscore[g], gbest)
    cs, ca = [], []
    for i in range(K):
        c, a = sel[i], aff[i]
        for g in range(1, G):
            pick = gb == float(g)
            c = jnp.where(pick, sel[g * K + i], c)
            a = jnp.where(pick, aff[g * K + i], a)
        cs.append(c)
        ca.append(a)

    def first_max(vals):
        bi = jnp.zeros_like(vals[0])
        bv = vals[0]
        for i in range(1, K):
            upd = vals[i] > bv
            bi = jnp.where(upd, float(i), bi)
            bv = jnp.where(upd, vals[i], bv)
        return bi

    i1 = first_max(cs)
    cs2 = [jnp.where(i1 == float(i), -jnp.inf, cs[i]) for i in range(K)]
    i2 = first_max(cs2)
    a1 = sum(jnp.where(i1 == float(i), ca[i], 0.0) for i in range(K))
    a2 = sum(jnp.where(i2 == float(i), ca[i], 0.0) for i in range(K))
    den = a1 + a2
    return gb * K + i1, gb * K + i2, a1 / den, a2 / den


def _post_kernel(x_ref, mod_ref, fo_ref, ml_ref, sw_ref, wof_ref, wom_ref, wos_ref, n2g_ref,
                 rwh_ref, rwl_ref, rb_ref, tri_ref, ones_ref, xn_ref, h2_ref, rc_ref, rr_ref, cnt_ref):
    m = mod_ref[...]
    parts = []
    for r0 in range(0, TM, POST_CHAIN_ROWS):
        rs = slice(r0, r0 + POST_CHAIN_ROWS)
        mix = (_dot(fo_ref[rs, :], wof_ref[...]) + _dot(ml_ref[rs, :], wom_ref[...])
               + _dot(sw_ref[rs, :], wos_ref[...]))
        xn = x_ref[rs, :] + m[2:3] * mix
        xn_ref[rs, :] = xn
        h2 = _rms(xn, D_MODEL) * n2g_ref[...] * (1.0 + m[4:5]) + m[3:4]
        h2_ref[rs, :] = h2.astype(BF16)
        hh = h2.astype(BF16)
        hl = (h2 - hh.astype(F32)).astype(BF16)
        both = _dot(hh, rwl_ref[...])
        parts.append(both[:, :LANES] + (_dot(hl, rwh_ref[...]) + both[:, LANES:]))
    logits = jnp.concatenate(parts, axis=0)
    lt = logits.T[0:N_EXPERTS, :]
    aff_t = jax.nn.sigmoid(lt)
    sel_t = aff_t + rb_ref[...]
    sel = [sel_t[e:e + 1, :] for e in range(N_EXPERTS)]
    aff = [aff_t[e:e + 1, :] for e in range(N_EXPERTS)]
    e1, e2, w1, w2 = _route_rows(sel, aff)

    eio = lax.broadcasted_iota(jnp.int32, (N_EXPERTS, TM), 0).astype(F32)
    oh = jnp.concatenate([jnp.where(eio == e1, 1.0, 0.0), jnp.where(eio == e2, 1.0, 0.0)], axis=1)
    ohb = oh.astype(BF16)
    rank = _dot(ohb, tri_ref[...])
    cnt = _dot(ohb, ones_ref[...]).astype(jnp.int32)
    cnt8 = jnp.left_shift(jnp.right_shift(cnt + (CHUNK - 1), CHUNK_LOG2), CHUNK_LOG2)
    cnt_ref[...] = cnt8
    cnt8f = cnt8.astype(F32)
    off = jnp.zeros((1, 1), F32)
    slot = jnp.zeros((1, 2 * TM), F32)
    for e in range(N_EXPERTS):
        slot = slot + oh[e:e + 1, :] * (off + rank[e:e + 1, :])
        off = off + cnt8f[e:e + 1, 0:1]
    s0, s1 = slot[:, :TM], slot[:, TM:]

    sub = lax.broadcasted_iota(jnp.int32, (8, TM), 0)
    rr_ref[...] = jnp.where(sub == 0, s0, jnp.where(sub == 1, s1, 0.0))
    blk = jnp.where(sub == 0, e1, jnp.where(sub == 1, e2, jnp.where(sub == 2, w1, jnp.where(
        sub == 3, w2, jnp.where(sub == 4, s0, jnp.where(sub == 5, s1, 0.0))))))
    rows = jnp.concatenate([blk, jnp.zeros((LANES - 8, TM), F32)], axis=0)
    rc_ref[...] = rows.T


def _post(xc, mod, fo, ml, sw, wl, n2g, rw_hi, rw_lo, rb_col, nt, nx_tiles, tiles_per_batch):
    rows = nt * TM

    def bidx(i):
        return jnp.where(i < nx_tiles, i // tiles_per_batch, mod.shape[0] - 1)

    def full(a):
        return pl.BlockSpec(a.shape, lambda i: (0,) * a.ndim)

    row = lambda w: pl.BlockSpec((TM, w), lambda i: (i, 0))
    pair = np.arange(2 * TM)
    tri = jnp.asarray(pair[:, None] < pair[None, :], BF16)
    ones = jnp.ones((2 * TM, LANES), BF16)
    consts = [wl["wo_f"], wl["wo_m"], wl["wo_s"], n2g, rw_hi, rw_lo, rb_col, tri, ones]
    return pl.pallas_call(
        _post_kernel,
        grid=(nt,),
        in_specs=[row(D_MODEL), pl.BlockSpec((None, 8, D_MODEL), lambda i: (bidx(i), 0, 0)),
                  row(D_FNET), row(MLA_HEADS * MLA_V), row(SWA_HEADS * SWA_HEAD_DIM)]
                 + [full(a) for a in consts],
        out_specs=[row(D_MODEL), row(D_MODEL), row(LANES),
                   pl.BlockSpec((None, 8, TM), lambda i: (i, 0, 0)),
                   pl.BlockSpec((None, N_EXPERTS, LANES), lambda i: (i, 0, 0))],
        out_shape=[jax.ShapeDtypeStruct((rows, D_MODEL), F32),
                   jax.ShapeDtypeStruct((rows, D_MODEL), BF16),
                   jax.ShapeDtypeStruct((rows, LANES), F32),
                   jax.ShapeDtypeStruct((nt, 8, TM), F32),
                   jax.ShapeDtypeStruct((nt, N_EXPERTS, LANES), jnp.int32)],
        compiler_params=_cparams(("arbitrary",)),
        name="post",
    )(xc, mod, fo, ml, sw, *consts)


def _moe_tables(cnt8, te):
    nt = cnt8.shape[0]
    tile_prefix = jnp.cumsum(cnt8, axis=0) - cnt8
    tot = jnp.sum(cnt8, axis=0)
    tot_e = ((tot + te - 1) // te) * te
    goff = jnp.cumsum(tot_e) - tot_e
    dbase = goff[None, :] + tile_prefix
    nch = cnt8 // CHUNK
    cum = jnp.cumsum(nch, axis=1)
    k = jnp.arange(MAX_CHUNKS, dtype=jnp.int32)[None, :, None]
    owns = jnp.logical_and(k >= (cum - nch)[:, None, :], k < cum[:, None, :])
    dst = jnp.sum(jnp.where(owns, dbase[:, None, :] + CHUNK * (k - (cum - nch)[:, None, :]), 0), axis=-1)
    nchunks = cum[:, -1]
    padch = (tot_e - tot) // CHUNK
    cump = jnp.cumsum(padch)
    kp = jnp.arange(N_EXPERTS * (te // CHUNK), dtype=jnp.int32)[:, None]
    pown = jnp.logical_and(kp >= (cump - padch)[None, :], kp < cump[None, :])
    pdst = jnp.sum(jnp.where(pown, (goff + tot)[None, :] + CHUNK * (kp - (cump - padch)[None, :]), 0), axis=-1)
    npad = cump[-1]
    ntile_cum = jnp.cumsum(tot_e // te)
    nact = ntile_cum[-1]
    return dict(dst=dst.reshape(-1).astype(jnp.int32), nchunks=nchunks.astype(jnp.int32),
                pdst=pdst.astype(jnp.int32), npad=npad.reshape(1).astype(jnp.int32),
                ntile_cum=ntile_cum.astype(jnp.int32), nact=nact.reshape(1).astype(jnp.int32))


def _dispatch_kernel(dst_ref, nch_ref, pdst_ref, npad_ref, h_ref, rr_ref, xs_ref, sbuf, zbuf, sem, zsem):
    i = pl.program_id(0)
    nt = pl.num_programs(0)
    slot = i % 2

    def chunk_copy(sl, k, d):
        return pltpu.make_async_copy(
            sbuf.at[sl, pl.ds(pl.multiple_of(k * CHUNK, CHUNK), CHUNK), :],
            xs_ref.at[pl.ds(pl.multiple_of(d, CHUNK), CHUNK), :], sem.at[sl])

    def wait_tile(t, sl):
        n = nch_ref[t] * CHUNK
        pltpu.make_async_copy(sbuf.at[sl, pl.ds(0, n), :], xs_ref.at[pl.ds(0, n), :], sem.at[sl]).wait()

    def pad_copy(d):
        return pltpu.make_async_copy(zbuf, xs_ref.at[pl.ds(pl.multiple_of(d, CHUNK), CHUNK), :], zsem)

    @pl.when(i == 0)
    def _():
        zbuf[...] = jnp.zeros_like(zbuf)

        def start(k, c):
            pad_copy(pdst_ref[k]).start()
            return c
        lax.fori_loop(0, npad_ref[0], start, 0)

        def wait(k, c):
            pad_copy(0).wait()
            return c
        lax.fori_loop(0, npad_ref[0], wait, 0)

    @pl.when(i >= 2)
    def _():
        wait_tile(i - 2, slot)

    rr = rr_ref[...]
    sio = lax.broadcasted_iota(jnp.int32, (NSLOT, TM), 0).astype(F32)
    psel = jnp.where(jnp.logical_or(sio == rr[0:1, :], sio == rr[1:2, :]), 1.0, 0.0).astype(BF16)
    sbuf[slot] = _dot(psel, h_ref[...]).astype(BF16)

    for k in range(MIN_CHUNKS):
        chunk_copy(slot, k, dst_ref[i * MAX_CHUNKS + k]).start(priority=k % 2)

    def issue(k, c):
        chunk_copy(slot, k, dst_ref[i * MAX_CHUNKS + k]).start()
        return c
    lax.fori_loop(MIN_CHUNKS, nch_ref[i], issue, 0)

    @pl.when(i == nt - 1)
    def _():
        wait_tile(i, slot)

        @pl.when(i >= 1)
        def _():
            wait_tile(i - 1, 1 - slot)


def _dispatch(h2, rr, tb, rows_sorted):
    nt = rr.shape[0]
    return pl.pallas_call(
        _dispatch_kernel,
        grid_spec=pltpu.PrefetchScalarGridSpec(
            num_scalar_prefetch=4,
            grid=(nt,),
            in_specs=[pl.BlockSpec((TM, D_MODEL), lambda i, *_: (i, 0)),
                      pl.BlockSpec((None, 8, TM), lambda i, *_: (i, 0, 0))],
            out_specs=pl.BlockSpec(memory_space=pl.ANY),
            scratch_shapes=[pltpu.VMEM((2, NSLOT, D_MODEL), BF16), pltpu.VMEM((CHUNK, D_MODEL), BF16),
                            pltpu.SemaphoreType.DMA((2,)), pltpu.SemaphoreType.DMA(())]),
        out_shape=jax.ShapeDtypeStruct((rows_sorted, D_MODEL), BF16),
        compiler_params=_cparams(("arbitrary",)),
        name="moe_dispatch",
    )(tb["dst"], tb["nchunks"], tb["pdst"], tb["npad"], h2, rr)


def _expert_kernel(te_ref, na_ref, x_ref, wg_ref, wu_ref, wd_ref, o_ref, wgb, wub, wdb):
    j = pl.program_id(0)
    active = j < na_ref[0]
    fresh = jnp.logical_or(j == 0, te_ref[j] != te_ref[jnp.maximum(j - 1, 0)])

    @pl.when(jnp.logical_and(active, fresh))
    def _():
        wgb[...] = wg_ref[...].astype(BF16)
        wub[...] = wu_ref[...].astype(BF16)
        wdb[...] = wd_ref[...].astype(BF16)

    @pl.when(active)
    def _():
        x = x_ref[...]
        a = _dot(x, wgb[...])
        a = a * jax.nn.sigmoid(a) * _dot(x, wub[...])
        o_ref[...] = _dot(a.astype(BF16), wdb[...]).astype(BF16)


def _experts(xs, wg, wu, wd, layer, tb, te):
    nte = xs.shape[0] // te
    jj = jnp.minimum(jnp.arange(nte, dtype=jnp.int32), tb["nact"][0] - 1)
    tile_e = jnp.sum(jj[:, None] >= tb["ntile_cum"][None, :], axis=-1).astype(jnp.int32)

    def tmap(j, te_ref, na):
        return (jnp.minimum(j, na[0] - 1), 0)

    def wmap(j, te_ref, na):
        return (layer, te_ref[j], 0, 0)

    return pl.pallas_call(
        _expert_kernel,
        grid_spec=pltpu.PrefetchScalarGridSpec(
            num_scalar_prefetch=2,
            grid=(nte,),
            in_specs=[pl.BlockSpec((te, D_MODEL), tmap),
                      pl.BlockSpec((None, None, D_MODEL, D_EXPERT), wmap),
                      pl.BlockSpec((None, None, D_MODEL, D_EXPERT), wmap),
                      pl.BlockSpec((None, None, D_EXPERT, D_MODEL), wmap)],
            out_specs=pl.BlockSpec((te, D_MODEL), tmap),
            scratch_shapes=[pltpu.VMEM((D_MODEL, D_EXPERT), BF16), pltpu.VMEM((D_MODEL, D_EXPERT), BF16),
                            pltpu.VMEM((D_EXPERT, D_MODEL), BF16)]),
        out_shape=jax.ShapeDtypeStruct(xs.shape, BF16),
        compiler_params=_cparams(("arbitrary",)),
        name="moe_experts",
    )(tile_e, tb["nact"], xs, wg, wu, wd)


def _combine_kernel(dst_ref, nch_ref, xn_ref, mod_ref, rc_ref, ys_ref, o_ref, gbuf, sem):
    i = pl.program_id(0)
    nt = pl.num_programs(0)
    slot = i % 2

    def chunk_copy(sl, k, d):
        return pltpu.make_async_copy(
            ys_ref.at[pl.ds(pl.multiple_of(d, CHUNK), CHUNK), :],
            gbuf.at[sl, pl.ds(pl.multiple_of(k * CHUNK, CHUNK), CHUNK), :], sem.at[sl])

    def issue_tile(t, sl):
        for k in range(MIN_CHUNKS):
            chunk_copy(sl, k, dst_ref[t * MAX_CHUNKS + k]).start(priority=k % 2)

        def body(k, c):
            chunk_copy(sl, k, dst_ref[t * MAX_CHUNKS + k]).start()
            return c
        lax.fori_loop(MIN_CHUNKS, nch_ref[t], body, 0)

    @pl.when(i == 0)
    def _():
        gbuf[...] = jnp.zeros_like(gbuf)
        issue_tile(0, 0)

    @pl.when(i + 1 < nt)
    def _():
        issue_tile(i + 1, 1 - slot)

    n = nch_ref[i] * CHUNK
    pltpu.make_async_copy(ys_ref.at[pl.ds(0, n), :], gbuf.at[slot, pl.ds(0, n), :], sem.at[slot]).wait()

    g = gbuf[slot]
    rc = rc_ref[...]
    lio = lax.broadcasted_iota(jnp.int32, (TM, NSLOT), 1).astype(F32)
    p0 = jnp.where(lio == rc[:, 4:5], 1.0, 0.0).astype(BF16)
    p1 = jnp.where(lio == rc[:, 5:6], 1.0, 0.0).astype(BF16)
    y = rc[:, 2:3] * _dot(p0, g) + rc[:, 3:4] * _dot(p1, g)
    o_ref[...] = xn_ref[...] + mod_ref[5:6, :] * y


def _combine(xn, mod, rc, ys, tb, nx_tiles, tiles_per_batch):
    rows = xn.shape[0]
    nt = rows // TM

    def bidx(i, *_):
        return (jnp.where(i < nx_tiles, i // tiles_per_batch, mod.shape[0] - 1), 0, 0)

    return pl.pallas_call(
        _combine_kernel,
        grid_spec=pltpu.PrefetchScalarGridSpec(
            num_scalar_prefetch=2,
            grid=(nt,),
            in_specs=[pl.BlockSpec((TM, D_MODEL), lambda i, *_: (i, 0)),
                      pl.BlockSpec((None, 8, D_MODEL), bidx),
                      pl.BlockSpec((TM, LANES), lambda i, *_: (i, 0)),
                      pl.BlockSpec(memory_space=pl.ANY)],
            out_specs=pl.BlockSpec((TM, D_MODEL), lambda i, *_: (i, 0)),
            scratch_shapes=[pltpu.VMEM((2, NSLOT, D_MODEL), BF16), pltpu.SemaphoreType.DMA((2,))]),
        out_shape=jax.ShapeDtypeStruct((rows, D_MODEL), F32),
        compiler_params=_cparams(("arbitrary",)),
        name="moe_combine",
    )(tb["dst"], tb["nchunks"], xn, mod, rc, ys)


def _moe(h2, rc, rr, cnt, wg, wu, wd, layer, xn, mod, nx_tiles, tiles_per_batch):
    nt = rr.shape[0]
    te = TE
    max_rows = 2 * nt * TM + (CHUNK - 1) * N_EXPERTS * nt + N_EXPERTS * (te - CHUNK)
    rows_sorted = ((max_rows + te - 1) // te) * te
    tb = _moe_tables(cnt[:, :, 0], te)
    xs = _dispatch(h2, rr, tb, rows_sorted)
    ys = _experts(xs, wg, wu, wd, layer, tb, te)
    return _combine(xn, mod, rc, ys, tb, nx_tiles, tiles_per_batch)


def _rope_tables(S, C):
    t = jnp.arange(S)
    rows, cols = (t // GRID_W).astype(F32), (t % GRID_W).astype(F32)

    def axis_tabs(d_rot, lane0, width):
        d_axis = d_rot // 2
        inv = ROPE_THETA ** (-jnp.arange(0, d_axis, 2, dtype=F32) / d_axis)
        ar, ac = rows[:, None] * inv, cols[:, None] * inv
        ang = jnp.concatenate([ar, ar, ac, ac], axis=-1)
        q = d_rot // 4
        first = np.concatenate([np.ones(q), np.zeros(q), np.ones(q), np.zeros(q)]).astype(np.float32)
        pad = ((0, C), (lane0, width - lane0 - d_rot))
        cos = jnp.pad(jnp.cos(ang) - 1.0, pad) + 1.0
        sa = jnp.pad(-jnp.sin(ang) * first, pad)
        sb = jnp.pad(jnp.sin(ang) * (1.0 - first), pad)
        return cos, sa, sb

    cm, sam, sbm = axis_tabs(MLA_ROPE, MLA_NOPE, LANES)
    cs, sas, sbs = axis_tabs(SWA_HEAD_DIM, 0, SWA_HEAD_DIM)
    tile2 = lambda a: jnp.concatenate([a, a], axis=1)
    return dict(cm=cm, sam=sam, sbm=sbm, cs=tile2(cs), sas=tile2(sas), sbs=tile2(sbs))


_TWO_PI_HI = float(np.float32(2.0 * np.pi))
_TWO_PI_LO = float(np.float32(2.0 * np.pi - np.float64(np.float32(2.0 * np.pi))))


def _dft_mats(N):
    k = jnp.arange(N, dtype=jnp.int32)

    def cos_sin(rows, period):
        frac = ((rows[:, None] * k[None, :]) % period).astype(F32) / period
        ang = _TWO_PI_HI * frac + _TWO_PI_LO * frac
        return jnp.cos(ang), jnp.sin(ang)

    if N <= 4 * FNET_CH:
        return cos_sin(k, N)
    A = N // FNET_CH
    c1, s1 = cos_sin(jnp.arange(A, dtype=jnp.int32), A)
    c2, s2 = cos_sin(jnp.arange(FNET_CH, dtype=jnp.int32), N)
    c = c1[:, None, :] * c2[None, :, :] - s1[:, None, :] * s2[None, :, :]
    s = s1[:, None, :] * c2[None, :, :] + c1[:, None, :] * s2[None, :, :]
    return c.reshape(N, N), s.reshape(N, N)


def _dft64_blocks():
    c, s = _dft_mats(FNET_CH)
    eye = jnp.eye(FNET_GROUPS, dtype=F32)
    return jnp.concatenate([jnp.kron(eye, c), jnp.kron(eye, s)], axis=1)


def _layer_weights(l, w_in, fnet_w, mla_cq_g, mla_ckv_g, mla_w_uq, mla_w_uk, mla_w_uv, mla_q_g, mla_k_g,
                   swa_q_g, swa_k_g, swa_sink, w_out):
    D = D_MODEL
    wi = w_in[l]
    o_kr = D_FNET + MLA_Q_RANK + MLA_KV_RANK
    o_qs = o_kr + MLA_ROPE
    o_ks = o_qs + SWA_HEADS * SWA_HEAD_DIM
    o_vs = o_ks + SWA_KV_HEADS * SWA_HEAD_DIM
    order = np.array(SWA_HEAD_ORDER)
    w_qs = wi[:, o_qs:o_ks].reshape(D, SWA_HEADS, SWA_HEAD_DIM)[:, order].reshape(D, -1)
    z = lambda n: jnp.zeros((D, n), F32)
    win = jnp.concatenate([wi[:, :o_kr], w_qs, wi[:, o_ks:o_vs], wi[:, o_vs:],
                           z(MLA_NOPE), wi[:, o_kr:o_qs], z(LANES - MLA_QK)], axis=1)
    pad_slot = lambda w, d: jnp.pad(w.reshape(w.shape[0], MLA_HEADS, d),
                                    ((0, 0), (0, 0), (0, HEAD_SLOT - d))).reshape(w.shape[0], -1)
    wo = w_out[l]
    o_m = D_FNET
    o_s = D_FNET + MLA_HEADS * MLA_V
    wo_s = wo[o_s:].reshape(SWA_HEADS, SWA_HEAD_DIM, D)[order].reshape(-1, D)
    fw = fnet_w[l]
    wblk = jnp.zeros((D_FNET, D_FNET), F32)
    for g in range(FNET_GROUPS):
        wblk = wblk.at[g * FNET_CH:(g + 1) * FNET_CH, g * FNET_CH:(g + 1) * FNET_CH].set(fw[g])
    pad_g = lambda g: jnp.pad(g, (0, HEAD_SLOT - MLA_QK)).reshape(1, HEAD_SLOT)
    return dict(
        w_in=win.astype(BF16),
        cq_g=mla_cq_g[l].reshape(1, -1), ckv_g=mla_ckv_g[l].reshape(1, -1),
        w_uq=pad_slot(mla_w_uq[l], MLA_QK).astype(BF16),
        w_uk=pad_slot(mla_w_uk[l], MLA_NOPE).astype(BF16),
        w_uv=pad_slot(mla_w_uv[l], MLA_V).astype(BF16),
        mq_g=pad_g(mla_q_g[l]), mk_g=pad_g(mla_k_g[l]),
        sq_g=jnp.tile(swa_q_g[l], 2).reshape(1, LANES), sk_g=jnp.tile(swa_k_g[l], 2).reshape(1, LANES),
        sink=swa_sink[l].reshape(1, SWA_HEADS),
        wo_f=wo[:o_m].astype(BF16), wo_m=wo[o_m:o_s].astype(BF16), wo_s=wo_s.astype(BF16),
        fnet=wblk,
    )


def kernel(x, c, ctx, c_ctx, ada_w, ada_b, norm1_g, norm2_g, w_in, fnet_w, mla_cq_g, mla_ckv_g, mla_w_uq,
           mla_w_uk, mla_w_uv, mla_q_g, mla_k_g, swa_q_g, swa_k_g, swa_sink, w_out, router_w, router_b,
           exp_w_gate, exp_w_up, exp_w_down):
    B, S, D = x.shape
    C = ctx.shape[1]
    L = ada_w.shape[0]
    assert D == D_MODEL and S % 512 == 0 and C == TM and S % TM == 0
    nx_tiles = B * S // TM
    nt_all = nx_tiles + B * C // TM
    tiles_per_batch = S // TM

    prep_tm = PREP_TM if (B * C) % PREP_TM == 0 and S % PREP_TM == 0 else TM
    tabs = _rope_tables(S, prep_tm)
    tabs["dft64"] = _dft64_blocks().astype(BF16)
    dft = {n: tuple(m.astype(BF16) for m in _dft_mats(n)) for n in (S, C)}
    fscale = {n: lax.rsqrt(jnp.full((), n * FNET_CH, F32)) for n in (S, C)}

    nmod = 16
    cvec = jnp.concatenate([c, c_ctx[None, :], jnp.zeros((nmod - B - 1, D), F32)], axis=0)
    mod_all = _adaln(cvec, ada_w, ada_b)
    mod_all = mod_all[:, :B + 1].reshape(L, B + 1, 6, D)
    mod_all = jnp.pad(mod_all, ((0, 0), (0, 0), (0, 2), (0, 0)))

    rw = jnp.pad(router_w, ((0, 0), (0, LANES - N_EXPERTS)))
    rw_hi = rw.astype(BF16)
    rw_lo = jnp.concatenate([rw_hi, (rw - rw_hi.astype(F32)).astype(BF16)], axis=1)
    rb_col = router_b.reshape(N_EXPERTS, 1)

    xc = jnp.concatenate([x.reshape(B * S, D), ctx.reshape(B * C, D)], axis=0)
    for l in range(L):
        last = l == L - 1
        wl = _layer_weights(l, w_in, fnet_w, mla_cq_g, mla_ckv_g, mla_w_uq, mla_w_uk, mla_w_uv, mla_q_g,
                            mla_k_g, swa_q_g, swa_k_g, swa_sink, w_out)
        mod = mod_all[l]
        pr = _prep(xc, mod, norm1_g[l].reshape(1, D), wl, tabs, B * S, S, prep_tm)
        fo = _fourier(dft[S][0], dft[S][1], pr["pp"], (wl["fnet"] * fscale[S]).astype(BF16), B, S, 0)
        ml = _mla_attend(pr["qm"], pr["km"], pr["vm"], B, S, C)
        sw = _swa_attend(wl["sink"], pr["qs"], pr["ks"], pr["vs"], B, S, C)
        if not last:
            fo = _fourier(dft[C][0], dft[C][1], pr["pp"], (wl["fnet"] * fscale[C]).astype(BF16), B, C, B * S,
                          prev=fo)
            ml = _mla_attend(pr["qm"], pr["km"], pr["vm"], B, S, C, prev=ml)
            sw = _swa_attend(wl["sink"], pr["qs"], pr["ks"], pr["vs"], B, S, C, prev=sw)
        nt = nx_tiles if last else nt_all
        xn, h2, rc, rr, cnt = _post(xc, mod, fo, ml, sw, wl, norm2_g[l].reshape(1, D), rw_hi, rw_lo, rb_col,
                                    nt, nx_tiles, tiles_per_batch)
        xc = _moe(h2, rc, rr, cnt, exp_w_gate, exp_w_up, exp_w_down, l, xn, mod, nx_tiles, tiles_per_batch)
    return xc[:B * S].reshape(B, S, D)
```

```python
import functools

import numpy as np
import jax
import jax.numpy as jnp
from jax import lax
from jax.experimental import pallas as pl
from jax.experimental.pallas import tpu as pltpu

F32 = jnp.float32
BF16 = jnp.bfloat16

D_MODEL = 1024
GRID_W = 64
FNET_GROUPS = 4
FNET_CH = 64
D_FNET = FNET_GROUPS * FNET_CH
MLA_HEADS = 6
MLA_Q_RANK = 256
MLA_KV_RANK = 128
MLA_NOPE = 64
MLA_ROPE = 32
MLA_QK = MLA_NOPE + MLA_ROPE
MLA_V = 64
SWA_HEADS = 6
SWA_KV_HEADS = 2
SWA_HEAD_DIM = 64
WINDOW = 128
BLOCK = 128
N_EXPERTS = 16
N_EXPERT_GROUPS = 4
EXPERTS_PER_GROUP = 4
D_EXPERT = 512
ROPE_THETA = 10000.0
EPS = 1e-6

LANES = 128
TM = 256
HEAD_SLOT = LANES
CHUNK = 16
CHUNK_LOG2 = 4
NSLOT = 768
MAX_CHUNKS = NSLOT // CHUNK
TE = 512
PREP_CHAIN_ROWS = 128
POST_CHAIN_ROWS = TM
PREP_TM = 256
MLA_CHAINS = 8
SWA_QB = 8
MIN_CHUNKS = 2 * TM // CHUNK
VMEM_LIMIT = 48 * 1024 * 1024

P_UF = 0
P_CQ = 256
P_CKV = 512
P_QS = 640
P_KS = 1024
P_VS = 1152
P_KR = 1280
P_TOT = 1408
SWA_HEAD_ORDER = (0, 3, 1, 4, 2, 5)


def _dot(a, b):
    return jnp.dot(a, b, preferred_element_type=F32)


def _dot_nt(a, b):
    return lax.dot_general(a, b, (((1,), (1,)), ((), ())), preferred_element_type=F32)


def _rms(x, n):
    return x * lax.rsqrt(jnp.sum(x * x, axis=-1, keepdims=True) / n + EPS)


def _rope(x, c, sa, sb, half):
    n = x.shape[-1]
    return x * c + pltpu.roll(x, n - half, 1) * sa + pltpu.roll(x, half, 1) * sb


def _cparams(sem):
    return pltpu.CompilerParams(dimension_semantics=sem, vmem_limit_bytes=VMEM_LIMIT)


def _adaln_kernel(c_ref, w_ref, b_ref, o_ref):
    c = c_ref[...]
    sc = c * jax.nn.sigmoid(c)
    w = w_ref[...]
    s_hi = sc.astype(BF16)
    s_lo = (sc - s_hi.astype(F32)).astype(BF16)
    w_hi = w.astype(BF16)
    w_lo = (w - w_hi.astype(F32)).astype(BF16)
    o_ref[...] = _dot(s_hi, w_hi) + (_dot(s_lo, w_hi) + _dot(s_hi, w_lo)) + b_ref[...]


def _adaln(cvec, ada_w, ada_b):
    L, D, N6 = ada_w.shape
    R = cvec.shape[0]
    bn = 512
    return pl.pallas_call(
        _adaln_kernel,
        grid=(L, N6 // bn),
        in_specs=[pl.BlockSpec((R, D), lambda l, j: (0, 0)),
                  pl.BlockSpec((None, D, bn), lambda l, j: (l, 0, j)),
                  pl.BlockSpec((None, 1, bn), lambda l, j: (l, 0, j))],
        out_specs=pl.BlockSpec((None, R, bn), lambda l, j: (l, 0, j)),
        out_shape=jax.ShapeDtypeStruct((L, R, N6), F32),
        compiler_params=_cparams(("arbitrary", "arbitrary")),
        name="adaln",
    )(cvec, ada_w, ada_b.reshape(L, 1, N6))


def _prep_kernel(x_ref, mod_ref, n1g_ref, win_ref, cqg_ref, ckvg_ref, wuq_ref, wuk_ref, wuv_ref,
                 mqg_ref, mkg_ref, sqg_ref, skg_ref, cm_ref, sam_ref, sbm_ref, cs_ref, sas_ref, sbs_ref,
                 dft_ref, pp_ref, qm_ref, km_ref, vm_ref, qs_ref, ks_ref, vs_ref):
    m = mod_ref[...]
    slot_lane = jnp.bitwise_and(lax.broadcasted_iota(jnp.int32, (1, MLA_HEADS * HEAD_SLOT), 1), HEAD_SLOT - 1)
    vone = jnp.where(slot_lane == MLA_V, 1.0, 0.0)

    li = lax.broadcasted_iota(jnp.int32, (LANES, LANES), 0) < SWA_HEAD_DIM
    lj = lax.broadcasted_iota(jnp.int32, (LANES, LANES), 1) < SWA_HEAD_DIM
    head_ones = jnp.where(li == lj, 1.0, 0.0).astype(BF16)

    def head_norm(slab, g):
        sq = slab * slab
        sq_hi = sq.astype(BF16)
        sq_lo = (sq - sq_hi.astype(F32)).astype(BF16)
        ss = _dot(sq_hi, head_ones) + _dot(sq_lo, head_ones)
        return slab * lax.rsqrt(ss * (1.0 / SWA_HEAD_DIM) + EPS) * g

    rows = PREP_CHAIN_ROWS
    for ch in range(x_ref.shape[0] // rows):
        rs = slice(ch * rows, (ch + 1) * rows)
        x = x_ref[rs, :]
        h = _rms(x, D_MODEL) * n1g_ref[...] * (1.0 + m[1:2]) + m[0:1]
        p = _dot(h.astype(BF16), win_ref[...])

        u = p[:, P_UF:P_UF + D_FNET].astype(BF16)
        pp_ref[rs, :] = _dot(u, dft_ref[...]).astype(BF16)

        cm, sam, sbm = cm_ref[rs, :], sam_ref[rs, :], sbm_ref[rs, :]
        cs, sas, sbs = cs_ref[rs, :], sas_ref[rs, :], sbs_ref[rs, :]

        cq = _rms(p[:, P_CQ:P_CQ + MLA_Q_RANK], MLA_Q_RANK) * cqg_ref[...]
        qboth = _dot(cq.astype(BF16), wuq_ref[...])
        nslot = MLA_HEADS * HEAD_SLOT
        mqg = mqg_ref[...]
        ssm = sam + sbm
        for hh in range(MLA_HEADS):
            sl = slice(hh * HEAD_SLOT, (hh + 1) * HEAD_SLOT)
            q_h = qboth[:, sl]
            r = lax.rsqrt(jnp.sum(q_h * q_h, axis=-1, keepdims=True) / MLA_QK + EPS)
            qr = q_h * mqg * cm + qboth[:, nslot + hh * HEAD_SLOT:nslot + (hh + 1) * HEAD_SLOT] * ssm
            qm_ref[rs, sl] = (qr * r * (MLA_QK ** -0.5)).astype(BF16)

        ckv = (_rms(p[:, P_CKV:P_CKV + MLA_KV_RANK], MLA_KV_RANK) * ckvg_ref[...]).astype(BF16)
        knope = _dot(ckv, wuk_ref[...])
        vm_ref[rs, :] = (_dot(ckv, wuv_ref[...]) + vone).astype(BF16)
        kr = p[:, P_KR:P_KR + LANES]
        mkg = mkg_ref[...]
        kr_ss = jnp.sum(kr * kr, axis=-1, keepdims=True)
        kr_rot = _rope(kr * mkg, cm, sam, sbm, MLA_ROPE // 4)
        for hh in range(MLA_HEADS):
            sl = slice(hh * HEAD_SLOT, (hh + 1) * HEAD_SLOT)
            kn_h = knope[:, sl]
            r = lax.rsqrt((jnp.sum(kn_h * kn_h, axis=-1, keepdims=True) + kr_ss) / MLA_QK + EPS)
            km_ref[rs, sl] = ((kn_h * mkg + kr_rot) * r).astype(BF16)

        sqg = sqg_ref[...]
        for s in range(SWA_HEADS // 2):
            sl = slice(P_QS + s * LANES, P_QS + (s + 1) * LANES)
            qn = head_norm(p[:, sl], sqg)
            qs_ref[rs, s * LANES:(s + 1) * LANES] = (
                _rope(qn, cs, sas, sbs, SWA_HEAD_DIM // 4) * (SWA_HEAD_DIM ** -0.5)).astype(BF16)
        kn = head_norm(p[:, P_KS:P_KS + LANES], skg_ref[...])
        ks_ref[rs, :] = _rope(kn, cs, sas, sbs, SWA_HEAD_DIM // 4).astype(BF16)
        vs_ref[rs, :] = p[:, P_VS:P_VS + LANES].astype(BF16)


def _prep(xc, mod, n1g, wl, tabs, n_latent, seq, tm):
    T = xc.shape[0]
    nt = T // tm
    nx_tiles = n_latent // tm
    tiles_per_batch = seq // tm

    def bidx(i):
        return jnp.where(i < nx_tiles, i // tiles_per_batch, mod.shape[0] - 1)

    def ridx(i):
        return jnp.where(i < nx_tiles, i % tiles_per_batch, tiles_per_batch)

    def full(a):
        return pl.BlockSpec(a.shape, lambda i: (0,) * a.ndim)

    tab_spec = pl.BlockSpec((tm, LANES), lambda i: (ridx(i), 0))
    row = lambda w: pl.BlockSpec((tm, w), lambda i: (i, 0))
    consts = [n1g, wl["w_in"], wl["cq_g"], wl["ckv_g"], wl["w_uq"], wl["w_uk"], wl["w_uv"],
              wl["mq_g"], wl["mk_g"], wl["sq_g"], wl["sk_g"]]
    outs = [("pp", 2 * D_FNET), ("qm", MLA_HEADS * HEAD_SLOT), ("km", MLA_HEADS * HEAD_SLOT),
            ("vm", MLA_HEADS * HEAD_SLOT), ("qs", SWA_HEADS * SWA_HEAD_DIM), ("ks", LANES), ("vs", LANES)]
    res = pl.pallas_call(
        _prep_kernel,
        grid=(nt,),
        in_specs=[row(D_MODEL), pl.BlockSpec((None, 8, D_MODEL), lambda i: (bidx(i), 0, 0))]
                 + [full(a) for a in consts] + [tab_spec] * 6 + [full(tabs["dft64"])],
        out_specs=[row(w) for _, w in outs],
        out_shape=[jax.ShapeDtypeStruct((T, w), BF16) for _, w in outs],
        compiler_params=_cparams(("arbitrary",)),
        name="prep",
    )(xc, mod, *consts, tabs["cm"], tabs["sam"], tabs["sbm"], tabs["cs"], tabs["sas"], tabs["sbs"],
      tabs["dft64"])
    return dict(zip([n for n, _ in outs], res))


def _mla_kernel(*refs, with_x):
    if with_x:
        q_ref, kx_ref, kc_ref, vx_ref, vc_ref, o_ref = refs
    else:
        q_ref, kc_ref, vc_ref, o_ref = refs
    lane = lax.broadcasted_iota(jnp.int32, (1, LANES), 1)
    rows = q_ref.shape[0] // MLA_CHAINS if with_x else q_ref.shape[0]
    for r0 in range(0, q_ref.shape[0], rows):
        rs = slice(r0, r0 + rows)
        outs = []
        for hh in range(2):
            sl = slice(hh * HEAD_SLOT, (hh + 1) * HEAD_SLOT)
            q = q_ref[rs, sl]
            sc = _dot_nt(q, kc_ref[:, sl])
            m = jnp.max(sc, axis=-1, keepdims=True)
            if with_x:
                sx = _dot_nt(q, kx_ref[:, sl])
                m = jnp.maximum(m, jnp.max(sx, axis=-1, keepdims=True))
                px = jnp.exp(sx - m)
            pc = jnp.exp(sc - m)
            o = _dot(pc.astype(BF16), vc_ref[:, sl])
            if with_x:
                o = o + _dot(px.astype(BF16), vx_ref[:, sl])
            outs.append(o / o[:, MLA_V:MLA_V + 1])
        o_ref[rs, :] = jnp.where(lane < MLA_V, outs[0], pltpu.roll(outs[1], MLA_V, 1)).astype(BF16)


def _ctx_rows_kernel(kernel_fn, *refs, **kw):
    kernel_fn(*refs[:-2], refs[-1], **kw)


def _mla_attend(qm, km, vm, B, S, C, prev=None):
    T = qm.shape[0]
    npair = MLA_HEADS // 2
    with_x = prev is None
    if with_x:
        tq = 2048
        nq = S // tq
        qmap = lambda b, p, i: (b * nq + i, p)
        in_specs = [pl.BlockSpec((tq, 2 * HEAD_SLOT), qmap),
                    pl.BlockSpec((S, 2 * HEAD_SLOT), lambda b, p, i: (b, p)),
                    pl.BlockSpec((C, 2 * HEAD_SLOT), lambda b, p, i: (B * S // C + b, p)),
                    pl.BlockSpec((S, 2 * HEAD_SLOT), lambda b, p, i: (b, p)),
                    pl.BlockSpec((C, 2 * HEAD_SLOT), lambda b, p, i: (B * S // C + b, p))]
        args = (qm, km, km, vm, vm)
        body = functools.partial(_mla_kernel, with_x=True)
        aliases = {}
    else:
        tq = C
        nq = 1
        qmap = lambda b, p, i: (B * S // C + b, p)
        in_specs = [pl.BlockSpec((tq, 2 * HEAD_SLOT), qmap),
                    pl.BlockSpec((C, 2 * HEAD_SLOT), qmap),
                    pl.BlockSpec((C, 2 * HEAD_SLOT), qmap),
                    pl.BlockSpec(memory_space=pl.ANY)]
        args = (qm, km, vm, prev)
        body = functools.partial(_ctx_rows_kernel, _mla_kernel, with_x=False)
        aliases = {3: 0}
    return pl.pallas_call(
        body,
        grid=(B, npair, nq),
        in_specs=in_specs,
        out_specs=pl.BlockSpec((tq, LANES), qmap),
        out_shape=jax.ShapeDtypeStruct((T, MLA_HEADS * MLA_V), BF16),
        input_output_aliases=aliases,
        compiler_params=_cparams(("arbitrary",) * 3),
        name="mla_x" if with_x else "mla_c",
    )(*args)


def _swa_kernel(*refs, with_x, nblk, qb):
    if with_x:
        sink_ref, q_ref, kp_ref, ko_ref, kn_ref, kc_ref, vp_ref, vo_ref, vn_ref, vc_ref, o_ref = refs
    else:
        sink_ref, q_ref, kc_ref, vc_ref, o_ref = refs
    n0 = pl.program_id(1) * qb
    lane = lax.broadcasted_iota(jnp.int32, (1, LANES), 1)
    lo = lane < SWA_HEAD_DIM
    row2 = lax.broadcasted_iota(jnp.int32, (2 * BLOCK, 1), 0)
    kc, vc = kc_ref[...], vc_ref[...]
    if with_x:
        kloc = jnp.concatenate([kp_ref[...], ko_ref[...], kn_ref[...]], axis=0)
        vloc = jnp.concatenate([vp_ref[...], vo_ref[...], vn_ref[...]], axis=0)
        nk = 3 * BLOCK + kc.shape[0]
        qi = lax.broadcasted_iota(jnp.int32, (2 * BLOCK, nk), 0) % BLOCK
        kj = lax.broadcasted_iota(jnp.int32, (2 * BLOCK, nk), 1)
        out_prev = jnp.logical_and(kj < BLOCK, kj - qi < BLOCK - WINDOW)
        out_next = jnp.logical_and(jnp.logical_and(kj >= 2 * BLOCK, kj < 3 * BLOCK), kj - qi > BLOCK + WINDOW)
        in_prev = kj < BLOCK
        in_next = jnp.logical_and(kj >= 2 * BLOCK, kj < 3 * BLOCK)
    for i in range(qb):
        rs = slice(i * BLOCK, (i + 1) * BLOCK)
        if with_x:
            kall = jnp.concatenate([kloc[i * BLOCK:(i + 3) * BLOCK], kc], axis=0)
            vall = jnp.concatenate([vloc[i * BLOCK:(i + 3) * BLOCK], vc], axis=0)
            n = n0 + i
            bad = jnp.logical_or(
                jnp.logical_or(out_prev, jnp.logical_and(in_prev, n == 0)),
                jnp.logical_or(out_next, jnp.logical_and(in_next, n == nblk - 1)))
        else:
            kall, vall = kc, vc
        for s in range(SWA_HEADS // 2):
            q = q_ref[rs, s * LANES:(s + 1) * LANES]
            zero = jnp.zeros_like(q)
            q2 = jnp.concatenate([jnp.where(lo, q, zero), jnp.where(lo, zero, q)], axis=0)
            sink = jnp.where(row2 < BLOCK, sink_ref[0, s], sink_ref[0, SWA_HEADS // 2 + s])
            sc = _dot_nt(q2, kall)
            if with_x:
                sc = jnp.where(bad, -jnp.inf, sc)
            m = jnp.maximum(jnp.max(sc, axis=-1, keepdims=True), sink)
            p = jnp.exp(sc - m)
            l = jnp.sum(p, axis=-1, keepdims=True) + jnp.exp(sink - m)
            o = _dot(p.astype(BF16), vall) / l
            o_ref[rs, s * LANES:(s + 1) * LANES] = jnp.where(lo, o[:BLOCK], o[BLOCK:]).astype(BF16)


def _swa_attend(sink, qs, ks, vs, B, S, C, prev=None):
    T = qs.shape[0]
    cb = B * S // C
    with_x = prev is None
    if with_x:
        nblk = S // BLOCK
        qb = SWA_QB
        nstep = nblk // qb
        qmap = lambda b, j: (b * nstep + j, 0)
        pmap = lambda b, j: (b * nblk + jnp.maximum(j * qb - 1, 0), 0)
        nmap = lambda b, j: (b * nblk + jnp.minimum(j * qb + qb, nblk - 1), 0)
        cmap = lambda b, j: (cb + b, 0)
        kv1 = lambda mp: pl.BlockSpec((BLOCK, LANES), mp)
        kvq = pl.BlockSpec((qb * BLOCK, LANES), qmap)
        cspec = pl.BlockSpec((C, LANES), cmap)
        in_specs = [pl.BlockSpec(memory_space=pltpu.SMEM),
                    pl.BlockSpec((qb * BLOCK, SWA_HEADS * SWA_HEAD_DIM), qmap),
                    kv1(pmap), kvq, kv1(nmap), cspec, kv1(pmap), kvq, kv1(nmap), cspec]
        args = (sink, qs, ks, ks, ks, ks, vs, vs, vs, vs)
        body = functools.partial(_swa_kernel, with_x=True, nblk=nblk, qb=qb)
        aliases = {}
    else:
        nblk = C // BLOCK
        qb = nblk
        nstep = 1
        qmap = lambda b, j: (cb + b, 0)
        cspec = pl.BlockSpec((C, LANES), qmap)
        in_specs = [pl.BlockSpec(memory_space=pltpu.SMEM),
                    pl.BlockSpec((C, SWA_HEADS * SWA_HEAD_DIM), qmap), cspec, cspec,
                    pl.BlockSpec(memory_space=pl.ANY)]
        args = (sink, qs, ks, vs, prev)
        body = functools.partial(_ctx_rows_kernel, _swa_kernel, with_x=False, nblk=nblk, qb=qb)
        aliases = {4: 0}
    return pl.pallas_call(
        body,
        grid=(B, nstep),
        in_specs=in_specs,
        out_specs=pl.BlockSpec((qb * BLOCK, SWA_HEADS * SWA_HEAD_DIM), qmap),
        out_shape=jax.ShapeDtypeStruct((T, SWA_HEADS * SWA_HEAD_DIM), BF16),
        input_output_aliases=aliases,
        compiler_params=_cparams(("arbitrary",) * 2),
        name="swa_x" if with_x else "swa_c",
    )(*args)


def _fourier_kernel(c_ref, s_ref, pp_ref, w_ref, o_ref):
    f = _dot(c_ref[...], pp_ref[:, 0:D_FNET]) - _dot(s_ref[...], pp_ref[:, D_FNET:2 * D_FNET])
    o_ref[...] = _dot(f.astype(BF16), w_ref[...]).astype(BF16)


def _fourier(cmat, smat, pp, wblk, B, N, row0, prev=None):
    T = pp.shape[0]
    tq = min(512, N)
    nr = N // tq
    b0 = row0 // N
    o0 = row0 // tq
    in_specs = [pl.BlockSpec((tq, N), lambda r, b: (r, 0)),
                pl.BlockSpec((tq, N), lambda r, b: (r, 0)),
                pl.BlockSpec((N, 2 * D_FNET), lambda r, b: (b0 + b, 0)),
                pl.BlockSpec((D_FNET, D_FNET), lambda r, b: (0, 0))]
    args = (cmat, smat, pp, wblk)
    if prev is None:
        body = _fourier_kernel
        aliases = {}
    else:
        in_specs.append(pl.BlockSpec(memory_space=pl.ANY))
        args = args + (prev,)
        body = functools.partial(_ctx_rows_kernel, _fourier_kernel)
        aliases = {4: 0}
    return pl.pallas_call(
        body,
        grid=(nr, B),
        in_specs=in_specs,
        out_specs=pl.BlockSpec((tq, D_FNET), lambda r, b: (o0 + b * nr + r, 0)),
        out_shape=jax.ShapeDtypeStruct((T, D_FNET), BF16),
        input_output_aliases=aliases,
        compiler_params=_cparams(("arbitrary",) * 2),
        name="fourier_%d" % N,
    )(*args)


def _route_rows(sel, aff):
    G, K = N_EXPERT_GROUPS, EXPERTS_PER_GROUP
    gscore = []
    for g in range(G):
        a = sel[g * K:(g + 1) * K]
        best = None
        for i in range(K):
            for j in range(i + 1, K):
                v = a[i] + a[j]
                best = v if best is None else jnp.maximum(best, v)
        gscore.append(best)
    gb = jnp.zeros_like(gscore[0])
    gbest = gscore[0]
    for g in range(1, G):
        upd = gscore[g] > gbest
        gb = jnp.where(upd, float(g), gb)
        gbest = jnp.where(upd, gscore[g], gbest)
    cs, ca = [], []
    for i in range(K):
        c, a = sel[i], aff[i]
        for g in range(1, G):
            pick = gb == float(g)
            c = jnp.where(pick, sel[g * K + i], c)
            a = jnp.where(pick, aff[g * K + i], a)
        cs.append(c)
        ca.append(a)

    def first_max(vals):
        bi = jnp.zeros_like(vals[0])
        bv = vals[0]
        for i in range(1, K):
            upd = vals[i] > bv
            bi = jnp.where(upd, float(i), bi)
            bv = jnp.where(upd, vals[i], bv)
        return bi

    i1 = first_max(cs)
    cs2 = [jnp.where(i1 == float(i), -jnp.inf, cs[i]) for i in range(K)]
    i2 = first_max(cs2)
    a1 = sum(jnp.where(i1 == float(i), ca[i], 0.0) for i in range(K))
    a2 = sum(jnp.where(i2 == float(i), ca[i], 0.0) for i in range(K))
    den = a1 + a2
    return gb * K + i1, gb * K + i2, a1 / den, a2 / den


def _post_kernel(x_ref, mod_ref, fo_ref, ml_ref, sw_ref, wof_ref, wom_ref, wos_ref, n2g_ref,
                 rwh_ref, rwl_ref, rb_ref, tri_ref, ones_ref, xn_ref, h2_ref, rc_ref, rr_ref, cnt_ref):
    m = mod_ref[...]
    parts = []
    for r0 in range(0, TM, POST_CHAIN_ROWS):
        rs = slice(r0, r0 + POST_CHAIN_ROWS)
        mix = (_dot(fo_ref[rs, :], wof_ref[...]) + _dot(ml_ref[rs, :], wom_ref[...])
               + _dot(sw_ref[rs, :], wos_ref[...]))
        xn = x_ref[rs, :] + m[2:3] * mix
        xn_ref[rs, :] = xn
        h2 = _rms(xn, D_MODEL) * n2g_ref[...] * (1.0 + m[4:5]) + m[3:4]
        h2_ref[rs, :] = h2.astype(BF16)
        hh = h2.astype(BF16)
        hl = (h2 - hh.astype(F32)).astype(BF16)
        both = _dot(hh, rwl_ref[...])
        parts.append(both[:, :LANES] + (_dot(hl, rwh_ref[...]) + both[:, LANES:]))
    logits = jnp.concatenate(parts, axis=0)
    lt = logits.T[0:N_EXPERTS, :]
    aff_t = jax.nn.sigmoid(lt)
    sel_t = aff_t + rb_ref[...]
    sel = [sel_t[e:e + 1, :] for e in range(N_EXPERTS)]
    aff = [aff_t[e:e + 1, :] for e in range(N_EXPERTS)]
    e1, e2, w1, w2 = _route_rows(sel, aff)

    eio = lax.broadcasted_iota(jnp.int32, (N_EXPERTS, TM), 0).astype(F32)
    oh = jnp.concatenate([jnp.where(eio == e1, 1.0, 0.0), jnp.where(eio == e2, 1.0, 0.0)], axis=1)
    ohb = oh.astype(BF16)
    rank = _dot(ohb, tri_ref[...])
    cnt = _dot(ohb, ones_ref[...]).astype(jnp.int32)
    cnt8 = jnp.left_shift(jnp.right_shift(cnt + (CHUNK - 1), CHUNK_LOG2), CHUNK_LOG2)
    cnt_ref[...] = cnt8
    cnt8f = cnt8.astype(F32)
    off = jnp.zeros((1, 1), F32)
    slot = jnp.zeros((1, 2 * TM), F32)
    for e in range(N_EXPERTS):
        slot = slot + oh[e:e + 1, :] * (off + rank[e:e + 1, :])
        off = off + cnt8f[e:e + 1, 0:1]
    s0, s1 = slot[:, :TM], slot[:, TM:]

    sub = lax.broadcasted_iota(jnp.int32, (8, TM), 0)
    rr_ref[...] = jnp.where(sub == 0, s0, jnp.where(sub == 1, s1, 0.0))
    blk = jnp.where(sub == 0, e1, jnp.where(sub == 1, e2, jnp.where(sub == 2, w1, jnp.where(
        sub == 3, w2, jnp.where(sub == 4, s0, jnp.where(sub == 5, s1, 0.0))))))
    rows = jnp.concatenate([blk, jnp.zeros((LANES - 8, TM), F32)], axis=0)
    rc_ref[...] = rows.T


def _post(xc, mod, fo, ml, sw, wl, n2g, rw_hi, rw_lo, rb_col, nt, nx_tiles, tiles_per_batch):
    rows = nt * TM

    def bidx(i):
        return jnp.where(i < nx_tiles, i // tiles_per_batch, mod.shape[0] - 1)

    def full(a):
        return pl.BlockSpec(a.shape, lambda i: (0,) * a.ndim)

    row = lambda w: pl.BlockSpec((TM, w), lambda i: (i, 0))
    pair = np.arange(2 * TM)
    tri = jnp.asarray(pair[:, None] < pair[None, :], BF16)
    ones = jnp.ones((2 * TM, LANES), BF16)
    consts = [wl["wo_f"], wl["wo_m"], wl["wo_s"], n2g, rw_hi, rw_lo, rb_col, tri, ones]
    return pl.pallas_call(
        _post_kernel,
        grid=(nt,),
        in_specs=[row(D_MODEL), pl.BlockSpec((None, 8, D_MODEL), lambda i: (bidx(i), 0, 0)),
                  row(D_FNET), row(MLA_HEADS * MLA_V), row(SWA_HEADS * SWA_HEAD_DIM)]
                 + [full(a) for a in consts],
        out_specs=[row(D_MODEL), row(D_MODEL), row(LANES),
                   pl.BlockSpec((None, 8, TM), lambda i: (i, 0, 0)),
                   pl.BlockSpec((None, N_EXPERTS, LANES), lambda i: (i, 0, 0))],
        out_shape=[jax.ShapeDtypeStruct((rows, D_MODEL), F32),
                   jax.ShapeDtypeStruct((rows, D_MODEL), BF16),
                   jax.ShapeDtypeStruct((rows, LANES), F32),
                   jax.ShapeDtypeStruct((nt, 8, TM), F32),
                   jax.ShapeDtypeStruct((nt, N_EXPERTS, LANES), jnp.int32)],
        compiler_params=_cparams(("arbitrary",)),
        name="post",
    )(xc, mod, fo, ml, sw, *consts)


def _moe_tables(cnt8, te):
    nt = cnt8.shape[0]
    tile_prefix = jnp.cumsum(cnt8, axis=0) - cnt8
    tot = jnp.sum(cnt8, axis=0)
    tot_e = ((tot + te - 1) // te) * te
    goff = jnp.cumsum(tot_e) - tot_e
    dbase = goff[None, :] + tile_prefix
    nch = cnt8 // CHUNK
    cum = jnp.cumsum(nch, axis=1)
    k = jnp.arange(MAX_CHUNKS, dtype=jnp.int32)[None, :, None]
    owns = jnp.logical_and(k >= (cum - nch)[:, None, :], k < cum[:, None, :])
    dst = jnp.sum(jnp.where(owns, dbase[:, None, :] + CHUNK * (k - (cum - nch)[:, None, :]), 0), axis=-1)
    nchunks = cum[:, -1]
    padch = (tot_e - tot) // CHUNK
    cump = jnp.cumsum(padch)
    kp = jnp.arange(N_EXPERTS * (te // CHUNK), dtype=jnp.int32)[:, None]
    pown = jnp.logical_and(kp >= (cump - padch)[None, :], kp < cump[None, :])
    pdst = jnp.sum(jnp.where(pown, (goff + tot)[None, :] + CHUNK * (kp - (cump - padch)[None, :]), 0), axis=-1)
    npad = cump[-1]
    ntile_cum = jnp.cumsum(tot_e // te)
    nact = ntile_cum[-1]
    return dict(dst=dst.reshape(-1).astype(jnp.int32), nchunks=nchunks.astype(jnp.int32),
                pdst=pdst.astype(jnp.int32), npad=npad.reshape(1).astype(jnp.int32),
                ntile_cum=ntile_cum.astype(jnp.int32), nact=nact.reshape(1).astype(jnp.int32))


def _dispatch_kernel(dst_ref, nch_ref, pdst_ref, npad_ref, h_ref, rr_ref, xs_ref, sbuf, zbuf, sem, zsem):
    i = pl.program_id(0)
    nt = pl.num_programs(0)
    slot = i % 2

    def chunk_copy(sl, k, d):
        return pltpu.make_async_copy(
            sbuf.at[sl, pl.ds(pl.multiple_of(k * CHUNK, CHUNK), CHUNK), :],
            xs_ref.at[pl.ds(pl.multiple_of(d, CHUNK), CHUNK), :], sem.at[sl])

    def wait_tile(t, sl):
        n = nch_ref[t] * CHUNK
        pltpu.make_async_copy(sbuf.at[sl, pl.ds(0, n), :], xs_ref.at[pl.ds(0, n), :], sem.at[sl]).wait()

    def pad_copy(d):
        return pltpu.make_async_copy(zbuf, xs_ref.at[pl.ds(pl.multiple_of(d, CHUNK), CHUNK), :], zsem)

    @pl.when(i == 0)
    def _():
        zbuf[...] = jnp.zeros_like(zbuf)

        def start(k, c):
            pad_copy(pdst_ref[k]).start()
            return c
        lax.fori_loop(0, npad_ref[0], start, 0)

        def wait(k, c):
            pad_copy(0).wait()
            return c
        lax.fori_loop(0, npad_ref[0], wait, 0)

    @pl.when(i >= 2)
    def _():
        wait_tile(i - 2, slot)

    rr = rr_ref[...]
    sio = lax.broadcasted_iota(jnp.int32, (NSLOT, TM), 0).astype(F32)
    psel = jnp.where(jnp.logical_or(sio == rr[0:1, :], sio == rr[1:2, :]), 1.0, 0.0).astype(BF16)
    sbuf[slot] = _dot(psel, h_ref[...]).astype(BF16)

    for k in range(MIN_CHUNKS):
        chunk_copy(slot, k, dst_ref[i * MAX_CHUNKS + k]).start(priority=k % 2)

    def issue(k, c):
        chunk_copy(slot, k, dst_ref[i * MAX_CHUNKS + k]).start()
        return c
    lax.fori_loop(MIN_CHUNKS, nch_ref[i], issue, 0)

    @pl.when(i == nt - 1)
    def _():
        wait_tile(i, slot)

        @pl.when(i >= 1)
        def _():
            wait_tile(i - 1, 1 - slot)


def _dispatch(h2, rr, tb, rows_sorted):
    nt = rr.shape[0]
    return pl.pallas_call(
        _dispatch_kernel,
        grid_spec=pltpu.PrefetchScalarGridSpec(
            num_scalar_prefetch=4,
            grid=(nt,),
            in_specs=[pl.BlockSpec((TM, D_MODEL), lambda i, *_: (i, 0)),
                      pl.BlockSpec((None, 8, TM), lambda i, *_: (i, 0, 0))],
            out_specs=pl.BlockSpec(memory_space=pl.ANY),
            scratch_shapes=[pltpu.VMEM((2, NSLOT, D_MODEL), BF16), pltpu.VMEM((CHUNK, D_MODEL), BF16),
                            pltpu.SemaphoreType.DMA((2,)), pltpu.SemaphoreType.DMA(())]),
        out_shape=jax.ShapeDtypeStruct((rows_sorted, D_MODEL), BF16),
        compiler_params=_cparams(("arbitrary",)),
        name="moe_dispatch",
    )(tb["dst"], tb["nchunks"], tb["pdst"], tb["npad"], h2, rr)


def _expert_kernel(te_ref, na_ref, x_ref, wg_ref, wu_ref, wd_ref, o_ref, wgb, wub, wdb):
    j = pl.program_id(0)
    active = j < na_ref[0]
    fresh = jnp.logical_or(j == 0, te_ref[j] != te_ref[jnp.maximum(j - 1, 0)])

    @pl.when(jnp.logical_and(active, fresh))
    def _():
        wgb[...] = wg_ref[...].astype(BF16)
        wub[...] = wu_ref[...].astype(BF16)
        wdb[...] = wd_ref[...].astype(BF16)

    @pl.when(active)
    def _():
        x = x_ref[...]
        a = _dot(x, wgb[...])
        a = a * jax.nn.sigmoid(a) * _dot(x, wub[...])
        o_ref[...] = _dot(a.astype(BF16), wdb[...]).astype(BF16)


def _experts(xs, wg, wu, wd, layer, tb, te):
    nte = xs.shape[0] // te
    jj = jnp.minimum(jnp.arange(nte, dtype=jnp.int32), tb["nact"][0] - 1)
    tile_e = jnp.sum(jj[:, None] >= tb["ntile_cum"][None, :], axis=-1).astype(jnp.int32)

    def tmap(j, te_ref, na):
        return (jnp.minimum(j, na[0] - 1), 0)

    def wmap(j, te_ref, na):
        return (layer, te_ref[j], 0, 0)

    return pl.pallas_call(
        _expert_kernel,
        grid_spec=pltpu.PrefetchScalarGridSpec(
            num_scalar_prefetch=2,
            grid=(nte,),
            in_specs=[pl.BlockSpec((te, D_MODEL), tmap),
                      pl.BlockSpec((None, None, D_MODEL, D_EXPERT), wmap),
                      pl.BlockSpec((None, None, D_MODEL, D_EXPERT), wmap),
                      pl.BlockSpec((None, None, D_EXPERT, D_MODEL), wmap)],
            out_specs=pl.BlockSpec((te, D_MODEL), tmap),
            scratch_shapes=[pltpu.VMEM((D_MODEL, D_EXPERT), BF16), pltpu.VMEM((D_MODEL, D_EXPERT), BF16),
                            pltpu.VMEM((D_EXPERT, D_MODEL), BF16)]),
        out_shape=jax.ShapeDtypeStruct(xs.shape, BF16),
        compiler_params=_cparams(("arbitrary",)),
        name="moe_experts",
    )(tile_e, tb["nact"], xs, wg, wu, wd)


def _combine_kernel(dst_ref, nch_ref, xn_ref, mod_ref, rc_ref, ys_ref, o_ref, gbuf, sem):
    i = pl.program_id(0)
    nt = pl.num_programs(0)
    slot = i % 2

    def chunk_copy(sl, k, d):
        return pltpu.make_async_copy(
            ys_ref.at[pl.ds(pl.multiple_of(d, CHUNK), CHUNK), :],
            gbuf.at[sl, pl.ds(pl.multiple_of(k * CHUNK, CHUNK), CHUNK), :], sem.at[sl])

    def issue_tile(t, sl):
        for k in range(MIN_CHUNKS):
            chunk_copy(sl, k, dst_ref[t * MAX_CHUNKS + k]).start(priority=k % 2)

        def body(k, c):
            chunk_copy(sl, k, dst_ref[t * MAX_CHUNKS + k]).start()
            return c
        lax.fori_loop(MIN_CHUNKS, nch_ref[t], body, 0)

    @pl.when(i == 0)
    def _():
        gbuf[...] = jnp.zeros_like(gbuf)
        issue_tile(0, 0)

    @pl.when(i + 1 < nt)
    def _():
        issue_tile(i + 1, 1 - slot)

    n = nch_ref[i] * CHUNK
    pltpu.make_async_copy(ys_ref.at[pl.ds(0, n), :], gbuf.at[slot, pl.ds(0, n), :], sem.at[slot]).wait()

    g = gbuf[slot]
    rc = rc_ref[...]
    lio = lax.broadcasted_iota(jnp.int32, (TM, NSLOT), 1).astype(F32)
    p0 = jnp.where(lio == rc[:, 4:5], 1.0, 0.0).astype(BF16)
    p1 = jnp.where(lio == rc[:, 5:6], 1.0, 0.0).astype(BF16)
    y = rc[:, 2:3] * _dot(p0, g) + rc[:, 3:4] * _dot(p1, g)
    o_ref[...] = xn_ref[...] + mod_ref[5:6, :] * y


def _combine(xn, mod, rc, ys, tb, nx_tiles, tiles_per_batch):
    rows = xn.shape[0]
    nt = rows // TM

    def bidx(i, *_):
        return (jnp.where(i < nx_tiles, i // tiles_per_batch, mod.shape[0] - 1), 0, 0)

    return pl.pallas_call(
        _combine_kernel,
        grid_spec=pltpu.PrefetchScalarGridSpec(
            num_scalar_prefetch=2,
            grid=(nt,),
            in_specs=[pl.BlockSpec((TM, D_MODEL), lambda i, *_: (i, 0)),
                      pl.BlockSpec((None, 8, D_MODEL), bidx),
                      pl.BlockSpec((TM, LANES), lambda i, *_: (i, 0)),
                      pl.BlockSpec(memory_space=pl.ANY)],
            out_specs=pl.BlockSpec((TM, D_MODEL), lambda i, *_: (i, 0)),
            scratch_shapes=[pltpu.VMEM((2, NSLOT, D_MODEL), BF16), pltpu.SemaphoreType.DMA((2,))]),
        out_shape=jax.ShapeDtypeStruct((rows, D_MODEL), F32),
        compiler_params=_cparams(("arbitrary",)),
        name="moe_combine",
    )(tb["dst"], tb["nchunks"], xn, mod, rc, ys)


def _moe(h2, rc, rr, cnt, wg, wu, wd, layer, xn, mod, nx_tiles, tiles_per_batch):
    nt = rr.shape[0]
    te = TE
    max_rows = 2 * nt * TM + (CHUNK - 1) * N_EXPERTS * nt + N_EXPERTS * (te - CHUNK)
    rows_sorted = ((max_rows + te - 1) // te) * te
    tb = _moe_tables(cnt[:, :, 0], te)
    xs = _dispatch(h2, rr, tb, rows_sorted)
    ys = _experts(xs, wg, wu, wd, layer, tb, te)
    return _combine(xn, mod, rc, ys, tb, nx_tiles, tiles_per_batch)


def _rope_tables(S, C):
    t = jnp.arange(S)
    rows, cols = (t // GRID_W).astype(F32), (t % GRID_W).astype(F32)

    def axis_tabs(d_rot, lane0, width):
        d_axis = d_rot // 2
        inv = ROPE_THETA ** (-jnp.arange(0, d_axis, 2, dtype=F32) / d_axis)
        ar, ac = rows[:, None] * inv, cols[:, None] * inv
        ang = jnp.concatenate([ar, ar, ac, ac], axis=-1)
        q = d_rot // 4
        first = np.concatenate([np.ones(q), np.zeros(q), np.ones(q), np.zeros(q)]).astype(np.float32)
        pad = ((0, C), (lane0, width - lane0 - d_rot))
        cos = jnp.pad(jnp.cos(ang) - 1.0, pad) + 1.0
        sa = jnp.pad(-jnp.sin(ang) * first, pad)
        sb = jnp.pad(jnp.sin(ang) * (1.0 - first), pad)
        return cos, sa, sb

    cm, sam, sbm = axis_tabs(MLA_ROPE, MLA_NOPE, LANES)
    cs, sas, sbs = axis_tabs(SWA_HEAD_DIM, 0, SWA_HEAD_DIM)
    tile2 = lambda a: jnp.concatenate([a, a], axis=1)
    return dict(cm=cm, sam=sam, sbm=sbm, cs=tile2(cs), sas=tile2(sas), sbs=tile2(sbs))


_TWO_PI_HI = float(np.float32(2.0 * np.pi))
_TWO_PI_LO = float(np.float32(2.0 * np.pi - np.float64(np.float32(2.0 * np.pi))))


def _dft_mats(N):
    k = jnp.arange(N, dtype=jnp.int32)

    def cos_sin(rows, period):
        frac = ((rows[:, None] * k[None, :]) % period).astype(F32) / period
        ang = _TWO_PI_HI * frac + _TWO_PI_LO * frac
        return jnp.cos(ang), jnp.sin(ang)

    if N <= 4 * FNET_CH:
        return cos_sin(k, N)
    A = N // FNET_CH
    c1, s1 = cos_sin(jnp.arange(A, dtype=jnp.int32), A)
    c2, s2 = cos_sin(jnp.arange(FNET_CH, dtype=jnp.int32), N)
    c = c1[:, None, :] * c2[None, :, :] - s1[:, None, :] * s2[None, :, :]
    s = s1[:, None, :] * c2[None, :, :] + c1[:, None, :] * s2[None, :, :]
    return c.reshape(N, N), s.reshape(N, N)


def _dft64_blocks():
    c, s = _dft_mats(FNET_CH)
    eye = jnp.eye(FNET_GROUPS, dtype=F32)
    return jnp.concatenate([jnp.kron(eye, c), jnp.kron(eye, s)], axis=1)


def _with_rotated_columns(w, gain_slot):
    q = MLA_ROPE // 4
    wg = w * jnp.tile(gain_slot, (1, MLA_HEADS))
    lane = np.arange(w.shape[1]) % HEAD_SLOT - MLA_NOPE
    in_rope = (lane >= 0) & (lane < MLA_ROPE)
    first = in_rope & ((lane // q) % 2 == 0)
    second = in_rope & ((lane // q) % 2 == 1)
    rot = jnp.where(first, jnp.roll(wg, -q, axis=1), jnp.where(second, jnp.roll(wg, q, axis=1), 0.0))
    return jnp.concatenate([w, rot], axis=1)


def _layer_weights(l, w_in, fnet_w, mla_cq_g, mla_ckv_g, mla_w_uq, mla_w_uk, mla_w_uv, mla_q_g, mla_k_g,
                   swa_q_g, swa_k_g, swa_sink, w_out):
    D = D_MODEL
    wi = w_in[l]
    o_kr = D_FNET + MLA_Q_RANK + MLA_KV_RANK
    o_qs = o_kr + MLA_ROPE
    o_ks = o_qs + SWA_HEADS * SWA_HEAD_DIM
    o_vs = o_ks + SWA_KV_HEADS * SWA_HEAD_DIM
    order = np.array(SWA_HEAD_ORDER)
    w_qs = wi[:, o_qs:o_ks].reshape(D, SWA_HEADS, SWA_HEAD_DIM)[:, order].reshape(D, -1)
    z = lambda n: jnp.zeros((D, n), F32)
    win = jnp.concatenate([wi[:, :o_kr], w_qs, wi[:, o_ks:o_vs], wi[:, o_vs:],
                           z(MLA_NOPE), wi[:, o_kr:o_qs], z(LANES - MLA_QK)], axis=1)
    pad_slot = lambda w, d: jnp.pad(w.reshape(w.shape[0], MLA_HEADS, d),
                                    ((0, 0), (0, 0), (0, HEAD_SLOT - d))).reshape(w.shape[0], -1)
    wo = w_out[l]
    o_m = D_FNET
    o_s = D_FNET + MLA_HEADS * MLA_V
    wo_s = wo[o_s:].reshape(SWA_HEADS, SWA_HEAD_DIM, D)[order].reshape(-1, D)
    fw = fnet_w[l]
    wblk = jnp.zeros((D_FNET, D_FNET), F32)
    for g in range(FNET_GROUPS):
        wblk = wblk.at[g * FNET_CH:(g + 1) * FNET_CH, g * FNET_CH:(g + 1) * FNET_CH].set(fw[g])
    pad_g = lambda g: jnp.pad(g, (0, HEAD_SLOT - MLA_QK)).reshape(1, HEAD_SLOT)
    return dict(
        w_in=win.astype(BF16),
        cq_g=mla_cq_g[l].reshape(1, -1), ckv_g=mla_ckv_g[l].reshape(1, -1),
        w_uq=_with_rotated_columns(pad_slot(mla_w_uq[l], MLA_QK), pad_g(mla_q_g[l])).astype(BF16),
        w_uk=pad_slot(mla_w_uk[l], MLA_NOPE).astype(BF16),
        w_uv=pad_slot(mla_w_uv[l], MLA_V).astype(BF16),
        mq_g=pad_g(mla_q_g[l]), mk_g=pad_g(mla_k_g[l]),
        sq_g=jnp.tile(swa_q_g[l], 2).reshape(1, LANES), sk_g=jnp.tile(swa_k_g[l], 2).reshape(1, LANES),
        sink=swa_sink[l].reshape(1, SWA_HEADS),
        wo_f=wo[:o_m].astype(BF16), wo_m=wo[o_m:o_s].astype(BF16), wo_s=wo_s.astype(BF16),
        fnet=wblk,
    )


def kernel(x, c, ctx, c_ctx, ada_w, ada_b, norm1_g, norm2_g, w_in, fnet_w, mla_cq_g, mla_ckv_g, mla_w_uq,
           mla_w_uk, mla_w_uv, mla_q_g, mla_k_g, swa_q_g, swa_k_g, swa_sink, w_out, router_w, router_b,
           exp_w_gate, exp_w_up, exp_w_down):
    B, S, D = x.shape
    C = ctx.shape[1]
    L = ada_w.shape[0]
    assert D == D_MODEL and S % 512 == 0 and C == TM and S % TM == 0
    nx_tiles = B * S // TM
    nt_all = nx_tiles + B * C // TM
    tiles_per_batch = S // TM

    prep_tm = PREP_TM if (B * C) % PREP_TM == 0 and S % PREP_TM == 0 else TM
    tabs = _rope_tables(S, prep_tm)
    tabs["dft64"] = _dft64_blocks().astype(BF16)
    dft = {n: tuple(m.astype(BF16) for m in _dft_mats(n)) for n in (S, C)}
    fscale = {n: lax.rsqrt(jnp.full((), n * FNET_CH, F32)) for n in (S, C)}

    nmod = 16
    cvec = jnp.concatenate([c, c_ctx[None, :], jnp.zeros((nmod - B - 1, D), F32)], axis=0)
    mod_all = _adaln(cvec, ada_w, ada_b)
    mod_all = mod_all[:, :B + 1].reshape(L, B + 1, 6, D)
    mod_all = jnp.pad(mod_all, ((0, 0), (0, 0), (0, 2), (0, 0)))

    rw = jnp.pad(router_w, ((0, 0), (0, LANES - N_EXPERTS)))
    rw_hi = rw.astype(BF16)
    rw_lo = jnp.concatenate([rw_hi, (rw - rw_hi.astype(F32)).astype(BF16)], axis=1)
    rb_col = router_b.reshape(N_EXPERTS, 1)

    xc = jnp.concatenate([x.reshape(B * S, D), ctx.reshape(B * C, D)], axis=0)
    for l in range(L):
        last = l == L - 1
        wl = _layer_weights(l, w_in, fnet_w, mla_cq_g, mla_ckv_g, mla_w_uq, mla_w_uk, mla_w_uv, mla_q_g,
                            mla_k_g, swa_q_g, swa_k_g, swa_sink, w_out)
        mod = mod_all[l]
        pr = _prep(xc, mod, norm1_g[l].reshape(1, D), wl, tabs, B * S, S, prep_tm)
        fo = _fourier(dft[S][0], dft[S][1], pr["pp"], (wl["fnet"] * fscale[S]).astype(BF16), B, S, 0)
        ml = _mla_attend(pr["qm"], pr["km"], pr["vm"], B, S, C)
        sw = _swa_attend(wl["sink"], pr["qs"], pr["ks"], pr["vs"], B, S, C)
        if not last:
            fo = _fourier(dft[C][0], dft[C][1], pr["pp"], (wl["fnet"] * fscale[C]).astype(BF16), B, C, B * S,
                          prev=fo)
            ml = _mla_attend(pr["qm"], pr["km"], pr["vm"], B, S, C, prev=ml)
            sw = _swa_attend(wl["sink"], pr["qs"], pr["ks"], pr["vs"], B, S, C, prev=sw)
        nt = nx_tiles if last else nt_all
        xn, h2, rc, rr, cnt = _post(xc, mod, fo, ml, sw, wl, norm2_g[l].reshape(1, D), rw_hi, rw_lo, rb_col,
                                    nt, nx_tiles, tiles_per_batch)
        xc = _moe(h2, rc, rr, cnt, exp_w_gate, exp_w_up, exp_w_down, l, xn, mod, nx_tiles, tiles_per_batch)
    return xc[:B * S].reshape(B, S, D)
```
